```python
import jax, jax.numpy as jnp
from jax import lax
import numpy as np

D_MODEL = 1024
BATCH = 4
SEQ = 4096
DEPTH = 2

N_POOL_LAYERS = DEPTH // 2
N_ATTN_LAYERS = DEPTH - N_POOL_LAYERS
POOL_WINDOWS = (2, 4, 8, 16)
N_POOL_GROUPS = len(POOL_WINDOWS)
POOL_GROUP_DIM = D_MODEL // N_POOL_GROUPS
HEAD_DIM = 64
N_HEADS = D_MODEL // HEAD_DIM
Q_BLOCK = 128
N_EXPERTS = 256
TOP_K = 8
N_EXPERT_GROUPS = 8
TOPK_GROUPS = 4
D_EXPERT = D_MODEL // 4
D_SHARED = D_EXPERT
ROUTED_SCALE = 2.5
MOE_BLOCK = 128
DN_ALPHA = (2 * DEPTH) ** 0.25
DN_BETA = (8 * DEPTH) ** -0.25
LN_EPS = 1e-5
FORGET_BIAS_MEAN = 2.0

kernel_name = "yoco_pool_fox_moe_deepnorm"


def layer_norm(x, g, b):
    xf = x.astype(jnp.float32)
    mu = jnp.mean(xf, axis=-1, keepdims=True)
    var = jnp.mean(jnp.square(xf - mu), axis=-1, keepdims=True)
    return ((xf - mu) * lax.rsqrt(var + LN_EPS) * g.astype(jnp.float32) + b.astype(jnp.float32)).astype(x.dtype)


def multiscale_pool_mixer(x, w_group, scale):
    B, S, D = x.shape
    xg = x.reshape(B, S, N_POOL_GROUPS, POOL_GROUP_DIM).astype(jnp.float32)
    csum = jnp.pad(jnp.cumsum(xg, axis=1), ((0, 0), (1, 0), (0, 0), (0, 0)))
    pos = jnp.arange(1, S + 1, dtype=jnp.float32)
    outs = []
    for g, w in enumerate(POOL_WINDOWS):
        c = csum[:, :, g]
        lower = jnp.pad(c[:, :S + 1 - w], ((0, 0), (w - 1, 0), (0, 0)))
        mean = (c[:, 1:] - lower) / jnp.minimum(pos, float(w))[None, :, None]
        outs.append(mean - xg[:, :, g])
    mixed = jnp.stack(outs, axis=2).astype(x.dtype)
    y = jnp.einsum('bsgc,gcd->bsgd', mixed, w_group)
    return y.reshape(B, S, D) * scale


def shared_kv(xs, w_kvf, b_f):
    B, S, D = xs.shape
    kvf = xs @ w_kvf
    k = kvf[..., :D].reshape(B, S, N_HEADS, HEAD_DIM).transpose(0, 2, 1, 3)
    v = kvf[..., D:2 * D].reshape(B, S, N_HEADS, HEAD_DIM).transpose(0, 2, 1, 3)
    logf = jax.nn.log_sigmoid(kvf[..., 2 * D:].astype(jnp.float32) + b_f.astype(jnp.float32))
    F = jnp.cumsum(logf, axis=1).transpose(0, 2, 1)
    return k, v, F


def forgetting_attention(x, w_q, w_o, kt, vt, Fk):
    B, S, D = x.shape
    nq = S // Q_BLOCK
    q = (x @ w_q).reshape(B, nq, Q_BLOCK, N_HEADS, HEAD_DIM).transpose(1, 0, 3, 2, 4)
    Fq = Fk.reshape(B, N_HEADS, nq, Q_BLOCK).transpose(2, 0, 1, 3)
    kpos = jnp.arange(S)
    sm_scale = HEAD_DIM ** -0.5

    def block(args):
        i, qi, Fqi = args
        logits = jnp.einsum('bhqd,bhkd->bhqk', qi, kt, preferred_element_type=jnp.float32) * sm_scale
        logits = logits + Fqi[..., None] - Fk[:, :, None, :]
        qpos = i * Q_BLOCK + jnp.arange(Q_BLOCK)
        logits = jnp.where(kpos[None, :] <= qpos[:, None], logits, -jnp.inf)
        p = jax.nn.softmax(logits, axis=-1)
        return jnp.einsum('bhqk,bhkd->bhqd', p.astype(vt.dtype), vt)

    o = lax.map(block, (jnp.arange(nq), q, Fq))
    o = o.transpose(1, 0, 3, 2, 4).reshape(B, S, D)
    return o @ w_o


def moe(x2, w_router, b_router, w_gate_up, w_down, ws_gate_up, ws_down):
    N, D = x2.shape
    s = jax.nn.sigmoid((x2 @ w_router).astype(jnp.float32))
    sb = s + b_router.astype(jnp.float32)
    per_group = N_EXPERTS // N_EXPERT_GROUPS
    gscore = lax.top_k(sb.reshape(N, N_EXPERT_GROUPS, per_group), 2)[0].sum(-1)
    _, gidx = lax.top_k(gscore, TOPK_GROUPS)
    gmask = jax.nn.one_hot(gidx, N_EXPERT_GROUPS, dtype=jnp.float32).sum(1) > 0
    sb = jnp.where(jnp.repeat(gmask, per_group, axis=1), sb, -jnp.inf)
    _, eidx = lax.top_k(sb, TOP_K)
    gw = jnp.take_along_axis(s, eidx, axis=1)
    gw = gw / jnp.sum(gw, axis=-1, keepdims=True) * ROUTED_SCALE

    M = N * TOP_K
    n_blocks = -(-(M + N_EXPERTS * (MOE_BLOCK - 1)) // MOE_BLOCK)
    P = n_blocks * MOE_BLOCK
    flat_e = eidx.reshape(M)
    flat_tok = jnp.repeat(jnp.arange(N, dtype=jnp.int32), TOP_K)
    flat_w = gw.reshape(M)
    order = jnp.argsort(flat_e, stable=True)
    se = flat_e[order]
    counts = jnp.bincount(flat_e, length=N_EXPERTS)
    padded = (counts + MOE_BLOCK - 1) // MOE_BLOCK * MOE_BLOCK
    start = jnp.cumsum(counts) - counts
    pend = jnp.cumsum(padded)
    pstart = pend - padded
    dest = pstart[se] + (jnp.arange(M) - start[se])
    buf_tok = jnp.zeros((P,), jnp.int32).at[dest].set(flat_tok[order])
    buf_w = jnp.zeros((P,), x2.dtype).at[dest].set(flat_w[order].astype(x2.dtype))
    block_e = jnp.minimum(jnp.searchsorted(pend, jnp.arange(n_blocks) * MOE_BLOCK, side='right'), N_EXPERTS - 1)

    def body(acc, args):
        e, tok, wt = args
        xb = x2[tok]
        g_, u_ = jnp.split(xb @ w_gate_up[e], 2, axis=-1)
        y = (jax.nn.silu(g_) * u_) @ w_down[e]
        return acc.at[tok].add(y * wt[:, None]), None

    routed, _ = lax.scan(body, jnp.zeros_like(x2),
                         (block_e, buf_tok.reshape(n_blocks, MOE_BLOCK), buf_w.reshape(n_blocks, MOE_BLOCK)))
    sg, su = jnp.split(x2 @ ws_gate_up, 2, axis=-1)
    shared = (jax.nn.silu(sg) * su) @ ws_down
    return routed + shared


def setup_inputs(seed: int = 0) -> dict:
    key = jax.random.key(seed)
    ks = jax.random.split(key, 16)
    D, E, F, FS, H = D_MODEL, N_EXPERTS, D_EXPERT, D_SHARED, N_HEADS
    nrm = jax.random.normal
    x = nrm(ks[0], (BATCH, SEQ, D), jnp.float32)
    pool_w = nrm(ks[1], (N_POOL_LAYERS, N_POOL_GROUPS, POOL_GROUP_DIM, POOL_GROUP_DIM), jnp.float32) * (POOL_GROUP_DIM ** -0.5 * DN_BETA)
    pool_scale = 1.0 + 0.02 * nrm(ks[2], (N_POOL_LAYERS, D), jnp.float32)
    w_q = nrm(ks[3], (N_ATTN_LAYERS, D, D), jnp.float32) * D ** -0.5
    w_o = nrm(ks[4], (N_ATTN_LAYERS, D, D), jnp.float32) * (D ** -0.5 * DN_BETA)
    col_scale = jnp.concatenate([jnp.full((D,), D ** -0.5), jnp.full((D,), D ** -0.5 * DN_BETA), jnp.full((H,), 0.5 * D ** -0.5)]).astype(jnp.float32)
    w_kvf = nrm(ks[5], (D, 2 * D + H), jnp.float32) * col_scale
    b_f = FORGET_BIAS_MEAN + 0.1 * nrm(ks[6], (H,), jnp.float32)
    ln_g = 1.0 + 0.02 * nrm(ks[7], (DEPTH, 2, D), jnp.float32)
    ln_b = 0.02 * nrm(ks[8], (DEPTH, 2, D), jnp.float32)
    router_w = nrm(ks[9], (DEPTH, D, E), jnp.float32) * D ** -0.5
    router_b = 0.01 * nrm(ks[10], (DEPTH, E), jnp.float32)
    w_gate_up = nrm(ks[11], (DEPTH, E, D, 2 * F), jnp.float32) * D ** -0.5
    w_down = nrm(ks[12], (DEPTH, E, F, D), jnp.float32) * (F ** -0.5 * DN_BETA)
    ws_gate_up = nrm(ks[13], (DEPTH, D, 2 * FS), jnp.float32) * D ** -0.5
    ws_down = nrm(ks[14], (DEPTH, FS, D), jnp.float32) * (FS ** -0.5 * DN_BETA)
    return {"x": x, "pool_w": pool_w, "pool_scale": pool_scale, "w_q": w_q, "w_o": w_o,
            "w_kvf": w_kvf, "b_f": b_f, "ln_g": ln_g, "ln_b": ln_b, "router_w": router_w,
            "router_b": router_b, "w_gate_up": w_gate_up, "w_down": w_down,
            "ws_gate_up": ws_gate_up, "ws_down": ws_down}


def reference(x, pool_w, pool_scale, w_q, w_o, w_kvf, b_f, ln_g, ln_b, router_w, router_b,
              w_gate_up, w_down, ws_gate_up, ws_down):
    B, S, D = x.shape
    kt = vt = Fk = None
    for l in range(DEPTH):
        if l < N_POOL_LAYERS:
            mix = multiscale_pool_mixer(x, pool_w[l], pool_scale[l])
        else:
            if l == N_POOL_LAYERS:
                kt, vt, Fk = shared_kv(x, w_kvf, b_f)
            j = l - N_POOL_LAYERS
            mix = forgetting_attention(x, w_q[j], w_o[j], kt, vt, Fk)
        x = layer_norm(DN_ALPHA * x + mix, ln_g[l, 0], ln_b[l, 0])
        ffn = moe(x.reshape(B * S, D), router_w[l], router_b[l], w_gate_up[l], w_down[l],
                  ws_gate_up[l], ws_down[l])
        x = layer_norm(DN_ALPHA * x + ffn.reshape(B, S, D), ln_g[l, 1], ln_b[l, 1])
    return x
```

```python
import functools

import jax
import jax.numpy as jnp
from jax import lax
from jax.experimental import pallas as pl
from jax.experimental.pallas import tpu as pltpu

POOL_WINDOWS = (2, 4, 8, 16)
POOL_HALO = 16
HEAD_DIM = 64
N_EXPERT_GROUPS = 8
TOPK_GROUPS = 4
TOP_K = 8
ROUTED_SCALE = 2.5
MOE_BLOCK = 128
DMA_UNROLL = 8
LN_EPS = 1e-5

LANES = 128
SUBLANES = 8
VMEM_LIMIT = 48 * 1024 * 1024

F32 = jnp.float32
BF16 = jnp.bfloat16
NEG_INF = float("-inf")


def _params(semantics):
    return pltpu.CompilerParams(dimension_semantics=semantics, vmem_limit_bytes=VMEM_LIMIT)


def _layer_norm(z, g, b):
    mu = jnp.mean(z, axis=-1, keepdims=True)
    zc = z - mu
    var = jnp.mean(zc * zc, axis=-1, keepdims=True)
    return zc * lax.rsqrt(var + LN_EPS) * g + b


def _silu(x):
    return x * jax.nn.sigmoid(x)


def _pool_ln_kernel(x_ref, w_ref, scale_ref, g_ref, b_ref, o_ref, ext_ref, *, alpha, tile):
    s = pl.program_id(1)
    d = x_ref.shape[2]
    c = d // len(POOL_WINDOWS)

    @pl.when(s == 0)
    def _():
        ext_ref[0:POOL_HALO, :] = jnp.zeros((POOL_HALO, d), F32)

    x = x_ref[0]
    ext_ref[POOL_HALO:POOL_HALO + tile, :] = x
    pos = (s * tile + 1 + lax.broadcasted_iota(jnp.int32, (tile, 1), 0)).astype(F32)
    ys = []
    for gi, w in enumerate(POOL_WINDOWS):
        cs = slice(gi * c, (gi + 1) * c)
        acc = ext_ref[:, cs]
        span = 1
        while span < w:
            acc = acc + pltpu.roll(acc, span, axis=0)
            span *= 2
        mean = acc[POOL_HALO:, :] / jnp.minimum(pos, float(w))
        mixed = mean - x[:, cs]
        ys.append(jnp.dot(mixed.astype(BF16), w_ref[gi], preferred_element_type=F32))
    y = jnp.concatenate(ys, axis=1) * scale_ref[...]
    o_ref[0] = _layer_norm(alpha * x + y, g_ref[...], b_ref[...])
    ext_ref[0:POOL_HALO, :] = x[tile - POOL_HALO:, :]


def _pool_ln(x, w, scale, g, b, alpha, tile=512):
    bsz, seq, d = x.shape
    ng, c, _ = w.shape
    vec = pl.BlockSpec((1, d), lambda i, j: (0, 0))
    return pl.pallas_call(
        functools.partial(_pool_ln_kernel, alpha=alpha, tile=tile),
        grid=(bsz, seq // tile),
        in_specs=[pl.BlockSpec((1, tile, d), lambda i, j: (i, j, 0)),
                  pl.BlockSpec((ng, c, c), lambda i, j: (0, 0, 0)),
                  vec, vec, vec],
        out_specs=pl.BlockSpec((1, tile, d), lambda i, j: (i, j, 0)),
        out_shape=jax.ShapeDtypeStruct(x.shape, F32),
        scratch_shapes=[pltpu.VMEM((POOL_HALO + tile, d), F32)],
        compiler_params=_params(("arbitrary", "arbitrary")),
        name="pool_ln",
    )(x, w.astype(BF16), scale.reshape(1, d), g.reshape(1, d), b.reshape(1, d))


def _first_argmax(v, iota, size):
    m = jnp.max(v, axis=0, keepdims=True)
    idx = jnp.min(jnp.where(v == m, iota, size), axis=0, keepdims=True)
    return m, idx


def _router_kernel(h_ref, wt_ref, b_ref, eidx_ref, gw_ref, rank_ref, cnt_ref, carry_ref, *, tile):
    i = pl.program_id(0)
    n_e = wt_ref.shape[0]
    per_group = n_e // N_EXPERT_GROUPS

    @pl.when(i == 0)
    def _():
        carry_ref[...] = jnp.zeros_like(carry_ref)

    logits = lax.dot_general(wt_ref[...], h_ref[...], (((1,), (1,)), ((), ())),
                             precision=lax.Precision.HIGHEST, preferred_element_type=F32)
    s = jax.nn.sigmoid(logits)
    sb = s + b_ref[...]

    iota_g = lax.broadcasted_iota(jnp.int32, (per_group, tile), 0)
    gscores = []
    for g in range(N_EXPERT_GROUPS):
        blk = sb[g * per_group:(g + 1) * per_group, :]
        m1, i1 = _first_argmax(blk, iota_g, per_group)
        m2 = jnp.max(jnp.where(iota_g == i1, NEG_INF, blk), axis=0, keepdims=True)
        gscores.append(m1 + m2)
    gs = jnp.concatenate(gscores, axis=0)
    iota_8 = lax.broadcasted_iota(jnp.int32, (N_EXPERT_GROUPS, tile), 0)
    gsel = jnp.zeros((N_EXPERT_GROUPS, tile), jnp.bool_)
    for _ in range(TOPK_GROUPS):
        _, gi = _first_argmax(gs, iota_8, N_EXPERT_GROUPS)
        hit = iota_8 == gi
        gsel = jnp.logical_or(gsel, hit)
        gs = jnp.where(hit, NEG_INF, gs)
    cand = jnp.concatenate(
        [jnp.where(gsel[g:g + 1, :], sb[g * per_group:(g + 1) * per_group, :], NEG_INF)
         for g in range(N_EXPERT_GROUPS)], axis=0)

    iota_e = lax.broadcasted_iota(jnp.int32, (n_e, tile), 0)
    member = jnp.zeros((n_e, tile), F32)
    idxs, svals = [], []
    for _ in range(TOP_K):
        _, ei = _first_argmax(cand, iota_e, n_e)
        hit = iota_e == ei
        idxs.append(ei)
        svals.append(jnp.sum(jnp.where(hit, s, 0.0), axis=0, keepdims=True))
        member = jnp.where(hit, 1.0, member)
        cand = jnp.where(hit, NEG_INF, cand)
    sv = jnp.concatenate(svals, axis=0)
    gw_ref[...] = sv / jnp.sum(sv, axis=0, keepdims=True) * ROUTED_SCALE
    eidx_ref[...] = jnp.concatenate(idxs, axis=0)

    r = lax.broadcasted_iota(jnp.int32, (tile, tile), 0)
    c = lax.broadcasted_iota(jnp.int32, (tile, tile), 1)
    earlier = (r < c).astype(BF16)
    before = jnp.dot(member.astype(BF16), earlier, preferred_element_type=F32) + carry_ref[...]
    ranks = [jnp.sum(jnp.where(iota_e == ei, before, 0.0), axis=0, keepdims=True) for ei in idxs]
    rank_ref[...] = jnp.concatenate(ranks, axis=0).astype(jnp.int32)
    carry_ref[...] = carry_ref[...] + jnp.sum(member, axis=1, keepdims=True)
    cnt_ref[...] = carry_ref[...].astype(jnp.int32)


def _router(h, w_router, b_router, tile=512):
    n, d = h.shape
    n_e = w_router.shape[1]
    out_kt = pl.BlockSpec((TOP_K, tile), lambda i: (0, i))
    return pl.pallas_call(
        functools.partial(_router_kernel, tile=tile),
        grid=(n // tile,),
        in_specs=[pl.BlockSpec((tile, d), lambda i: (i, 0)),
                  pl.BlockSpec((n_e, d), lambda i: (0, 0)),
                  pl.BlockSpec((n_e, 1), lambda i: (0, 0))],
        out_specs=[out_kt, out_kt, out_kt, pl.BlockSpec((n_e, 1), lambda i: (0, 0))],
        out_shape=[jax.ShapeDtypeStruct((TOP_K, n), jnp.int32),
                   jax.ShapeDtypeStruct((TOP_K, n), F32),
                   jax.ShapeDtypeStruct((TOP_K, n), jnp.int32),
                   jax.ShapeDtypeStruct((n_e, 1), jnp.int32)],
        scratch_shapes=[pltpu.VMEM((n_e, 1), F32)],
        compiler_params=_params(("arbitrary",)),
        name="router",
    )(h, w_router.T, b_router.reshape(n_e, 1))


def _for_rows(count, fn):
    groups = count // DMA_UNROLL

    def group(g, carry):
        for u in range(DMA_UNROLL):
            fn(g * DMA_UNROLL + u)
        return carry

    def single(r, carry):
        fn(r)
        return carry

    lax.fori_loop(0, groups, group, 0)
    lax.fori_loop(groups * DMA_UNROLL, count, single, 0)


def _grouped_kernel(be_ref, nu_ref, nv_ref, tok0_ref, tokn_ref, dst_ref, h_hbm, wgu_ref, wd_ref,
                    y_hbm, xbuf, ybuf, wgu_bf, wd_bf, gsem, ssem):
    b = pl.program_id(0)
    n_used = nu_ref[0]
    slot = lax.rem(b, 2)
    f = wd_ref.shape[2]

    def start_gather(tok_ref, s, count):
        def start(r):
            pltpu.make_async_copy(h_hbm.at[pl.ds(tok_ref[0, 0, r], 1)],
                                  xbuf.at[s, pl.ds(r, 1)], gsem.at[s]).start()
        _for_rows(count, start)

    def wait_rows(hbm, buf, sem, s, count):
        tiles = count // SUBLANES
        full = pl.multiple_of(tiles * SUBLANES, SUBLANES)

        @pl.when(tiles > 0)
        def _():
            pltpu.make_async_copy(hbm.at[pl.ds(0, full)], buf.at[s, pl.ds(0, full)],
                                  sem.at[s]).wait()

        def wait_one(r, carry):
            pltpu.make_async_copy(hbm.at[pl.ds(0, 1)], buf.at[s, pl.ds(0, 1)], sem.at[s]).wait()
            return carry
        lax.fori_loop(full, count, wait_one, 0)

    @pl.when(b < n_used)
    def _():
        @pl.when(b == 0)
        def _():
            xbuf[...] = jnp.zeros_like(xbuf)
            start_gather(tok0_ref, 0, nv_ref[0])

        @pl.when(b + 1 < n_used)
        def _():
            start_gather(tokn_ref, 1 - slot, nv_ref[b + 1])

        prev_e = be_ref[jnp.maximum(b - 1, 0)]

        @pl.when(jnp.logical_or(b == 0, be_ref[b] != prev_e))
        def _():
            wgu_bf[...] = wgu_ref[0, 0].astype(BF16)
            wd_bf[...] = wd_ref[0, 0].astype(BF16)

        wait_rows(h_hbm, xbuf, gsem, slot, nv_ref[b])
        x = xbuf[slot].astype(BF16)
        gu = jnp.dot(x, wgu_bf[...], preferred_element_type=F32)
        act = _silu(gu[:, :f]) * gu[:, f:]
        y = jnp.dot(act.astype(BF16), wd_bf[...], preferred_element_type=F32)

        @pl.when(b >= 2)
        def _():
            wait_rows(y_hbm, ybuf, ssem, slot, nv_ref[jnp.maximum(b - 2, 0)])

        ybuf[slot] = y

        def start_scatter(r):
            pltpu.make_async_copy(ybuf.at[slot, pl.ds(r, 1)],
                                  y_hbm.at[pl.ds(dst_ref[0, 0, r], 1)], ssem.at[slot]).start()
        _for_rows(nv_ref[b], start_scatter)

        @pl.when(b == n_used - 1)
        def _():
            wait_rows(y_hbm, ybuf, ssem, slot, nv_ref[b])

            @pl.when(b >= 1)
            def _():
                wait_rows(y_hbm, ybuf, ssem, 1 - slot, nv_ref[jnp.maximum(b - 1, 0)])


def _grouped_experts(h, block_e, n_used, n_valid, src_tok, dst_row, w_gate_up, w_down, layer,
                     n_rows_out):
    n, d = h.shape
    _, n_e, _, f2 = w_gate_up.shape
    f = f2 // 2
    nb = block_e.shape[0]
    last = nb - 1

    def used(b, nu):
        return jnp.minimum(b, nu[0] - 1)

    idx_blk = (1, 1, MOE_BLOCK)
    grid_spec = pltpu.PrefetchScalarGridSpec(
        num_scalar_prefetch=3,
        grid=(nb,),
        in_specs=[
            pl.BlockSpec(idx_blk, lambda b, be, nu, nv: (0, 0, 0), memory_space=pltpu.SMEM),
            pl.BlockSpec(idx_blk, lambda b, be, nu, nv: (jnp.minimum(used(b, nu) + 1, last), 0, 0),
                         memory_space=pltpu.SMEM),
            pl.BlockSpec(idx_blk, lambda b, be, nu, nv: (used(b, nu), 0, 0),
                         memory_space=pltpu.SMEM),
            pl.BlockSpec(memory_space=pl.ANY),
            pl.BlockSpec((1, 1, d, f2), lambda b, be, nu, nv: (layer, be[used(b, nu)], 0, 0)),
            pl.BlockSpec((1, 1, f, d), lambda b, be, nu, nv: (layer, be[used(b, nu)], 0, 0)),
        ],
        out_specs=pl.BlockSpec(memory_space=pl.ANY),
        scratch_shapes=[pltpu.VMEM((2, MOE_BLOCK, d), F32),
                        pltpu.VMEM((2, MOE_BLOCK, d), F32),
                        pltpu.VMEM((d, f2), BF16),
                        pltpu.VMEM((f, d), BF16),
                        pltpu.SemaphoreType.DMA((2,)),
                        pltpu.SemaphoreType.DMA((2,))],
    )
    src3 = src_tok.reshape(nb, 1, MOE_BLOCK)
    return pl.pallas_call(
        _grouped_kernel,
        grid_spec=grid_spec,
        out_shape=jax.ShapeDtypeStruct((n_rows_out, d), F32),
        compiler_params=_params(("arbitrary",)),
        name="grouped_experts",
    )(block_e, n_used, n_valid, src3, src3, dst_row.reshape(nb, 1, MOE_BLOCK), h, w_gate_up,
      w_down)


def _combine_ln_kernel(*refs, alpha):
    h_ref, gw_ref = refs[0], refs[1]
    y_refs = refs[2:2 + TOP_K]
    wsgu_ref, wsd_ref, g_ref, b_ref, o_ref = refs[2 + TOP_K:]
    h = h_ref[...]
    gw = gw_ref[...]
    f = wsd_ref.shape[0]
    routed = gw[:, 0:1] * y_refs[0][...]
    for k in range(1, TOP_K):
        routed = routed + gw[:, k:k + 1] * y_refs[k][...]
    gu = jnp.dot(h.astype(BF16), wsgu_ref[...], preferred_element_type=F32)
    act = _silu(gu[:, :f]) * gu[:, f:]
    shared = jnp.dot(act.astype(BF16), wsd_ref[...], preferred_element_type=F32)
    o_ref[...] = _layer_norm(alpha * h + (routed + shared), g_ref[...], b_ref[...])


def _combine_ln(h, gw, y_rows, ws_gate_up, ws_down, g, b, alpha, tile=256):
    n, d = h.shape
    f2 = ws_gate_up.shape[1]
    nt = n // tile
    vec = pl.BlockSpec((1, d), lambda i: (0, 0))
    y_specs = [pl.BlockSpec((tile, d), functools.partial(lambda i, k: (k * nt + i, 0), k=k))
               for k in range(TOP_K)]
    return pl.pallas_call(
        functools.partial(_combine_ln_kernel, alpha=alpha),
        grid=(nt,),
        in_specs=[pl.BlockSpec((tile, d), lambda i: (i, 0)),
                  pl.BlockSpec((tile, TOP_K), lambda i: (i, 0))] + y_specs +
                 [pl.BlockSpec((d, f2), lambda i: (0, 0)),
                  pl.BlockSpec((f2 // 2, d), lambda i: (0, 0)), vec, vec],
        out_specs=pl.BlockSpec((tile, d), lambda i: (i, 0)),
        out_shape=jax.ShapeDtypeStruct((n, d), F32),
        compiler_params=_params(("arbitrary",)),
        name="combine_ln",
    )(h, gw, *([y_rows] * TOP_K), ws_gate_up.astype(BF16), ws_down.astype(BF16),
      g.reshape(1, d), b.reshape(1, d))


def _moe_ln(h, w_router, b_router, w_gate_up, w_down, layer, ws_gate_up, ws_down, g, b, alpha):
    n, d = h.shape
    n_e = w_router.shape[1]
    eidx, gw, rank, counts = _router(h, w_router, b_router)

    m = n * TOP_K
    nb = -(-(m + n_e * (MOE_BLOCK - 1)) // MOE_BLOCK)
    p = nb * MOE_BLOCK
    counts = counts[:, 0]
    padded = (counts + MOE_BLOCK - 1) // MOE_BLOCK * MOE_BLOCK
    pend = jnp.cumsum(padded)
    pstart = pend - padded
    dest = (pstart[eidx] + rank).reshape(m)
    tok = jnp.broadcast_to(jnp.arange(n, dtype=jnp.int32)[None, :], (TOP_K, n)).reshape(m)
    src_tok = jnp.zeros((p,), jnp.int32).at[dest].set(tok)
    dst_row = jnp.zeros((p,), jnp.int32).at[dest].set(jnp.arange(m, dtype=jnp.int32))
    block_start = jnp.arange(nb, dtype=jnp.int32) * MOE_BLOCK
    block_e = jnp.minimum(jnp.searchsorted(pend, block_start, side="right"),
                          n_e - 1).astype(jnp.int32)
    n_valid = jnp.clip(pstart[block_e] + counts[block_e] - block_start, 0, MOE_BLOCK)
    n_used = (pend[-1:] // MOE_BLOCK).astype(jnp.int32)

    y_rows = _grouped_experts(h, block_e, n_used, n_valid.astype(jnp.int32), src_tok, dst_row,
                              w_gate_up, w_down, layer, m)
    return _combine_ln(h, gw.T, y_rows, ws_gate_up, ws_down, g, b, alpha)


def _kvf_kernel(h_ref, wk_ref, wv_ref, wf_ref, bf_ref, k_ref, v_ref, f_ref, carry_ref, *, tile):
    s = pl.program_id(1)

    @pl.when(s == 0)
    def _():
        carry_ref[...] = jnp.zeros_like(carry_ref)

    h = h_ref[0]
    hb = h.astype(BF16)
    k_ref[0] = jnp.dot(hb, wk_ref[...], preferred_element_type=F32).astype(BF16)
    v_ref[0] = jnp.dot(hb, wv_ref[...], preferred_element_type=F32).astype(BF16)
    z = jnp.dot(h, wf_ref[...], precision=lax.Precision.HIGHEST,
                preferred_element_type=F32) + bf_ref[...]
    logf = jnp.minimum(z, 0.0) - jnp.log1p(jnp.exp(-jnp.abs(z)))
    row = lax.broadcasted_iota(jnp.int32, (tile, 1), 0)
    span = 1
    while span < tile:
        logf = logf + jnp.where(row >= span, pltpu.roll(logf, span, axis=0), 0.0)
        span *= 2
    cum = logf + carry_ref[...]
    f_ref[0] = cum
    carry_ref[...] = cum[tile - 1:tile, :]


def _kvf(h3, w_kvf, b_f, tile=512):
    bsz, seq, d = h3.shape
    n_h = b_f.shape[0]
    hp = LANES
    wk = w_kvf[:, :d].astype(BF16)
    wv = w_kvf[:, d:2 * d].astype(BF16)
    wf = jnp.pad(w_kvf[:, 2 * d:], ((0, 0), (0, hp - n_h)))
    bf = jnp.pad(b_f, (0, hp - n_h)).reshape(1, hp)
    wspec = pl.BlockSpec((d, d), lambda i, j: (0, 0))
    tspec = pl.BlockSpec((1, tile, d), lambda i, j: (i, j, 0))
    k, v, fcum = pl.pallas_call(
        functools.partial(_kvf_kernel, tile=tile),
        grid=(bsz, seq // tile),
        in_specs=[tspec, wspec, wspec,
                  pl.BlockSpec((d, hp), lambda i, j: (0, 0)),
                  pl.BlockSpec((1, hp), lambda i, j: (0, 0))],
        out_specs=[tspec, tspec, pl.BlockSpec((1, tile, hp), lambda i, j: (i, j, 0))],
        out_shape=[jax.ShapeDtypeStruct((bsz, seq, d), BF16),
                   jax.ShapeDtypeStruct((bsz, seq, d), BF16),
                   jax.ShapeDtypeStruct((bsz, seq, hp), F32)],
        scratch_shapes=[pltpu.VMEM((1, hp), F32)],
        compiler_params=_params(("arbitrary", "arbitrary")),
        name="kvf_proj",
    )(h3, wk, wv, wf, bf)
    return k, v, fcum[:, :, :n_h]


def _qproj_kernel(h_ref, wq_ref, q_ref, *, scale):
    q = jnp.dot(h_ref[...].astype(BF16), wq_ref[...], preferred_element_type=F32)
    q_ref[...] = (q * scale).astype(BF16)


def _qproj(h, w_q, scale, tile=512):
    n, d = h.shape
    return pl.pallas_call(
        functools.partial(_qproj_kernel, scale=scale),
        grid=(n // tile,),
        in_specs=[pl.BlockSpec((tile, d), lambda i: (i, 0)),
                  pl.BlockSpec((d, d), lambda i: (0, 0))],
        out_specs=pl.BlockSpec((tile, d), lambda i: (i, 0)),
        out_shape=jax.ShapeDtypeStruct((n, d), BF16),
        compiler_params=_params(("arbitrary",)),
        name="q_proj",
    )(h, w_q.astype(BF16))


def _attn_kernel(q_ref, k_ref, v_ref, fq_ref, fk_ref, o_ref, *, blk):
    i = pl.program_id(2)
    q2 = q_ref[0]
    lane = lax.broadcasted_iota(jnp.int32, (1, LANES), 1)
    first = lane < HEAD_DIM
    qs = (jnp.where(first, q2, jnp.zeros_like(q2)), jnp.where(first, jnp.zeros_like(q2), q2))
    fq = fq_ref[0, 0]
    fqs = (fq[:, 0:1], fq[:, 1:2])
    row = lax.broadcasted_iota(jnp.int32, (blk, blk), 0)
    col = lax.broadcasted_iota(jnp.int32, (blk, blk), 1)
    causal = col <= row

    def step(j, carry, diagonal):
        ms, ls, acc = carry
        start = pl.multiple_of(j * blk, blk)
        kj = k_ref[0, pl.ds(start, blk), :]
        vj = v_ref[0, pl.ds(start, blk), :]
        fk = fk_ref[0, 0, :, pl.ds(start, blk)]
        new_ms, new_ls, alphas, pvs = [], [], [], []
        for hd in range(2):
            sc = lax.dot_general(qs[hd], kj, (((1,), (1,)), ((), ())), preferred_element_type=F32)
            sc = (sc + fqs[hd]) - fk[hd:hd + 1, :]
            if diagonal:
                sc = jnp.where(causal, sc, NEG_INF)
            m_new = jnp.maximum(ms[hd], jnp.max(sc, axis=1, keepdims=True))
            p = jnp.exp(sc - m_new)
            alpha = jnp.exp(ms[hd] - m_new)
            new_ms.append(m_new)
            new_ls.append(alpha * ls[hd] + jnp.sum(p, axis=1, keepdims=True))
            alphas.append(alpha)
            pvs.append(jnp.dot(p.astype(BF16), vj, preferred_element_type=F32))
        acc = jnp.where(first, alphas[0], alphas[1]) * acc + jnp.where(first, pvs[0], pvs[1])
        return tuple(new_ms), tuple(new_ls), acc

    init = ((jnp.full((blk, 1), NEG_INF, F32),) * 2, (jnp.zeros((blk, 1), F32),) * 2,
            jnp.zeros((blk, LANES), F32))
    carry = lax.fori_loop(0, i, functools.partial(step, diagonal=False), init)
    _, ls, acc = step(i, carry, True)
    o_ref[0] = (acc / jnp.where(first, ls[0], ls[1])).astype(BF16)


def _attention(q, k, v, fcum, blk=256):
    bsz, seq, d = q.shape
    n_h = d // HEAD_DIM
    n_pairs = n_h // 2
    f_pairs = fcum.reshape(bsz, seq, n_pairs, 2)
    f_col = f_pairs.transpose(0, 2, 1, 3)
    f_row = f_pairs.transpose(0, 2, 3, 1)
    return pl.pallas_call(
        functools.partial(_attn_kernel, blk=blk),
        grid=(bsz, n_pairs, seq // blk),
        in_specs=[pl.BlockSpec((1, blk, LANES), lambda b, p, i: (b, i, p)),
                  pl.BlockSpec((1, seq, LANES), lambda b, p, i: (b, 0, p)),
                  pl.BlockSpec((1, seq, LANES), lambda b, p, i: (b, 0, p)),
                  pl.BlockSpec((1, 1, blk, 2), lambda b, p, i: (b, p, i, 0)),
                  pl.BlockSpec((1, 1, 2, seq), lambda b, p, i: (b, p, 0, 0))],
        out_specs=pl.BlockSpec((1, blk, LANES), lambda b, p, i: (b, i, p)),
        out_shape=jax.ShapeDtypeStruct((bsz, seq, d), BF16),
        compiler_params=_params(("arbitrary", "arbitrary", "arbitrary")),
        name="fox_attention",
    )(q, k, v, f_col, f_row)


def _oproj_ln_kernel(o_ref, h_ref, wo_ref, g_ref, b_ref, out_ref, *, alpha):
    mix = jnp.dot(o_ref[...], wo_ref[...], preferred_element_type=F32)
    out_ref[...] = _layer_norm(alpha * h_ref[...] + mix, g_ref[...], b_ref[...])


def _oproj_ln(o, h, w_o, g, b, alpha, tile=512):
    n, d = h.shape
    vec = pl.BlockSpec((1, d), lambda i: (0, 0))
    tspec = pl.BlockSpec((tile, d), lambda i: (i, 0))
    return pl.pallas_call(
        functools.partial(_oproj_ln_kernel, alpha=alpha),
        grid=(n // tile,),
        in_specs=[tspec, tspec, pl.BlockSpec((d, d), lambda i: (0, 0)), vec, vec],
        out_specs=tspec,
        out_shape=jax.ShapeDtypeStruct((n, d), F32),
        compiler_params=_params(("arbitrary",)),
        name="oproj_ln",
    )(o, h, w_o.astype(BF16), g.reshape(1, d), b.reshape(1, d))


def kernel(x, pool_w, pool_scale, w_q, w_o, w_kvf, b_f, ln_g, ln_b, router_w, router_b,
           w_gate_up, w_down, ws_gate_up, ws_down):
    bsz, seq, d = x.shape
    depth = ln_g.shape[0]
    n_pool = pool_w.shape[0]
    alpha = float((2 * depth) ** 0.25)
    n = bsz * seq
    k = v = fcum = None
    for l in range(depth):
        if l < n_pool:
            x = _pool_ln(x, pool_w[l], pool_scale[l], ln_g[l, 0], ln_b[l, 0], alpha)
        else:
            if l == n_pool:
                k, v, fcum = _kvf(x, w_kvf, b_f)
            j = l - n_pool
            q = _qproj(x.reshape(n, d), w_q[j], HEAD_DIM ** -0.5).reshape(bsz, seq, d)
            o = _attention(q, k, v, fcum)
            x = _oproj_ln(o.reshape(n, d), x.reshape(n, d), w_o[j], ln_g[l, 0], ln_b[l, 0],
                          alpha).reshape(bsz, seq, d)
        x = _moe_ln(x.reshape(n, d), router_w[l], router_b[l], w_gate_up, w_down, l,
                    ws_gate_up[l], ws_down[l], ln_g[l, 1], ln_b[l, 1], alpha).reshape(bsz, seq, d)
    return x
```

```python
import functools

import jax
import jax.numpy as jnp
from jax import lax
from jax.experimental import pallas as pl
from jax.experimental.pallas import tpu as pltpu

POOL_WINDOWS = (2, 4, 8, 16)
POOL_HALO = 16
HEAD_DIM = 64
N_EXPERT_GROUPS = 8
TOPK_GROUPS = 4
TOP_K = 8
ROUTED_SCALE = 2.5
MOE_BLOCK = 128
DMA_UNROLL = 8
DEST_TILE = 256
LN_EPS = 1e-5

LANES = 128
SUBLANES = 8
VMEM_LIMIT = 48 * 1024 * 1024

F32 = jnp.float32
BF16 = jnp.bfloat16
NEG_INF = float("-inf")


def _params(semantics):
    return pltpu.CompilerParams(dimension_semantics=semantics, vmem_limit_bytes=VMEM_LIMIT)


def _layer_norm(z, g, b):
    mu = jnp.mean(z, axis=-1, keepdims=True)
    zc = z - mu
    var = jnp.mean(zc * zc, axis=-1, keepdims=True)
    return zc * lax.rsqrt(var + LN_EPS) * g + b


def _silu(x):
    return x * jax.nn.sigmoid(x)


def _pool_ln_kernel(x_ref, w_ref, scale_ref, g_ref, b_ref, o_ref, ext_ref, *, alpha, tile):
    s = pl.program_id(1)
    d = x_ref.shape[2]
    c = d // len(POOL_WINDOWS)

    @pl.when(s == 0)
    def _():
        ext_ref[0:POOL_HALO, :] = jnp.zeros((POOL_HALO, d), F32)

    x = x_ref[0]
    ext_ref[POOL_HALO:POOL_HALO + tile, :] = x
    pos = (s * tile + 1 + lax.broadcasted_iota(jnp.int32, (tile, 1), 0)).astype(F32)
    ys = []
    for gi, w in enumerate(POOL_WINDOWS):
        cs = slice(gi * c, (gi + 1) * c)
        acc = ext_ref[:, cs]
        span = 1
        while span < w:
            acc = acc + pltpu.roll(acc, span, axis=0)
            span *= 2
        mean = acc[POOL_HALO:, :] / jnp.minimum(pos, float(w))
        mixed = mean - x[:, cs]
        ys.append(jnp.dot(mixed.astype(BF16), w_ref[gi], preferred_element_type=F32))
    y = jnp.concatenate(ys, axis=1) * scale_ref[...]
    o_ref[0] = _layer_norm(alpha * x + y, g_ref[...], b_ref[...])
    ext_ref[0:POOL_HALO, :] = x[tile - POOL_HALO:, :]


def _pool_ln(x, w, scale, g, b, alpha, tile=512):
    bsz, seq, d = x.shape
    ng, c, _ = w.shape
    vec = pl.BlockSpec((1, d), lambda i, j: (0, 0))
    return pl.pallas_call(
        functools.partial(_pool_ln_kernel, alpha=alpha, tile=tile),
        grid=(bsz, seq // tile),
        in_specs=[pl.BlockSpec((1, tile, d), lambda i, j: (i, j, 0)),
                  pl.BlockSpec((ng, c, c), lambda i, j: (0, 0, 0)),
                  vec, vec, vec],
        out_specs=pl.BlockSpec((1, tile, d), lambda i, j: (i, j, 0)),
        out_shape=jax.ShapeDtypeStruct(x.shape, F32),
        scratch_shapes=[pltpu.VMEM((POOL_HALO + tile, d), F32)],
        compiler_params=_params(("arbitrary", "arbitrary")),
        name="pool_ln",
    )(x, w.astype(BF16), scale.reshape(1, d), g.reshape(1, d), b.reshape(1, d))


def _first_argmax(v, iota, size):
    m = jnp.max(v, axis=0, keepdims=True)
    idx = jnp.min(jnp.where(v == m, iota, size), axis=0, keepdims=True)
    return m, idx


def _router_kernel(h_ref, wt_ref, b_ref, eidx_ref, gw_ref, rank_ref, cnt_ref, carry_ref, *, tile):
    i = pl.program_id(0)
    n_e = wt_ref.shape[0]
    per_group = n_e // N_EXPERT_GROUPS

    @pl.when(i == 0)
    def _():
        carry_ref[...] = jnp.zeros_like(carry_ref)

    logits = lax.dot_general(wt_ref[...], h_ref[...], (((1,), (1,)), ((), ())),
                             precision=lax.Precision.HIGHEST, preferred_element_type=F32)
    s = jax.nn.sigmoid(logits)
    sb = s + b_ref[...]

    iota_g = lax.broadcasted_iota(jnp.int32, (per_group, tile), 0)
    gscores = []
    for g in range(N_EXPERT_GROUPS):
        blk = sb[g * per_group:(g + 1) * per_group, :]
        m1, i1 = _first_argmax(blk, iota_g, per_group)
        m2 = jnp.max(jnp.where(iota_g == i1, NEG_INF, blk), axis=0, keepdims=True)
        gscores.append(m1 + m2)
    gs = jnp.concatenate(gscores, axis=0)
    iota_8 = lax.broadcasted_iota(jnp.int32, (N_EXPERT_GROUPS, tile), 0)
    gsel = jnp.zeros((N_EXPERT_GROUPS, tile), jnp.bool_)
    for _ in range(TOPK_GROUPS):
        _, gi = _first_argmax(gs, iota_8, N_EXPERT_GROUPS)
        hit = iota_8 == gi
        gsel = jnp.logical_or(gsel, hit)
        gs = jnp.where(hit, NEG_INF, gs)
    cand = jnp.concatenate(
        [jnp.where(gsel[g:g + 1, :], sb[g * per_group:(g + 1) * per_group, :], NEG_INF)
         for g in range(N_EXPERT_GROUPS)], axis=0)

    iota_e = lax.broadcasted_iota(jnp.int32, (n_e, tile), 0)
    member = jnp.zeros((n_e, tile), F32)
    idxs, svals = [], []
    for _ in range(TOP_K):
        _, ei = _first_argmax(cand, iota_e, n_e)
        hit = iota_e == ei
        idxs.append(ei)
        svals.append(jnp.sum(jnp.where(hit, s, 0.0), axis=0, keepdims=True))
        member = jnp.where(hit, 1.0, member)
        cand = jnp.where(hit, NEG_INF, cand)
    sv = jnp.concatenate(svals, axis=0)
    gw_ref[...] = sv / jnp.sum(sv, axis=0, keepdims=True) * ROUTED_SCALE
    eidx_ref[...] = jnp.concatenate(idxs, axis=0)

    r = lax.broadcasted_iota(jnp.int32, (tile, tile), 0)
    c = lax.broadcasted_iota(jnp.int32, (tile, tile), 1)
    earlier = (r < c).astype(BF16)
    before = jnp.dot(member.astype(BF16), earlier, preferred_element_type=F32) + carry_ref[...]
    ranks = [jnp.sum(jnp.where(iota_e == ei, before, 0.0), axis=0, keepdims=True) for ei in idxs]
    rank_ref[...] = jnp.concatenate(ranks, axis=0).astype(jnp.int32)
    carry_ref[...] = carry_ref[...] + jnp.sum(member, axis=1, keepdims=True)
    cnt_ref[...] = carry_ref[...].astype(jnp.int32)


def _router(h, w_router, b_router, tile=512):
    n, d = h.shape
    n_e = w_router.shape[1]
    out_kt = pl.BlockSpec((TOP_K, tile), lambda i: (0, i))
    return pl.pallas_call(
        functools.partial(_router_kernel, tile=tile),
        grid=(n // tile,),
        in_specs=[pl.BlockSpec((tile, d), lambda i: (i, 0)),
                  pl.BlockSpec((n_e, d), lambda i: (0, 0)),
                  pl.BlockSpec((n_e, 1), lambda i: (0, 0))],
        out_specs=[out_kt, out_kt, out_kt, pl.BlockSpec((n_e, 1), lambda i: (0, 0))],
        out_shape=[jax.ShapeDtypeStruct((TOP_K, n), jnp.int32),
                   jax.ShapeDtypeStruct((TOP_K, n), F32),
                   jax.ShapeDtypeStruct((TOP_K, n), jnp.int32),
                   jax.ShapeDtypeStruct((n_e, 1), jnp.int32)],
        scratch_shapes=[pltpu.VMEM((n_e, 1), F32)],
        compiler_params=_params(("arbitrary",)),
        name="router",
    )(h, w_router.T, b_router.reshape(n_e, 1))


def _dest_kernel(eidx_ref, rank_ref, pstart_ref, dest_ref):
    n_e = pstart_ref.shape[0]
    tile = eidx_ref.shape[1]
    iota_e = lax.broadcasted_iota(jnp.int32, (n_e, tile), 0)
    starts = pstart_ref[...]
    rows = [jnp.sum(jnp.where(iota_e == eidx_ref[k:k + 1, :], starts, 0), axis=0, keepdims=True)
            for k in range(TOP_K)]
    dest_ref[0] = jnp.concatenate(rows, axis=0) + rank_ref[...]


def _dest_rows(eidx, rank, pstart):
    n = eidx.shape[1]
    n_e = pstart.shape[0]
    kt = pl.BlockSpec((TOP_K, DEST_TILE), lambda i: (0, i))
    return pl.pallas_call(
        _dest_kernel,
        grid=(n // DEST_TILE,),
        in_specs=[kt, kt, pl.BlockSpec((n_e, 1), lambda i: (0, 0))],
        out_specs=pl.BlockSpec((1, TOP_K, DEST_TILE), lambda i: (i, 0, 0)),
        out_shape=jax.ShapeDtypeStruct((n // DEST_TILE, TOP_K, DEST_TILE), jnp.int32),
        compiler_params=_params(("arbitrary",)),
        name="dest_rows",
    )(eidx, rank, pstart.reshape(n_e, 1))


def _dispatch_kernel(dest_ref, h_ref, xg_hbm, sem):
    tiles = dest_ref.shape[0]
    for sub in range(tiles):
        def group(g, carry, sub=sub):
            off = pl.multiple_of(g * DMA_UNROLL, DMA_UNROLL)
            for u in range(DMA_UNROLL):
                for k in range(TOP_K):
                    pltpu.make_async_copy(h_ref.at[pl.ds(sub * DEST_TILE + off + u, 1)],
                                          xg_hbm.at[pl.ds(dest_ref[sub, k, off + u], 1)],
                                          sem).start()
            return carry
        lax.fori_loop(0, DEST_TILE // DMA_UNROLL, group, 0)
    for _ in range(tiles * TOP_K):
        pltpu.make_async_copy(xg_hbm.at[pl.ds(0, DEST_TILE)], xg_hbm.at[pl.ds(0, DEST_TILE)],
                              sem).wait()


def _dispatch(dest, h, n_rows, tiles=2):
    n, d = h.shape
    rows = tiles * DEST_TILE
    return pl.pallas_call(
        _dispatch_kernel,
        grid=(n // rows,),
        in_specs=[pl.BlockSpec((tiles, TOP_K, DEST_TILE), lambda i: (i, 0, 0),
                               memory_space=pltpu.SMEM),
                  pl.BlockSpec((rows, d), lambda i: (i, 0))],
        out_specs=pl.BlockSpec(memory_space=pl.ANY),
        out_shape=jax.ShapeDtypeStruct((n_rows, d), F32),
        scratch_shapes=[pltpu.SemaphoreType.DMA(())],
        compiler_params=_params(("arbitrary",)),
        name="dispatch_rows",
    )(dest, h)


def _grouped_kernel(be_ref, nu_ref, nv_ref, x_ref, wgu_ref, wd_ref, y_ref, wgu_bf, wd_bf):
    b = pl.program_id(0)
    f = wd_ref.shape[2]

    @pl.when(b < nu_ref[0])
    def _():
        prev_e = be_ref[jnp.maximum(b - 1, 0)]

        @pl.when(jnp.logical_or(b == 0, be_ref[b] != prev_e))
        def _():
            wgu_bf[...] = wgu_ref[0, 0].astype(BF16)
            wd_bf[...] = wd_ref[0, 0].astype(BF16)

        row = lax.broadcasted_iota(jnp.int32, (MOE_BLOCK, 1), 0)
        x = jnp.where(row < nv_ref[b], x_ref[...], 0.0).astype(BF16)
        gu = jnp.dot(x, wgu_bf[...], preferred_element_type=F32)
        act = _silu(gu[:, :f]) * gu[:, f:]
        y_ref[...] = jnp.dot(act.astype(BF16), wd_bf[...], preferred_element_type=F32)

    @pl.when(b >= nu_ref[0])
    def _():
        y_ref[...] = jnp.zeros_like(y_ref)


def _grouped_experts(xg, block_e, n_used, n_valid, w_gate_up, w_down, layer):
    p, d = xg.shape
    f2 = w_gate_up.shape[3]
    f = f2 // 2

    def used(b, nu):
        return jnp.minimum(b, nu[0] - 1)

    grid_spec = pltpu.PrefetchScalarGridSpec(
        num_scalar_prefetch=3,
        grid=(p // MOE_BLOCK,),
        in_specs=[
            pl.BlockSpec((MOE_BLOCK, d), lambda b, be, nu, nv: (used(b, nu), 0)),
            pl.BlockSpec((1, 1, d, f2), lambda b, be, nu, nv: (layer, be[used(b, nu)], 0, 0)),
            pl.BlockSpec((1, 1, f, d), lambda b, be, nu, nv: (layer, be[used(b, nu)], 0, 0)),
        ],
        out_specs=pl.BlockSpec((MOE_BLOCK, d), lambda b, be, nu, nv: (b, 0)),
        scratch_shapes=[pltpu.VMEM((d, f2), BF16), pltpu.VMEM((f, d), BF16)],
    )
    return pl.pallas_call(
        _grouped_kernel,
        grid_spec=grid_spec,
        out_shape=jax.ShapeDtypeStruct((p, d), F32),
        compiler_params=_params(("arbitrary",)),
        name="grouped_experts",
    )(block_e, n_used, n_valid, xg, w_gate_up, w_down)


def _combine_ln_kernel(dfirst_ref, dnext_ref, h_ref, gw_ref, y_hbm, wsgu_ref, wsd_ref, g_ref, b_ref,
                       o_ref, ybuf, sem, *, alpha):
    i = pl.program_id(0)
    nt = pl.num_programs(0)
    slot = lax.rem(i, 2)
    f = wsd_ref.shape[0]

    def start_gather(dref, s):
        def group(g, carry):
            off = pl.multiple_of(g * DMA_UNROLL, DMA_UNROLL)
            for u in range(DMA_UNROLL):
                for k in range(TOP_K):
                    pltpu.make_async_copy(y_hbm.at[pl.ds(dref[0, k, off + u], 1)],
                                          ybuf.at[s, k, pl.ds(off + u, 1)], sem.at[s]).start()
            return carry
        lax.fori_loop(0, DEST_TILE // DMA_UNROLL, group, 0)

    @pl.when(i == 0)
    def _():
        start_gather(dfirst_ref, 0)

    @pl.when(i + 1 < nt)
    def _():
        start_gather(dnext_ref, 1 - slot)

    h = h_ref[...]
    gu = jnp.dot(h.astype(BF16), wsgu_ref[...], preferred_element_type=F32)
    act = _silu(gu[:, :f]) * gu[:, f:]
    shared = jnp.dot(act.astype(BF16), wsd_ref[...], preferred_element_type=F32)

    for k in range(TOP_K):
        pltpu.make_async_copy(y_hbm.at[pl.ds(0, DEST_TILE)], ybuf.at[slot, k], sem.at[slot]).wait()
    gw = gw_ref[...]
    routed = gw[:, 0:1] * ybuf[slot, 0]
    for k in range(1, TOP_K):
        routed = routed + gw[:, k:k + 1] * ybuf[slot, k]
    o_ref[...] = _layer_norm(alpha * h + (routed + shared), g_ref[...], b_ref[...])


def _combine_ln(dest, h, gw, y_rows, ws_gate_up, ws_down, g, b, alpha):
    n, d = h.shape
    f2 = ws_gate_up.shape[1]
    nt = n // DEST_TILE
    vec = pl.BlockSpec((1, d), lambda i: (0, 0))
    dblk = (1, TOP_K, DEST_TILE)
    return pl.pallas_call(
        functools.partial(_combine_ln_kernel, alpha=alpha),
        grid=(nt,),
        in_specs=[pl.BlockSpec(dblk, lambda i: (0, 0, 0), memory_space=pltpu.SMEM),
                  pl.BlockSpec(dblk, lambda i: (jnp.minimum(i + 1, nt - 1), 0, 0),
                               memory_space=pltpu.SMEM),
                  pl.BlockSpec((DEST_TILE, d), lambda i: (i, 0)),
                  pl.BlockSpec((DEST_TILE, TOP_K), lambda i: (i, 0)),
                  pl.BlockSpec(memory_space=pl.ANY),
                  pl.BlockSpec((d, f2), lambda i: (0, 0)),
                  pl.BlockSpec((f2 // 2, d), lambda i: (0, 0)), vec, vec],
        out_specs=pl.BlockSpec((DEST_TILE, d), lambda i: (i, 0)),
        out_shape=jax.ShapeDtypeStruct((n, d), F32),
        scratch_shapes=[pltpu.VMEM((2, TOP_K, DEST_TILE, d), F32),
                        pltpu.SemaphoreType.DMA((2,))],
        compiler_params=_params(("arbitrary",)),
        name="combine_ln",
    )(dest, dest, h, gw, y_rows, ws_gate_up.astype(BF16), ws_down.astype(BF16),
      g.reshape(1, d), b.reshape(1, d))


def _moe_ln(h, w_router, b_router, w_gate_up, w_down, layer, ws_gate_up, ws_down, g, b, alpha):
    n, d = h.shape
    n_e = w_router.shape[1]
    eidx, gw, rank, counts = _router(h, w_router, b_router)

    nb = -(-(n * TOP_K + n_e * (MOE_BLOCK - 1)) // MOE_BLOCK)
    counts = counts[:, 0]
    padded = (counts + MOE_BLOCK - 1) // MOE_BLOCK * MOE_BLOCK
    pend = jnp.cumsum(padded)
    pstart = pend - padded
    block_start = jnp.arange(nb, dtype=jnp.int32) * MOE_BLOCK
    block_e = jnp.minimum(jnp.sum(pend[None, :] <= block_start[:, None], axis=1),
                          n_e - 1).astype(jnp.int32)
    n_valid = jnp.clip(pstart[block_e] + counts[block_e] - block_start, 0, MOE_BLOCK)
    n_used = (pend[-1:] // MOE_BLOCK).astype(jnp.int32)

    dest = _dest_rows(eidx, rank, pstart.astype(jnp.int32))
    xg = _dispatch(dest, h, nb * MOE_BLOCK)
    y_rows = _grouped_experts(xg, block_e, n_used, n_valid.astype(jnp.int32), w_gate_up, w_down,
                              layer)
    return _combine_ln(dest, h, gw.T, y_rows, ws_gate_up, ws_down, g, b, alpha)


def _kvf_kernel(h_ref, wk_ref, wv_ref, wf_ref, bf_ref, k_ref, v_ref, f_ref, carry_ref, *, tile):
    s = pl.program_id(1)

    @pl.when(s == 0)
    def _():
        carry_ref[...] = jnp.zeros_like(carry_ref)

    h = h_ref[0]
    hb = h.astype(BF16)
    k_ref[0] = jnp.dot(hb, wk_ref[...], preferred_element_type=F32).astype(BF16)
    v_ref[0] = jnp.dot(hb, wv_ref[...], preferred_element_type=F32).astype(BF16)
    z = jnp.dot(h, wf_ref[...], precision=lax.Precision.HIGHEST,
                preferred_element_type=F32) + bf_ref[...]
    logf = jnp.minimum(z, 0.0) - jnp.log1p(jnp.exp(-jnp.abs(z)))
    row = lax.broadcasted_iota(jnp.int32, (tile, 1), 0)
    span = 1
    while span < tile:
        logf = logf + jnp.where(row >= span, pltpu.roll(logf, span, axis=0), 0.0)
        span *= 2
    cum = logf + carry_ref[...]
    f_ref[0] = cum
    carry_ref[...] = cum[tile - 1:tile, :]


def _kvf(h3, w_kvf, b_f, tile=512):
    bsz, seq, d = h3.shape
    n_h = b_f.shape[0]
    hp = LANES
    wk = w_kvf[:, :d].astype(BF16)
    wv = w_kvf[:, d:2 * d].astype(BF16)
    wf = jnp.pad(w_kvf[:, 2 * d:], ((0, 0), (0, hp - n_h)))
    bf = jnp.pad(b_f, (0, hp - n_h)).reshape(1, hp)
    wspec = pl.BlockSpec((d, d), lambda i, j: (0, 0))
    tspec = pl.BlockSpec((1, tile, d), lambda i, j: (i, j, 0))
    k, v, fcum = pl.pallas_call(
        functools.partial(_kvf_kernel, tile=tile),
        grid=(bsz, seq // tile),
        in_specs=[tspec, wspec, wspec,
                  pl.BlockSpec((d, hp), lambda i, j: (0, 0)),
                  pl.BlockSpec((1, hp), lambda i, j: (0, 0))],
        out_specs=[tspec, tspec, pl.BlockSpec((1, tile, hp), lambda i, j: (i, j, 0))],
        out_shape=[jax.ShapeDtypeStruct((bsz, seq, d), BF16),
                   jax.ShapeDtypeStruct((bsz, seq, d), BF16),
                   jax.ShapeDtypeStruct((bsz, seq, hp), F32)],
        scratch_shapes=[pltpu.VMEM((1, hp), F32)],
        compiler_params=_params(("arbitrary", "arbitrary")),
        name="kvf_proj",
    )(h3, wk, wv, wf, bf)
    return k, v, fcum[:, :, :n_h]


def _qproj_kernel(h_ref, wq_ref, q_ref, *, scale):
    q = jnp.dot(h_ref[...].astype(BF16), wq_ref[...], preferred_element_type=F32)
    q_ref[...] = (q * scale).astype(BF16)


def _qproj(h, w_q, scale, tile=512):
    n, d = h.shape
    return pl.pallas_call(
        functools.partial(_qproj_kernel, scale=scale),
        grid=(n // tile,),
        in_specs=[pl.BlockSpec((tile, d), lambda i: (i, 0)),
                  pl.BlockSpec((d, d), lambda i: (0, 0))],
        out_specs=pl.BlockSpec((tile, d), lambda i: (i, 0)),
        out_shape=jax.ShapeDtypeStruct((n, d), BF16),
        compiler_params=_params(("arbitrary",)),
        name="q_proj",
    )(h, w_q.astype(BF16))


def _attn_kernel(q_ref, k_ref, v_ref, fq_ref, fk_ref, o_ref, *, blk):
    i = pl.program_id(2)
    q2 = q_ref[0]
    lane = lax.broadcasted_iota(jnp.int32, (1, LANES), 1)
    first = lane < HEAD_DIM
    qs = (jnp.where(first, q2, jnp.zeros_like(q2)), jnp.where(first, jnp.zeros_like(q2), q2))
    fq = fq_ref[0, 0]
    fqs = (fq[:, 0:1], fq[:, 1:2])
    row = lax.broadcasted_iota(jnp.int32, (blk, blk), 0)
    col = lax.broadcasted_iota(jnp.int32, (blk, blk), 1)
    causal = col <= row

    def step(j, carry, diagonal):
        ms, ls, acc = carry
        start = pl.multiple_of(j * blk, blk)
        kj = k_ref[0, pl.ds(start, blk), :]
        vj = v_ref[0, pl.ds(start, blk), :]
        fk = fk_ref[0, 0, :, pl.ds(start, blk)]
        new_ms, new_ls, alphas, pvs = [], [], [], []
        for hd in range(2):
            sc = lax.dot_general(qs[hd], kj, (((1,), (1,)), ((), ())), preferred_element_type=F32)
            sc = (sc + fqs[hd]) - fk[hd:hd + 1, :]
            if diagonal:
                sc = jnp.where(causal, sc, NEG_INF)
            m_new = jnp.maximum(ms[hd], jnp.max(sc, axis=1, keepdims=True))
            p = jnp.exp(sc - m_new)
            alpha = jnp.exp(ms[hd] - m_new)
            new_ms.append(m_new)
            new_ls.append(alpha * ls[hd] + jnp.sum(p, axis=1, keepdims=True))
            alphas.append(alpha)
            pvs.append(jnp.dot(p.astype(BF16), vj, preferred_element_type=F32))
        acc = jnp.where(first, alphas[0], alphas[1]) * acc + jnp.where(first, pvs[0], pvs[1])
        return tuple(new_ms), tuple(new_ls), acc

    init = ((jnp.full((blk, 1), NEG_INF, F32),) * 2, (jnp.zeros((blk, 1), F32),) * 2,
            jnp.zeros((blk, LANES), F32))
    carry = lax.fori_loop(0, i, functools.partial(step, diagonal=False), init)
    _, ls, acc = step(i, carry, True)
    o_ref[0] = (acc / jnp.where(first, ls[0], ls[1])).astype(BF16)


def _attention(q, k, v, fcum, blk=256):
    bsz, seq, d = q.shape
    n_h = d // HEAD_DIM
    n_pairs = n_h // 2
    f_pairs = fcum.reshape(bsz, seq, n_pairs, 2)
    f_col = f_pairs.transpose(0, 2, 1, 3)
    f_row = f_pairs.transpose(0, 2, 3, 1)
    return pl.pallas_call(
        functools.partial(_attn_kernel, blk=blk),
        grid=(bsz, n_pairs, seq // blk),
        in_specs=[pl.BlockSpec((1, blk, LANES), lambda b, p, i: (b, i, p)),
                  pl.BlockSpec((1, seq, LANES), lambda b, p, i: (b, 0, p)),
                  pl.BlockSpec((1, seq, LANES), lambda b, p, i: (b, 0, p)),
                  pl.BlockSpec((1, 1, blk, 2), lambda b, p, i: (b, p, i, 0)),
                  pl.BlockSpec((1, 1, 2, seq), lambda b, p, i: (b, p, 0, 0))],
        out_specs=pl.BlockSpec((1, blk, LANES), lambda b, p, i: (b, i, p)),
        out_shape=jax.ShapeDtypeStruct((bsz, seq, d), BF16),
        compiler_params=_params(("arbitrary", "arbitrary", "arbitrary")),
        name="fox_attention",
    )(q, k, v, f_col, f_row)


def _oproj_ln_kernel(o_ref, h_ref, wo_ref, g_ref, b_ref, out_ref, *, alpha):
    mix = jnp.dot(o_ref[...], wo_ref[...], preferred_element_type=F32)
    out_ref[...] = _layer_norm(alpha * h_ref[...] + mix, g_ref[...], b_ref[...])


def _oproj_ln(o, h, w_o, g, b, alpha, tile=512):
    n, d = h.shape
    vec = pl.BlockSpec((1, d), lambda i: (0, 0))
    tspec = pl.BlockSpec((tile, d), lambda i: (i, 0))
    return pl.pallas_call(
        functools.partial(_oproj_ln_kernel, alpha=alpha),
        grid=(n // tile,),
        in_specs=[tspec, tspec, pl.BlockSpec((d, d), lambda i: (0, 0)), vec, vec],
        out_specs=tspec,
        out_shape=jax.ShapeDtypeStruct((n, d), F32),
        compiler_params=_params(("arbitrary",)),
        name="oproj_ln",
    )(o, h, w_o.astype(BF16), g.reshape(1, d), b.reshape(1, d))


def kernel(x, pool_w, pool_scale, w_q, w_o, w_kvf, b_f, ln_g, ln_b, router_w, router_b,
           w_gate_up, w_down, ws_gate_up, ws_down):
    bsz, seq, d = x.shape
    depth = ln_g.shape[0]
    n_pool = pool_w.shape[0]
    alpha = float((2 * depth) ** 0.25)
    n = bsz * seq
    k = v = fcum = None
    for l in range(depth):
        if l < n_pool:
            x = _pool_ln(x, pool_w[l], pool_scale[l], ln_g[l, 0], ln_b[l, 0], alpha)
        else:
            if l == n_pool:
                k, v, fcum = _kvf(x, w_kvf, b_f)
            j = l - n_pool
            q = _qproj(x.reshape(n, d), w_q[j], HEAD_DIM ** -0.5).reshape(bsz, seq, d)
            o = _attention(q, k, v, fcum)
            x = _oproj_ln(o.reshape(n, d), x.reshape(n, d), w_o[j], ln_g[l, 0], ln_b[l, 0],
                          alpha).reshape(bsz, seq, d)
        x = _moe_ln(x.reshape(n, d), router_w[l], router_b[l], w_gate_up, w_down, l,
                    ws_gate_up[l], ws_down[l], ln_g[l, 1], ln_b[l, 1], alpha).reshape(bsz, seq, d)
    return x
```

```python
import functools

import jax
import jax.numpy as jnp
from jax import lax
from jax.experimental import pallas as pl
from jax.experimental.pallas import tpu as pltpu

POOL_WINDOWS = (2, 4, 8, 16)
POOL_HALO = 16
HEAD_DIM = 64
N_EXPERT_GROUPS = 8
TOPK_GROUPS = 4
TOP_K = 8
ROUTED_SCALE = 2.5
MOE_BLOCK = 256
DMA_UNROLL = 8
DEST_TILE = 256
LN_EPS = 1e-5

LANES = 128
SUBLANES = 8
VMEM_LIMIT = 48 * 1024 * 1024

F32 = jnp.float32
BF16 = jnp.bfloat16
NEG_INF = float("-inf")


def _params(semantics):
    return pltpu.CompilerParams(dimension_semantics=semantics, vmem_limit_bytes=VMEM_LIMIT)


def _layer_norm(z, g, b):
    mu = jnp.mean(z, axis=-1, keepdims=True)
    zc = z - mu
    var = jnp.mean(zc * zc, axis=-1, keepdims=True)
    return zc * lax.rsqrt(var + LN_EPS) * g + b


def _silu(x):
    return x * jax.nn.sigmoid(x)


def _pool_ln_kernel(x_ref, w_ref, scale_ref, g_ref, b_ref, o_ref, ext_ref, *, alpha, tile):
    s = pl.program_id(1)
    d = x_ref.shape[2]
    c = d // len(POOL_WINDOWS)

    @pl.when(s == 0)
    def _():
        ext_ref[0:POOL_HALO, :] = jnp.zeros((POOL_HALO, d), F32)

    x = x_ref[0]
    ext_ref[POOL_HALO:POOL_HALO + tile, :] = x
    pos = (s * tile + 1 + lax.broadcasted_iota(jnp.int32, (tile, 1), 0)).astype(F32)
    ys = []
    for gi, w in enumerate(POOL_WINDOWS):
        cs = slice(gi * c, (gi + 1) * c)
        acc = ext_ref[:, cs]
        span = 1
        while span < w:
            acc = acc + pltpu.roll(acc, span, axis=0)
            span *= 2
        mean = acc[POOL_HALO:, :] / jnp.minimum(pos, float(w))
        mixed = mean - x[:, cs]
        ys.append(jnp.dot(mixed.astype(BF16), w_ref[gi], preferred_element_type=F32))
    y = jnp.concatenate(ys, axis=1) * scale_ref[...]
    o_ref[0] = _layer_norm(alpha * x + y, g_ref[...], b_ref[...])
    ext_ref[0:POOL_HALO, :] = x[tile - POOL_HALO:, :]


def _pool_ln(x, w, scale, g, b, alpha, tile=512):
    bsz, seq, d = x.shape
    ng, c, _ = w.shape
    vec = pl.BlockSpec((1, d), lambda i, j: (0, 0))
    return pl.pallas_call(
        functools.partial(_pool_ln_kernel, alpha=alpha, tile=tile),
        grid=(bsz, seq // tile),
        in_specs=[pl.BlockSpec((1, tile, d), lambda i, j: (i, j, 0)),
                  pl.BlockSpec((ng, c, c), lambda i, j: (0, 0, 0)),
                  vec, vec, vec],
        out_specs=pl.BlockSpec((1, tile, d), lambda i, j: (i, j, 0)),
        out_shape=jax.ShapeDtypeStruct(x.shape, F32),
        scratch_shapes=[pltpu.VMEM((POOL_HALO + tile, d), F32)],
        compiler_params=_params(("arbitrary", "arbitrary")),
        name="pool_ln",
    )(x, w.astype(BF16), scale.reshape(1, d), g.reshape(1, d), b.reshape(1, d))


def _first_argmax(v, iota, size):
    m = jnp.max(v, axis=0, keepdims=True)
    idx = jnp.min(jnp.where(v == m, iota, size), axis=0, keepdims=True)
    return m, idx


def _router_kernel(h_ref, wt_ref, b_ref, eidx_ref, gw_ref, rank_ref, cnt_ref, carry_ref, *, tile):
    i = pl.program_id(0)
    n_e = wt_ref.shape[0]
    per_group = n_e // N_EXPERT_GROUPS

    @pl.when(i == 0)
    def _():
        carry_ref[...] = jnp.zeros_like(carry_ref)

    logits = lax.dot_general(wt_ref[...], h_ref[...], (((1,), (1,)), ((), ())),
                             precision=lax.Precision.HIGHEST, preferred_element_type=F32)
    s = jax.nn.sigmoid(logits)
    sb = s + b_ref[...]

    iota_g = lax.broadcasted_iota(jnp.int32, (per_group, tile), 0)
    gscores = []
    for g in range(N_EXPERT_GROUPS):
        blk = sb[g * per_group:(g + 1) * per_group, :]
        m1, i1 = _first_argmax(blk, iota_g, per_group)
        m2 = jnp.max(jnp.where(iota_g == i1, NEG_INF, blk), axis=0, keepdims=True)
        gscores.append(m1 + m2)
    gs = jnp.concatenate(gscores, axis=0)
    iota_8 = lax.broadcasted_iota(jnp.int32, (N_EXPERT_GROUPS, tile), 0)
    gsel = jnp.zeros((N_EXPERT_GROUPS, tile), jnp.bool_)
    for _ in range(TOPK_GROUPS):
        _, gi = _first_argmax(gs, iota_8, N_EXPERT_GROUPS)
        hit = iota_8 == gi
        gsel = jnp.logical_or(gsel, hit)
        gs = jnp.where(hit, NEG_INF, gs)
    cand = jnp.concatenate(
        [jnp.where(gsel[g:g + 1, :], sb[g * per_group:(g + 1) * per_group, :], NEG_INF)
         for g in range(N_EXPERT_GROUPS)], axis=0)

    iota_e = lax.broadcasted_iota(jnp.int32, (n_e, tile), 0)
    member = jnp.zeros((n_e, tile), F32)
    idxs, svals = [], []
    for _ in range(TOP_K):
        _, ei = _first_argmax(cand, iota_e, n_e)
        hit = iota_e == ei
        idxs.append(ei)
        svals.append(jnp.sum(jnp.where(hit, s, 0.0), axis=0, keepdims=True))
        member = jnp.where(hit, 1.0, member)
        cand = jnp.where(hit, NEG_INF, cand)
    sv = jnp.concatenate(svals, axis=0)
    gw_ref[...] = sv / jnp.sum(sv, axis=0, keepdims=True) * ROUTED_SCALE
    eidx_ref[...] = jnp.concatenate(idxs, axis=0)

    r = lax.broadcasted_iota(jnp.int32, (tile, tile), 0)
    c = lax.broadcasted_iota(jnp.int32, (tile, tile), 1)
    earlier = (r < c).astype(BF16)
    before = jnp.dot(member.astype(BF16), earlier, preferred_element_type=F32) + carry_ref[...]
    ranks = [jnp.sum(jnp.where(iota_e == ei, before, 0.0), axis=0, keepdims=True) for ei in idxs]
    rank_ref[...] = jnp.concatenate(ranks, axis=0).astype(jnp.int32)
    carry_ref[...] = carry_ref[...] + jnp.sum(member, axis=1, keepdims=True)
    cnt_ref[...] = carry_ref[...].astype(jnp.int32)


def _router(h, w_router, b_router, tile=512):
    n, d = h.shape
    n_e = w_router.shape[1]
    out_kt = pl.BlockSpec((TOP_K, tile), lambda i: (0, i))
    return pl.pallas_call(
        functools.partial(_router_kernel, tile=tile),
        grid=(n // tile,),
        in_specs=[pl.BlockSpec((tile, d), lambda i: (i, 0)),
                  pl.BlockSpec((n_e, d), lambda i: (0, 0)),
                  pl.BlockSpec((n_e, 1), lambda i: (0, 0))],
        out_specs=[out_kt, out_kt, out_kt, pl.BlockSpec((n_e, 1), lambda i: (0, 0))],
        out_shape=[jax.ShapeDtypeStruct((TOP_K, n), jnp.int32),
                   jax.ShapeDtypeStruct((TOP_K, n), F32),
                   jax.ShapeDtypeStruct((TOP_K, n), jnp.int32),
                   jax.ShapeDtypeStruct((n_e, 1), jnp.int32)],
        scratch_shapes=[pltpu.VMEM((n_e, 1), F32)],
        compiler_params=_params(("arbitrary",)),
        name="router",
    )(h, w_router.T, b_router.reshape(n_e, 1))


def _dest_kernel(eidx_ref, rank_ref, pstart_ref, dest_ref):
    n_e = pstart_ref.shape[0]
    tile = eidx_ref.shape[1]
    iota_e = lax.broadcasted_iota(jnp.int32, (n_e, tile), 0)
    starts = pstart_ref[...]
    rows = [jnp.sum(jnp.where(iota_e == eidx_ref[k:k + 1, :], starts, 0), axis=0, keepdims=True)
            for k in range(TOP_K)]
    dest_ref[0] = jnp.concatenate(rows, axis=0) + rank_ref[...]


def _dest_rows(eidx, rank, pstart):
    n = eidx.shape[1]
    n_e = pstart.shape[0]
    kt = pl.BlockSpec((TOP_K, DEST_TILE), lambda i: (0, i))
    return pl.pallas_call(
        _dest_kernel,
        grid=(n // DEST_TILE,),
        in_specs=[kt, kt, pl.BlockSpec((n_e, 1), lambda i: (0, 0))],
        out_specs=pl.BlockSpec((1, TOP_K, DEST_TILE), lambda i: (i, 0, 0)),
        out_shape=jax.ShapeDtypeStruct((n // DEST_TILE, TOP_K, DEST_TILE), jnp.int32),
        compiler_params=_params(("arbitrary",)),
        name="dest_rows",
    )(eidx, rank, pstart.reshape(n_e, 1))


def _dispatch_kernel(dest_ref, h_ref, xg_hbm, sem):
    tiles = dest_ref.shape[0]
    for sub in range(tiles):
        def group(g, carry, sub=sub):
            off = pl.multiple_of(g * DMA_UNROLL, DMA_UNROLL)
            for u in range(DMA_UNROLL):
                for k in range(TOP_K):
                    pltpu.make_async_copy(h_ref.at[pl.ds(sub * DEST_TILE + off + u, 1)],
                                          xg_hbm.at[pl.ds(dest_ref[sub, k, off + u], 1)],
                                          sem).start()
            return carry
        lax.fori_loop(0, DEST_TILE // DMA_UNROLL, group, 0)
    for _ in range(tiles * TOP_K):
        pltpu.make_async_copy(xg_hbm.at[pl.ds(0, DEST_TILE)], xg_hbm.at[pl.ds(0, DEST_TILE)],
                              sem).wait()


def _dispatch(dest, h, n_rows, tiles=2):
    n, d = h.shape
    rows = tiles * DEST_TILE
    return pl.pallas_call(
        _dispatch_kernel,
        grid=(n // rows,),
        in_specs=[pl.BlockSpec((tiles, TOP_K, DEST_TILE), lambda i: (i, 0, 0),
                               memory_space=pltpu.SMEM),
                  pl.BlockSpec((rows, d), lambda i: (i, 0))],
        out_specs=pl.BlockSpec(memory_space=pl.ANY),
        out_shape=jax.ShapeDtypeStruct((n_rows, d), F32),
        scratch_shapes=[pltpu.SemaphoreType.DMA(())],
        compiler_params=_params(("arbitrary",)),
        name="dispatch_rows",
    )(dest, h)


def _grouped_kernel(be_ref, nu_ref, nv_ref, x_ref, wgu_ref, wd_ref, y_ref, wgu_bf, wd_bf):
    b = pl.program_id(0)
    f = wd_ref.shape[2]

    @pl.when(b < nu_ref[0])
    def _():
        prev_e = be_ref[jnp.maximum(b - 1, 0)]

        @pl.when(jnp.logical_or(b == 0, be_ref[b] != prev_e))
        def _():
            wgu_bf[...] = wgu_ref[0, 0].astype(BF16)
            wd_bf[...] = wd_ref[0, 0].astype(BF16)

        row = lax.broadcasted_iota(jnp.int32, (MOE_BLOCK, 1), 0)
        x = jnp.where(row < nv_ref[b], x_ref[...], 0.0).astype(BF16)
        gu = jnp.dot(x, wgu_bf[...], preferred_element_type=F32)
        act = _silu(gu[:, :f]) * gu[:, f:]
        y_ref[...] = jnp.dot(act.astype(BF16), wd_bf[...], preferred_element_type=F32)

    @pl.when(b >= nu_ref[0])
    def _():
        y_ref[...] = jnp.zeros_like(y_ref)


def _grouped_experts(xg, block_e, n_used, n_valid, w_gate_up, w_down, layer):
    p, d = xg.shape
    f2 = w_gate_up.shape[3]
    f = f2 // 2

    def used(b, nu):
        return jnp.minimum(b, nu[0] - 1)

    grid_spec = pltpu.PrefetchScalarGridSpec(
        num_scalar_prefetch=3,
        grid=(p // MOE_BLOCK,),
        in_specs=[
            pl.BlockSpec((MOE_BLOCK, d), lambda b, be, nu, nv: (used(b, nu), 0)),
            pl.BlockSpec((1, 1, d, f2), lambda b, be, nu, nv: (layer, be[used(b, nu)], 0, 0)),
            pl.BlockSpec((1, 1, f, d), lambda b, be, nu, nv: (layer, be[used(b, nu)], 0, 0)),
        ],
        out_specs=pl.BlockSpec((MOE_BLOCK, d), lambda b, be, nu, nv: (b, 0)),
        scratch_shapes=[pltpu.VMEM((d, f2), BF16), pltpu.VMEM((f, d), BF16)],
    )
    return pl.pallas_call(
        _grouped_kernel,
        grid_spec=grid_spec,
        out_shape=jax.ShapeDtypeStruct((p, d), F32),
        compiler_params=_params(("arbitrary",)),
        name="grouped_experts",
    )(block_e, n_used, n_valid, xg, w_gate_up, w_down)


def _combine_ln_kernel(dfirst_ref, dnext_ref, h_ref, gw_ref, y_hbm, wsgu_ref, wsd_ref, g_ref, b_ref,
                       o_ref, ybuf, sem, *, alpha):
    i = pl.program_id(0)
    nt = pl.num_programs(0)
    slot = lax.rem(i, 2)
    f = wsd_ref.shape[0]

    def start_gather(dref, s):
        def group(g, carry):
            off = pl.multiple_of(g * DMA_UNROLL, DMA_UNROLL)
            for u in range(DMA_UNROLL):
                for k in range(TOP_K):
                    pltpu.make_async_copy(y_hbm.at[pl.ds(dref[0, k, off + u], 1)],
                                          ybuf.at[s, k, pl.ds(off + u, 1)], sem.at[s]).start()
            return carry
        lax.fori_loop(0, DEST_TILE // DMA_UNROLL, group, 0)

    @pl.when(i == 0)
    def _():
        start_gather(dfirst_ref, 0)

    @pl.when(i + 1 < nt)
    def _():
        start_gather(dnext_ref, 1 - slot)

    h = h_ref[...]
    gu = jnp.dot(h.astype(BF16), wsgu_ref[...], preferred_element_type=F32)
    act = _silu(gu[:, :f]) * gu[:, f:]
    shared = jnp.dot(act.astype(BF16), wsd_ref[...], preferred_element_type=F32)

    for k in range(TOP_K):
        pltpu.make_async_copy(y_hbm.at[pl.ds(0, DEST_TILE)], ybuf.at[slot, k], sem.at[slot]).wait()
    gw = gw_ref[...]
    routed = gw[:, 0:1] * ybuf[slot, 0]
    for k in range(1, TOP_K):
        routed = routed + gw[:, k:k + 1] * ybuf[slot, k]
    o_ref[...] = _layer_norm(alpha * h + (routed + shared), g_ref[...], b_ref[...])


def _combine_ln(dest, h, gw, y_rows, ws_gate_up, ws_down, g, b, alpha):
    n, d = h.shape
    f2 = ws_gate_up.shape[1]
    nt = n // DEST_TILE
    vec = pl.BlockSpec((1, d), lambda i: (0, 0))
    dblk = (1, TOP_K, DEST_TILE)
    return pl.pallas_call(
        functools.partial(_combine_ln_kernel, alpha=alpha),
        grid=(nt,),
        in_specs=[pl.BlockSpec(dblk, lambda i: (0, 0, 0), memory_space=pltpu.SMEM),
                  pl.BlockSpec(dblk, lambda i: (jnp.minimum(i + 1, nt - 1), 0, 0),
                               memory_space=pltpu.SMEM),
                  pl.BlockSpec((DEST_TILE, d), lambda i: (i, 0)),
                  pl.BlockSpec((DEST_TILE, TOP_K), lambda i: (i, 0)),
                  pl.BlockSpec(memory_space=pl.ANY),
                  pl.BlockSpec((d, f2), lambda i: (0, 0)),
                  pl.BlockSpec((f2 // 2, d), lambda i: (0, 0)), vec, vec],
        out_specs=pl.BlockSpec((DEST_TILE, d), lambda i: (i, 0)),
        out_shape=jax.ShapeDtypeStruct((n, d), F32),
        scratch_shapes=[pltpu.VMEM((2, TOP_K, DEST_TILE, d), F32),
                        pltpu.SemaphoreType.DMA((2,))],
        compiler_params=_params(("arbitrary",)),
        name="combine_ln",
    )(dest, dest, h, gw, y_rows, ws_gate_up.astype(BF16), ws_down.astype(BF16),
      g.reshape(1, d), b.reshape(1, d))


def _moe_ln(h, w_router, b_router, w_gate_up, w_down, layer, ws_gate_up, ws_down, g, b, alpha):
    n, d = h.shape
    n_e = w_router.shape[1]
    eidx, gw, rank, counts = _router(h, w_router, b_router)

    nb = -(-(n * TOP_K + n_e * (MOE_BLOCK - 1)) // MOE_BLOCK)
    counts = counts[:, 0]
    padded = (counts + MOE_BLOCK - 1) // MOE_BLOCK * MOE_BLOCK
    pend = jnp.cumsum(padded)
    pstart = pend - padded
    block_start = jnp.arange(nb, dtype=jnp.int32) * MOE_BLOCK
    block_e = jnp.minimum(jnp.sum(pend[None, :] <= block_start[:, None], axis=1),
                          n_e - 1).astype(jnp.int32)
    n_valid = jnp.clip(pstart[block_e] + counts[block_e] - block_start, 0, MOE_BLOCK)
    n_used = (pend[-1:] // MOE_BLOCK).astype(jnp.int32)

    dest = _dest_rows(eidx, rank, pstart.astype(jnp.int32))
    xg = _dispatch(dest, h, nb * MOE_BLOCK)
    y_rows = _grouped_experts(xg, block_e, n_used, n_valid.astype(jnp.int32), w_gate_up, w_down,
                              layer)
    return _combine_ln(dest, h, gw.T, y_rows, ws_gate_up, ws_down, g, b, alpha)


F_PIECES = 3
ONES_ROWS = 16


def _gate_placement(n_heads):
    import numpy as np
    n_pairs = n_heads // 2
    pk = np.zeros((F_PIECES, LANES, n_pairs * LANES), np.float32)
    ones_k = np.zeros((1, n_pairs * LANES), np.float32)
    pq = np.zeros((F_PIECES, LANES, n_heads * LANES), np.float32)
    ones_q = np.zeros((1, n_heads * LANES), np.float32)
    for h in range(n_heads):
        p, e = divmod(h, 2)
        for j in range(F_PIECES):
            pk[j, h, p * LANES + F_PIECES * e + j] = -1.0
            pq[j, h, h * LANES + 2 * F_PIECES + j] = 1.0
            ones_q[0, h * LANES + F_PIECES * e + j] = 1.0
            ones_k[0, p * LANES + 2 * F_PIECES + j] = 1.0
    return (jnp.asarray(pk, BF16), jnp.asarray(ones_k), jnp.asarray(pq, BF16), jnp.asarray(ones_q))


def _kvf_kernel(h_ref, wk_ref, wvt_ref, wf_ref, bf_ref, pk_ref, onesk_ref, k_ref, vt_ref, fp_ref,
                carry_ref, *, tile):
    s = pl.program_id(1)
    n_pairs = vt_ref.shape[1]

    @pl.when(s == 0)
    def _():
        carry_ref[...] = jnp.zeros_like(carry_ref)

    h = h_ref[0]
    hb = h.astype(BF16)
    z = jnp.dot(h, wf_ref[...], precision=lax.Precision.HIGHEST,
                preferred_element_type=F32) + bf_ref[...]
    logf = jnp.minimum(z, 0.0) - jnp.log1p(jnp.exp(-jnp.abs(z)))
    row = lax.broadcasted_iota(jnp.int32, (tile, 1), 0)
    span = 1
    while span < tile:
        logf = logf + jnp.where(row >= span, pltpu.roll(logf, span, axis=0), 0.0)
        span *= 2
    cum = logf + carry_ref[...]
    carry_ref[...] = cum[tile - 1:tile, :]

    pieces = []
    rest = cum
    for _ in range(F_PIECES):
        piece = rest.astype(BF16)
        pieces.append(piece)
        rest = rest - piece.astype(F32)
    fp_ref[0] = jnp.concatenate(pieces, axis=1)
    kf = onesk_ref[...]
    for j in range(F_PIECES):
        kf = kf + jnp.dot(pieces[j], pk_ref[j], preferred_element_type=F32)

    k = jnp.dot(hb, wk_ref[...], preferred_element_type=F32)
    parts = []
    for p in range(n_pairs):
        parts += [k[:, p * LANES:(p + 1) * LANES], kf[:, p * LANES:(p + 1) * LANES]]
    k_ref[0] = jnp.concatenate(parts, axis=1).astype(BF16)

    vt = lax.dot_general(wvt_ref[...], hb, (((1,), (1,)), ((), ())),
                         preferred_element_type=F32).astype(BF16)
    for p in range(n_pairs):
        vt_ref[0, p, 0:LANES, :] = vt[p * LANES:(p + 1) * LANES, :]
        vt_ref[0, p, LANES:LANES + ONES_ROWS, :] = jnp.ones((ONES_ROWS, tile), BF16)


def _kvf(h3, w_kvf, b_f, pk, ones_k, tile=512):
    bsz, seq, d = h3.shape
    n_h = b_f.shape[0]
    n_pairs = n_h // 2
    wk = w_kvf[:, :d].astype(BF16)
    wvt = w_kvf[:, d:2 * d].T.astype(BF16)
    wf = jnp.pad(w_kvf[:, 2 * d:], ((0, 0), (0, LANES - n_h)))
    bf = jnp.pad(b_f, (0, LANES - n_h)).reshape(1, LANES)
    const2 = lambda i, j: (0, 0)
    return pl.pallas_call(
        functools.partial(_kvf_kernel, tile=tile),
        grid=(bsz, seq // tile),
        in_specs=[pl.BlockSpec((1, tile, d), lambda i, j: (i, j, 0)),
                  pl.BlockSpec((d, d), const2), pl.BlockSpec((d, d), const2),
                  pl.BlockSpec((d, LANES), const2), pl.BlockSpec((1, LANES), const2),
                  pl.BlockSpec(pk.shape, lambda i, j: (0, 0, 0)),
                  pl.BlockSpec(ones_k.shape, const2)],
        out_specs=[pl.BlockSpec((1, tile, 2 * d), lambda i, j: (i, j, 0)),
                   pl.BlockSpec((1, n_pairs, LANES + ONES_ROWS, tile), lambda i, j: (i, 0, 0, j)),
                   pl.BlockSpec((1, tile, F_PIECES * LANES), lambda i, j: (i, j, 0))],
        out_shape=[jax.ShapeDtypeStruct((bsz, seq, 2 * d), BF16),
                   jax.ShapeDtypeStruct((bsz, n_pairs, LANES + ONES_ROWS, seq), BF16),
                   jax.ShapeDtypeStruct((bsz, seq, F_PIECES * LANES), BF16)],
        scratch_shapes=[pltpu.VMEM((1, LANES), F32)],
        compiler_params=_params(("arbitrary", "arbitrary")),
        name="kvf_proj",
    )(h3, wk, wvt, wf, bf, pk, ones_k)


def _qproj_kernel(h_ref, wq_ref, fp_ref, pq_ref, onesq_ref, q_ref, *, scale):
    n_h = q_ref.shape[1] // (2 * LANES)
    q = jnp.dot(h_ref[...].astype(BF16), wq_ref[...], preferred_element_type=F32) * scale
    qf = onesq_ref[...]
    for j in range(F_PIECES):
        qf = qf + jnp.dot(fp_ref[:, j * LANES:(j + 1) * LANES], pq_ref[j],
                          preferred_element_type=F32)
    first = lax.broadcasted_iota(jnp.int32, (1, LANES), 1) < HEAD_DIM
    parts = []
    for hd in range(n_h):
        p, e = divmod(hd, 2)
        qp = q[:, p * LANES:(p + 1) * LANES]
        parts += [jnp.where(first if e == 0 else jnp.logical_not(first), qp, 0.0),
                  qf[:, hd * LANES:(hd + 1) * LANES]]
    q_ref[...] = jnp.concatenate(parts, axis=1).astype(BF16)


def _qproj(h, w_q, fpieces, pq, ones_q, scale, tile=512):
    n, d = h.shape
    n_h = d // HEAD_DIM
    const2 = lambda i: (0, 0)
    return pl.pallas_call(
        functools.partial(_qproj_kernel, scale=scale),
        grid=(n // tile,),
        in_specs=[pl.BlockSpec((tile, d), lambda i: (i, 0)),
                  pl.BlockSpec((d, d), const2),
                  pl.BlockSpec((tile, F_PIECES * LANES), lambda i: (i, 0)),
                  pl.BlockSpec(pq.shape, lambda i: (0, 0, 0)),
                  pl.BlockSpec(ones_q.shape, const2)],
        out_specs=pl.BlockSpec((tile, n_h * 2 * LANES), lambda i: (i, 0)),
        out_shape=jax.ShapeDtypeStruct((n, n_h * 2 * LANES), BF16),
        compiler_params=_params(("arbitrary",)),
        name="q_proj",
    )(h, w_q.astype(BF16), fpieces, pq, ones_q)


def _attn_kernel(q_ref, k_ref, vt_ref, o_ref, *, tq, tk):
    i = pl.program_id(2)
    per_q = tq // tk
    qa = q_ref[0]
    qs = (qa[:, :2 * LANES], qa[:, 2 * LANES:])
    row = lax.broadcasted_iota(jnp.int32, (tk, tq), 0)
    col = lax.broadcasted_iota(jnp.int32, (tk, tq), 1)
    upper = lax.broadcasted_iota(jnp.int32, (LANES, 1), 0) < HEAD_DIM

    def step(j, carry, diag_offset):
        ms, ls, acc = carry
        start = pl.multiple_of(j * tk, tk)
        kj = k_ref[0, pl.ds(start, tk), :]
        vtj = vt_ref[0, 0, :, pl.ds(start, tk)]
        new_ms, new_ls, alphas, outs = [], [], [], []
        for hd in range(2):
            sc = lax.dot_general(kj, qs[hd], (((1,), (1,)), ((), ())), preferred_element_type=F32)
            if diag_offset is not None:
                sc = jnp.where(row + diag_offset <= col, sc, NEG_INF)
            m_new = jnp.maximum(ms[hd], jnp.max(sc, axis=0, keepdims=True))
            pt = jnp.exp(sc - m_new).astype(BF16)
            alpha = jnp.exp(ms[hd] - m_new)
            r = jnp.dot(vtj, pt, preferred_element_type=F32)
            new_ms.append(m_new)
            new_ls.append(alpha * ls[hd] + r[LANES:LANES + 1, :])
            alphas.append(alpha)
            outs.append(r[:LANES, :])
        acc = jnp.where(upper, alphas[0], alphas[1]) * acc + jnp.where(upper, outs[0], outs[1])
        return tuple(new_ms), tuple(new_ls), acc

    def full_blocks(jq, carry):
        for u in range(per_q):
            carry = step(jq * per_q + u, carry, None)
        return carry

    init = ((jnp.full((1, tq), NEG_INF, F32),) * 2, (jnp.zeros((1, tq), F32),) * 2,
            jnp.zeros((LANES, tq), F32))
    carry = lax.fori_loop(0, i, full_blocks, init)
    for u in range(per_q):
        carry = step(i * per_q + u, carry, u * tk)
    _, ls, acc = carry
    out_t = acc / jnp.where(upper, ls[0], ls[1])
    o_ref[0] = out_t.T.astype(BF16)


def _attention(qaug, kaug, vaug_t, d, tq=512, tk=512):
    bsz, seq, _ = kaug.shape
    n_pairs = vaug_t.shape[1]
    return pl.pallas_call(
        functools.partial(_attn_kernel, tq=tq, tk=tk),
        grid=(bsz, n_pairs, seq // tq),
        in_specs=[pl.BlockSpec((1, tq, 4 * LANES), lambda b, p, i: (b, i, p)),
                  pl.BlockSpec((1, seq, 2 * LANES), lambda b, p, i: (b, 0, p)),
                  pl.BlockSpec((1, 1, LANES + ONES_ROWS, seq), lambda b, p, i: (b, p, 0, 0))],
        out_specs=pl.BlockSpec((1, tq, LANES), lambda b, p, i: (b, i, p)),
        out_shape=jax.ShapeDtypeStruct((bsz, seq, d), BF16),
        compiler_params=_params(("arbitrary", "arbitrary", "arbitrary")),
        name="fox_attention",
    )(qaug, kaug, vaug_t)


def _oproj_ln_kernel(o_ref, h_ref, wo_ref, g_ref, b_ref, out_ref, *, alpha):
    mix = jnp.dot(o_ref[...], wo_ref[...], preferred_element_type=F32)
    out_ref[...] = _layer_norm(alpha * h_ref[...] + mix, g_ref[...], b_ref[...])


def _oproj_ln(o, h, w_o, g, b, alpha, tile=512):
    n, d = h.shape
    vec = pl.BlockSpec((1, d), lambda i: (0, 0))
    tspec = pl.BlockSpec((tile, d), lambda i: (i, 0))
    return pl.pallas_call(
        functools.partial(_oproj_ln_kernel, alpha=alpha),
        grid=(n // tile,),
        in_specs=[tspec, tspec, pl.BlockSpec((d, d), lambda i: (0, 0)), vec, vec],
        out_specs=tspec,
        out_shape=jax.ShapeDtypeStruct((n, d), F32),
        compiler_params=_params(("arbitrary",)),
        name="oproj_ln",
    )(o, h, w_o.astype(BF16), g.reshape(1, d), b.reshape(1, d))


def kernel(x, pool_w, pool_scale, w_q, w_o, w_kvf, b_f, ln_g, ln_b, router_w, router_b,
           w_gate_up, w_down, ws_gate_up, ws_down):
    bsz, seq, d = x.shape
    depth = ln_g.shape[0]
    n_pool = pool_w.shape[0]
    alpha = float((2 * depth) ** 0.25)
    n = bsz * seq
    kaug = vaug_t = fpieces = None
    pk, ones_k, pq, ones_q = _gate_placement(d // HEAD_DIM)
    for l in range(depth):
        if l < n_pool:
            x = _pool_ln(x, pool_w[l], pool_scale[l], ln_g[l, 0], ln_b[l, 0], alpha)
        else:
            if l == n_pool:
                kaug, vaug_t, fpieces = _kvf(x, w_kvf, b_f, pk, ones_k)
            j = l - n_pool
            qaug = _qproj(x.reshape(n, d), w_q[j], fpieces.reshape(n, -1), pq, ones_q,
                          HEAD_DIM ** -0.5)
            o = _attention(qaug.reshape(bsz, seq, -1), kaug, vaug_t, d)
            x = _oproj_ln(o.reshape(n, d), x.reshape(n, d), w_o[j], ln_g[l, 0], ln_b[l, 0],
                          alpha).reshape(bsz, seq, d)
        x = _moe_ln(x.reshape(n, d), router_w[l], router_b[l], w_gate_up, w_down, l,
                    ws_gate_up[l], ws_down[l], ln_g[l, 1], ln_b[l, 1], alpha).reshape(bsz, seq, d)
    return x
```

```python
import functools

import jax
import jax.numpy as jnp
from jax import lax
from jax.experimental import pallas as pl
from jax.experimental.pallas import tpu as pltpu

POOL_WINDOWS = (2, 4, 8, 16)
POOL_HALO = 16
HEAD_DIM = 64
N_EXPERT_GROUPS = 8
TOPK_GROUPS = 4
TOP_K = 8
ROUTED_SCALE = 2.5
MOE_BLOCK = 256
DMA_UNROLL = 8
DMA_QUEUES = 2
DEST_TILE = 256
LN_EPS = 1e-5

LANES = 128
SUBLANES = 8
VMEM_LIMIT = 48 * 1024 * 1024

F32 = jnp.float32
BF16 = jnp.bfloat16
NEG_INF = float("-inf")


def _params(semantics):
    return pltpu.CompilerParams(dimension_semantics=semantics, vmem_limit_bytes=VMEM_LIMIT)


def _layer_norm(z, g, b):
    mu = jnp.mean(z, axis=-1, keepdims=True)
    zc = z - mu
    var = jnp.mean(zc * zc, axis=-1, keepdims=True)
    return zc * lax.rsqrt(var + LN_EPS) * g + b


def _silu(x):
    return x * jax.nn.sigmoid(x)


def _pool_ln_kernel(x_ref, w_ref, scale_ref, g_ref, b_ref, o_ref, ext_ref, *, alpha, tile):
    s = pl.program_id(1)
    d = x_ref.shape[2]
    c = d // len(POOL_WINDOWS)

    @pl.when(s == 0)
    def _():
        ext_ref[0:POOL_HALO, :] = jnp.zeros((POOL_HALO, d), F32)

    x = x_ref[0]
    ext_ref[POOL_HALO:POOL_HALO + tile, :] = x
    pos = (s * tile + 1 + lax.broadcasted_iota(jnp.int32, (tile, 1), 0)).astype(F32)
    ys = []
    for gi, w in enumerate(POOL_WINDOWS):
        cs = slice(gi * c, (gi + 1) * c)
        acc = ext_ref[:, cs]
        span = 1
        while span < w:
            acc = acc + pltpu.roll(acc, span, axis=0)
            span *= 2
        mean = acc[POOL_HALO:, :] / jnp.minimum(pos, float(w))
        mixed = mean - x[:, cs]
        ys.append(jnp.dot(mixed.astype(BF16), w_ref[gi], preferred_element_type=F32))
    y = jnp.concatenate(ys, axis=1) * scale_ref[...]
    o_ref[0] = _layer_norm(alpha * x + y, g_ref[...], b_ref[...])
    ext_ref[0:POOL_HALO, :] = x[tile - POOL_HALO:, :]


def _pool_ln(x, w, scale, g, b, alpha, tile=512):
    bsz, seq, d = x.shape
    ng, c, _ = w.shape
    vec = pl.BlockSpec((1, d), lambda i, j: (0, 0))
    return pl.pallas_call(
        functools.partial(_pool_ln_kernel, alpha=alpha, tile=tile),
        grid=(bsz, seq // tile),
        in_specs=[pl.BlockSpec((1, tile, d), lambda i, j: (i, j, 0)),
                  pl.BlockSpec((ng, c, c), lambda i, j: (0, 0, 0)),
                  vec, vec, vec],
        out_specs=pl.BlockSpec((1, tile, d), lambda i, j: (i, j, 0)),
        out_shape=jax.ShapeDtypeStruct(x.shape, F32),
        scratch_shapes=[pltpu.VMEM((POOL_HALO + tile, d), F32)],
        compiler_params=_params(("arbitrary", "arbitrary")),
        name="pool_ln",
    )(x, w.astype(BF16), scale.reshape(1, d), g.reshape(1, d), b.reshape(1, d))


def _first_argmax(v, iota, size):
    m = jnp.max(v, axis=0, keepdims=True)
    idx = jnp.min(jnp.where(v == m, iota, size), axis=0, keepdims=True)
    return m, idx


def _router_kernel(h_ref, wt_ref, b_ref, eidx_ref, gw_ref, rank_ref, cnt_ref, carry_ref, *, tile):
    i = pl.program_id(0)
    n_e = wt_ref.shape[0]
    per_group = n_e // N_EXPERT_GROUPS

    @pl.when(i == 0)
    def _():
        carry_ref[...] = jnp.zeros_like(carry_ref)

    logits = lax.dot_general(wt_ref[...], h_ref[...], (((1,), (1,)), ((), ())),
                             precision=lax.Precision.HIGHEST, preferred_element_type=F32)
    s = jax.nn.sigmoid(logits)
    sb = s + b_ref[...]

    iota_g = lax.broadcasted_iota(jnp.int32, (per_group, tile), 0)
    gscores = []
    for g in range(N_EXPERT_GROUPS):
        blk = sb[g * per_group:(g + 1) * per_group, :]
        m1, i1 = _first_argmax(blk, iota_g, per_group)
        m2 = jnp.max(jnp.where(iota_g == i1, NEG_INF, blk), axis=0, keepdims=True)
        gscores.append(m1 + m2)
    gs = jnp.concatenate(gscores, axis=0)
    iota_8 = lax.broadcasted_iota(jnp.int32, (N_EXPERT_GROUPS, tile), 0)
    gsel = jnp.zeros((N_EXPERT_GROUPS, tile), jnp.bool_)
    for _ in range(TOPK_GROUPS):
        _, gi = _first_argmax(gs, iota_8, N_EXPERT_GROUPS)
        hit = iota_8 == gi
        gsel = jnp.logical_or(gsel, hit)
        gs = jnp.where(hit, NEG_INF, gs)
    cand = jnp.concatenate(
        [jnp.where(gsel[g:g + 1, :], sb[g * per_group:(g + 1) * per_group, :], NEG_INF)
         for g in range(N_EXPERT_GROUPS)], axis=0)

    iota_e = lax.broadcasted_iota(jnp.int32, (n_e, tile), 0)
    member = jnp.zeros((n_e, tile), F32)
    idxs, svals = [], []
    for _ in range(TOP_K):
        _, ei = _first_argmax(cand, iota_e, n_e)
        hit = iota_e == ei
        idxs.append(ei)
        svals.append(jnp.sum(jnp.where(hit, s, 0.0), axis=0, keepdims=True))
        member = jnp.where(hit, 1.0, member)
        cand = jnp.where(hit, NEG_INF, cand)
    sv = jnp.concatenate(svals, axis=0)
    gw_ref[...] = sv / jnp.sum(sv, axis=0, keepdims=True) * ROUTED_SCALE
    eidx_ref[...] = jnp.concatenate(idxs, axis=0)

    r = lax.broadcasted_iota(jnp.int32, (tile, tile), 0)
    c = lax.broadcasted_iota(jnp.int32, (tile, tile), 1)
    earlier = (r < c).astype(BF16)
    before = jnp.dot(member.astype(BF16), earlier, preferred_element_type=F32) + carry_ref[...]
    ranks = [jnp.sum(jnp.where(iota_e == ei, before, 0.0), axis=0, keepdims=True) for ei in idxs]
    rank_ref[...] = jnp.concatenate(ranks, axis=0).astype(jnp.int32)
    carry_ref[...] = carry_ref[...] + jnp.sum(member, axis=1, keepdims=True)
    cnt_ref[...] = carry_ref[...].astype(jnp.int32)


def _router(h, w_router, b_router, tile=512):
    n, d = h.shape
    n_e = w_router.shape[1]
    out_kt = pl.BlockSpec((TOP_K, tile), lambda i: (0, i))
    return pl.pallas_call(
        functools.partial(_router_kernel, tile=tile),
        grid=(n // tile,),
        in_specs=[pl.BlockSpec((tile, d), lambda i: (i, 0)),
                  pl.BlockSpec((n_e, d), lambda i: (0, 0)),
                  pl.BlockSpec((n_e, 1), lambda i: (0, 0))],
        out_specs=[out_kt, out_kt, out_kt, pl.BlockSpec((n_e, 1), lambda i: (0, 0))],
        out_shape=[jax.ShapeDtypeStruct((TOP_K, n), jnp.int32),
                   jax.ShapeDtypeStruct((TOP_K, n), F32),
                   jax.ShapeDtypeStruct((TOP_K, n), jnp.int32),
                   jax.ShapeDtypeStruct((n_e, 1), jnp.int32)],
        scratch_shapes=[pltpu.VMEM((n_e, 1), F32)],
        compiler_params=_params(("arbitrary",)),
        name="router",
    )(h, w_router.T, b_router.reshape(n_e, 1))


def _dest_kernel(eidx_ref, rank_ref, pstart_ref, dest_ref):
    n_e = pstart_ref.shape[0]
    tile = eidx_ref.shape[1]
    iota_e = lax.broadcasted_iota(jnp.int32, (n_e, tile), 0)
    starts = pstart_ref[...]
    rows = [jnp.sum(jnp.where(iota_e == eidx_ref[k:k + 1, :], starts, 0), axis=0, keepdims=True)
            for k in range(TOP_K)]
    dest_ref[0] = jnp.concatenate(rows, axis=0) + rank_ref[...]


def _dest_rows(eidx, rank, pstart):
    n = eidx.shape[1]
    n_e = pstart.shape[0]
    kt = pl.BlockSpec((TOP_K, DEST_TILE), lambda i: (0, i))
    return pl.pallas_call(
        _dest_kernel,
        grid=(n // DEST_TILE,),
        in_specs=[kt, kt, pl.BlockSpec((n_e, 1), lambda i: (0, 0))],
        out_specs=pl.BlockSpec((1, TOP_K, DEST_TILE), lambda i: (i, 0, 0)),
        out_shape=jax.ShapeDtypeStruct((n // DEST_TILE, TOP_K, DEST_TILE), jnp.int32),
        compiler_params=_params(("arbitrary",)),
        name="dest_rows",
    )(eidx, rank, pstart.reshape(n_e, 1))


def _dispatch_kernel(dest_ref, h_ref, xg_hbm, sem):
    tiles = dest_ref.shape[0]
    for sub in range(tiles):
        def group(g, carry, sub=sub):
            off = pl.multiple_of(g * DMA_UNROLL, DMA_UNROLL)
            for u in range(DMA_UNROLL):
                for k in range(TOP_K):
                    pltpu.make_async_copy(h_ref.at[pl.ds(sub * DEST_TILE + off + u, 1)],
                                          xg_hbm.at[pl.ds(dest_ref[sub, k, off + u], 1)],
                                          sem).start(priority=k % DMA_QUEUES)
            return carry
        lax.fori_loop(0, DEST_TILE // DMA_UNROLL, group, 0)
    for _ in range(tiles * TOP_K):
        pltpu.make_async_copy(xg_hbm.at[pl.ds(0, DEST_TILE)], xg_hbm.at[pl.ds(0, DEST_TILE)],
                              sem).wait()


def _dispatch(dest, h, n_rows, tiles=2):
    n, d = h.shape
    rows = tiles * DEST_TILE
    return pl.pallas_call(
        _dispatch_kernel,
        grid=(n // rows,),
        in_specs=[pl.BlockSpec((tiles, TOP_K, DEST_TILE), lambda i: (i, 0, 0),
                               memory_space=pltpu.SMEM),
                  pl.BlockSpec((rows, d), lambda i: (i, 0))],
        out_specs=pl.BlockSpec(memory_space=pl.ANY),
        out_shape=jax.ShapeDtypeStruct((n_rows, d), F32),
        scratch_shapes=[pltpu.SemaphoreType.DMA(())],
        compiler_params=_params(("arbitrary",)),
        name="dispatch_rows",
    )(dest, h)


def _grouped_kernel(be_ref, nu_ref, nv_ref, x_ref, wgu_ref, wd_ref, y_ref, wgu_bf, wd_bf):
    b = pl.program_id(0)
    f = wd_ref.shape[2]

    @pl.when(b < nu_ref[0])
    def _():
        prev_e = be_ref[jnp.maximum(b - 1, 0)]

        @pl.when(jnp.logical_or(b == 0, be_ref[b] != prev_e))
        def _():
            wgu_bf[...] = wgu_ref[0, 0].astype(BF16)
            wd_bf[...] = wd_ref[0, 0].astype(BF16)

        row = lax.broadcasted_iota(jnp.int32, (MOE_BLOCK, 1), 0)
        x = jnp.where(row < nv_ref[b], x_ref[...], 0.0).astype(BF16)
        gu = jnp.dot(x, wgu_bf[...], preferred_element_type=F32)
        act = _silu(gu[:, :f]) * gu[:, f:]
        y_ref[...] = jnp.dot(act.astype(BF16), wd_bf[...], preferred_element_type=F32)

    @pl.when(b >= nu_ref[0])
    def _():
        y_ref[...] = jnp.zeros_like(y_ref)


def _grouped_experts(xg, block_e, n_used, n_valid, w_gate_up, w_down, layer):
    p, d = xg.shape
    f2 = w_gate_up.shape[3]
    f = f2 // 2

    def used(b, nu):
        return jnp.minimum(b, nu[0] - 1)

    grid_spec = pltpu.PrefetchScalarGridSpec(
        num_scalar_prefetch=3,
        grid=(p // MOE_BLOCK,),
        in_specs=[
            pl.BlockSpec((MOE_BLOCK, d), lambda b, be, nu, nv: (used(b, nu), 0)),
            pl.BlockSpec((1, 1, d, f2), lambda b, be, nu, nv: (layer, be[used(b, nu)], 0, 0)),
            pl.BlockSpec((1, 1, f, d), lambda b, be, nu, nv: (layer, be[used(b, nu)], 0, 0)),
        ],
        out_specs=pl.BlockSpec((MOE_BLOCK, d), lambda b, be, nu, nv: (b, 0)),
        scratch_shapes=[pltpu.VMEM((d, f2), BF16), pltpu.VMEM((f, d), BF16)],
    )
    return pl.pallas_call(
        _grouped_kernel,
        grid_spec=grid_spec,
        out_shape=jax.ShapeDtypeStruct((p, d), F32),
        compiler_params=_params(("arbitrary",)),
        name="grouped_experts",
    )(block_e, n_used, n_valid, xg, w_gate_up, w_down)


def _combine_ln_kernel(dpair_ref, dnext_ref, h_ref, gw_ref, y_hbm, wsgu_ref, wsd_ref, g_ref, b_ref,
                       o_ref, ybuf, sem, *, alpha):
    i = pl.program_id(0)
    n_steps = pl.num_programs(0)
    f = wsd_ref.shape[0]

    def start_gather(dref, sub, s):
        def group(g, carry):
            off = pl.multiple_of(g * DMA_UNROLL, DMA_UNROLL)
            for u in range(DMA_UNROLL):
                for k in range(TOP_K):
                    pltpu.make_async_copy(y_hbm.at[pl.ds(dref[sub, k, off + u], 1)],
                                          ybuf.at[s, k, pl.ds(off + u, 1)],
                                          sem.at[s]).start(priority=k % DMA_QUEUES)
            return carry
        lax.fori_loop(0, DEST_TILE // DMA_UNROLL, group, 0)

    def finish(s):
        rows = slice(s * DEST_TILE, (s + 1) * DEST_TILE)
        h = h_ref[rows, :]
        gu = jnp.dot(h.astype(BF16), wsgu_ref[...], preferred_element_type=F32)
        act = _silu(gu[:, :f]) * gu[:, f:]
        shared = jnp.dot(act.astype(BF16), wsd_ref[...], preferred_element_type=F32)
        for k in range(TOP_K):
            pltpu.make_async_copy(y_hbm.at[pl.ds(0, DEST_TILE)], ybuf.at[s, k], sem.at[s]).wait()
        gw = gw_ref[rows, :]
        routed = gw[:, 0:1] * ybuf[s, 0]
        for k in range(1, TOP_K):
            routed = routed + gw[:, k:k + 1] * ybuf[s, k]
        o_ref[rows, :] = _layer_norm(alpha * h + (routed + shared), g_ref[...], b_ref[...])

    @pl.when(i == 0)
    def _():
        start_gather(dpair_ref, 0, 0)

    start_gather(dpair_ref, 1, 1)
    finish(0)

    @pl.when(i + 1 < n_steps)
    def _():
        start_gather(dnext_ref, 0, 0)

    finish(1)


def _combine_ln(dest, h, gw, y_rows, ws_gate_up, ws_down, g, b, alpha):
    n, d = h.shape
    f2 = ws_gate_up.shape[1]
    nt = n // DEST_TILE
    rows = 2 * DEST_TILE
    vec = pl.BlockSpec((1, d), lambda i: (0, 0))
    return pl.pallas_call(
        functools.partial(_combine_ln_kernel, alpha=alpha),
        grid=(nt // 2,),
        in_specs=[pl.BlockSpec((2, TOP_K, DEST_TILE), lambda i: (i, 0, 0),
                               memory_space=pltpu.SMEM),
                  pl.BlockSpec((1, TOP_K, DEST_TILE),
                               lambda i: (jnp.minimum(2 * i + 2, nt - 1), 0, 0),
                               memory_space=pltpu.SMEM),
                  pl.BlockSpec((rows, d), lambda i: (i, 0)),
                  pl.BlockSpec((rows, TOP_K), lambda i: (i, 0)),
                  pl.BlockSpec(memory_space=pl.ANY),
                  pl.BlockSpec((d, f2), lambda i: (0, 0)),
                  pl.BlockSpec((f2 // 2, d), lambda i: (0, 0)), vec, vec],
        out_specs=pl.BlockSpec((rows, d), lambda i: (i, 0)),
        out_shape=jax.ShapeDtypeStruct((n, d), F32),
        scratch_shapes=[pltpu.VMEM((2, TOP_K, DEST_TILE, d), F32),
                        pltpu.SemaphoreType.DMA((2,))],
        compiler_params=_params(("arbitrary",)),
        name="combine_ln",
    )(dest, dest, h, gw, y_rows, ws_gate_up.astype(BF16), ws_down.astype(BF16),
      g.reshape(1, d), b.reshape(1, d))


def _moe_ln(h, w_router, b_router, w_gate_up, w_down, layer, ws_gate_up, ws_down, g, b, alpha):
    n, d = h.shape
    n_e = w_router.shape[1]
    eidx, gw, rank, counts = _router(h, w_router, b_router)

    nb = -(-(n * TOP_K + n_e * (MOE_BLOCK - 1)) // MOE_BLOCK)
    counts = counts[:, 0]
    padded = (counts + MOE_BLOCK - 1) // MOE_BLOCK * MOE_BLOCK
    pend = jnp.cumsum(padded)
    pstart = pend - padded
    block_start = jnp.arange(nb, dtype=jnp.int32) * MOE_BLOCK
    block_e = jnp.minimum(jnp.sum(pend[None, :] <= block_start[:, None], axis=1),
                          n_e - 1).astype(jnp.int32)
    n_valid = jnp.clip(pstart[block_e] + counts[block_e] - block_start, 0, MOE_BLOCK)
    n_used = (pend[-1:] // MOE_BLOCK).astype(jnp.int32)

    dest = _dest_rows(eidx, rank, pstart.astype(jnp.int32))
    xg = _dispatch(dest, h, nb * MOE_BLOCK)
    y_rows = _grouped_experts(xg, block_e, n_used, n_valid.astype(jnp.int32), w_gate_up, w_down,
                              layer)
    return _combine_ln(dest, h, gw.T, y_rows, ws_gate_up, ws_down, g, b, alpha)


F_PIECES = 3
ONES_ROWS = 16


def _gate_placement(n_heads):
    import numpy as np
    n_pairs = n_heads // 2
    pk = np.zeros((F_PIECES, LANES, n_pairs * LANES), np.float32)
    ones_k = np.zeros((1, n_pairs * LANES), np.float32)
    pq = np.zeros((F_PIECES, LANES, n_heads * LANES), np.float32)
    ones_q = np.zeros((1, n_heads * LANES), np.float32)
    for h in range(n_heads):
        p, e = divmod(h, 2)
        for j in range(F_PIECES):
            pk[j, h, p * LANES + F_PIECES * e + j] = -1.0
            pq[j, h, h * LANES + 2 * F_PIECES + j] = 1.0
            ones_q[0, h * LANES + F_PIECES * e + j] = 1.0
            ones_k[0, p * LANES + 2 * F_PIECES + j] = 1.0
    return (jnp.asarray(pk, BF16), jnp.asarray(ones_k), jnp.asarray(pq, BF16), jnp.asarray(ones_q))


def _kvf_kernel(h_ref, wk_ref, wvt_ref, wf_ref, bf_ref, pk_ref, onesk_ref, k_ref, vt_ref, fp_ref,
                carry_ref, *, tile):
    s = pl.program_id(1)
    n_pairs = vt_ref.shape[1]

    @pl.when(s == 0)
    def _():
        carry_ref[...] = jnp.zeros_like(carry_ref)

    h = h_ref[0]
    hb = h.astype(BF16)
    z = jnp.dot(h, wf_ref[...], precision=lax.Precision.HIGHEST,
                preferred_element_type=F32) + bf_ref[...]
    logf = jnp.minimum(z, 0.0) - jnp.log1p(jnp.exp(-jnp.abs(z)))
    row = lax.broadcasted_iota(jnp.int32, (tile, 1), 0)
    span = 1
    while span < tile:
        logf = logf + jnp.where(row >= span, pltpu.roll(logf, span, axis=0), 0.0)
        span *= 2
    cum = logf + carry_ref[...]
    carry_ref[...] = cum[tile - 1:tile, :]

    pieces = []
    rest = cum
    for _ in range(F_PIECES):
        piece = rest.astype(BF16)
        pieces.append(piece)
        rest = rest - piece.astype(F32)
    fp_ref[0] = jnp.concatenate(pieces, axis=1)
    kf = onesk_ref[...]
    for j in range(F_PIECES):
        kf = kf + jnp.dot(pieces[j], pk_ref[j], preferred_element_type=F32)

    k = jnp.dot(hb, wk_ref[...], preferred_element_type=F32)
    parts = []
    for p in range(n_pairs):
        parts += [k[:, p * LANES:(p + 1) * LANES], kf[:, p * LANES:(p + 1) * LANES]]
    k_ref[0] = jnp.concatenate(parts, axis=1).astype(BF16)

    vt = lax.dot_general(wvt_ref[...], hb, (((1,), (1,)), ((), ())),
                         preferred_element_type=F32).astype(BF16)
    for p in range(n_pairs):
        vt_ref[0, p, 0:LANES, :] = vt[p * LANES:(p + 1) * LANES, :]
        vt_ref[0, p, LANES:LANES + ONES_ROWS, :] = jnp.ones((ONES_ROWS, tile), BF16)


def _kvf(h3, w_kvf, b_f, pk, ones_k, tile=512):
    bsz, seq, d = h3.shape
    n_h = b_f.shape[0]
    n_pairs = n_h // 2
    wk = w_kvf[:, :d].astype(BF16)
    wvt = w_kvf[:, d:2 * d].T.astype(BF16)
    wf = jnp.pad(w_kvf[:, 2 * d:], ((0, 0), (0, LANES - n_h)))
    bf = jnp.pad(b_f, (0, LANES - n_h)).reshape(1, LANES)
    const2 = lambda i, j: (0, 0)
    return pl.pallas_call(
        functools.partial(_kvf_kernel, tile=tile),
        grid=(bsz, seq // tile),
        in_specs=[pl.BlockSpec((1, tile, d), lambda i, j: (i, j, 0)),
                  pl.BlockSpec((d, d), const2), pl.BlockSpec((d, d), const2),
                  pl.BlockSpec((d, LANES), const2), pl.BlockSpec((1, LANES), const2),
                  pl.BlockSpec(pk.shape, lambda i, j: (0, 0, 0)),
                  pl.BlockSpec(ones_k.shape, const2)],
        out_specs=[pl.BlockSpec((1, tile, 2 * d), lambda i, j: (i, j, 0)),
                   pl.BlockSpec((1, n_pairs, LANES + ONES_ROWS, tile), lambda i, j: (i, 0, 0, j)),
                   pl.BlockSpec((1, tile, F_PIECES * LANES), lambda i, j: (i, j, 0))],
        out_shape=[jax.ShapeDtypeStruct((bsz, seq, 2 * d), BF16),
                   jax.ShapeDtypeStruct((bsz, n_pairs, LANES + ONES_ROWS, seq), BF16),
                   jax.ShapeDtypeStruct((bsz, seq, F_PIECES * LANES), BF16)],
        scratch_shapes=[pltpu.VMEM((1, LANES), F32)],
        compiler_params=_params(("arbitrary", "arbitrary")),
        name="kvf_proj",
    )(h3, wk, wvt, wf, bf, pk, ones_k)


def _qproj_kernel(h_ref, wq_ref, fp_ref, pq_ref, onesq_ref, q_ref, *, scale):
    n_h = q_ref.shape[1] // (2 * LANES)
    q = jnp.dot(h_ref[...].astype(BF16), wq_ref[...], preferred_element_type=F32) * scale
    qf = onesq_ref[...]
    for j in range(F_PIECES):
        qf = qf + jnp.dot(fp_ref[:, j * LANES:(j + 1) * LANES], pq_ref[j],
                          preferred_element_type=F32)
    first = lax.broadcasted_iota(jnp.int32, (1, LANES), 1) < HEAD_DIM
    parts = []
    for hd in range(n_h):
        p, e = divmod(hd, 2)
        qp = q[:, p * LANES:(p + 1) * LANES]
        parts += [jnp.where(first if e == 0 else jnp.logical_not(first), qp, 0.0),
                  qf[:, hd * LANES:(hd + 1) * LANES]]
    q_ref[...] = jnp.concatenate(parts, axis=1).astype(BF16)


def _qproj(h, w_q, fpieces, pq, ones_q, scale, tile=512):
    n, d = h.shape
    n_h = d // HEAD_DIM
    const2 = lambda i: (0, 0)
    return pl.pallas_call(
        functools.partial(_qproj_kernel, scale=scale),
        grid=(n // tile,),
        in_specs=[pl.BlockSpec((tile, d), lambda i: (i, 0)),
                  pl.BlockSpec((d, d), const2),
                  pl.BlockSpec((tile, F_PIECES * LANES), lambda i: (i, 0)),
                  pl.BlockSpec(pq.shape, lambda i: (0, 0, 0)),
                  pl.BlockSpec(ones_q.shape, const2)],
        out_specs=pl.BlockSpec((tile, n_h * 2 * LANES), lambda i: (i, 0)),
        out_shape=jax.ShapeDtypeStruct((n, n_h * 2 * LANES), BF16),
        compiler_params=_params(("arbitrary",)),
        name="q_proj",
    )(h, w_q.astype(BF16), fpieces, pq, ones_q)


def _attn_kernel(q_ref, k_ref, vt_ref, o_ref, *, tq, tk):
    i = pl.program_id(2)
    per_q = tq // tk
    qa = q_ref[0]
    qs = (qa[:, :2 * LANES], qa[:, 2 * LANES:])
    row = lax.broadcasted_iota(jnp.int32, (tk, tq), 0)
    col = lax.broadcasted_iota(jnp.int32, (tk, tq), 1)
    upper = lax.broadcasted_iota(jnp.int32, (LANES, 1), 0) < HEAD_DIM

    def step(j, carry, diag_offset):
        ms, ls, acc = carry
        start = pl.multiple_of(j * tk, tk)
        kj = k_ref[0, pl.ds(start, tk), :]
        vtj = vt_ref[0, 0, :, pl.ds(start, tk)]
        new_ms, new_ls, alphas, outs = [], [], [], []
        for hd in range(2):
            sc = lax.dot_general(kj, qs[hd], (((1,), (1,)), ((), ())), preferred_element_type=F32)
            if diag_offset is not None:
                sc = jnp.where(row + diag_offset <= col, sc, NEG_INF)
            m_new = jnp.maximum(ms[hd], jnp.max(sc, axis=0, keepdims=True))
            pt = jnp.exp(sc - m_new).astype(BF16)
            alpha = jnp.exp(ms[hd] - m_new)
            r = jnp.dot(vtj, pt, preferred_element_type=F32)
            new_ms.append(m_new)
            new_ls.append(alpha * ls[hd] + r[LANES:LANES + 1, :])
            alphas.append(alpha)
            outs.append(r[:LANES, :])
        acc = jnp.where(upper, alphas[0], alphas[1]) * acc + jnp.where(upper, outs[0], outs[1])
        return tuple(new_ms), tuple(new_ls), acc

    def full_blocks(jq, carry):
        for u in range(per_q):
            carry = step(jq * per_q + u, carry, None)
        return carry

    init = ((jnp.full((1, tq), NEG_INF, F32),) * 2, (jnp.zeros((1, tq), F32),) * 2,
            jnp.zeros((LANES, tq), F32))
    carry = lax.fori_loop(0, i, full_blocks, init)
    for u in range(per_q):
        carry = step(i * per_q + u, carry, u * tk)
    _, ls, acc = carry
    out_t = acc / jnp.where(upper, ls[0], ls[1])
    o_ref[0] = out_t.T.astype(BF16)


def _attention(qaug, kaug, vaug_t, d, tq=512, tk=512):
    bsz, seq, _ = kaug.shape
    n_pairs = vaug_t.shape[1]
    return pl.pallas_call(
        functools.partial(_attn_kernel, tq=tq, tk=tk),
        grid=(bsz, n_pairs, seq // tq),
        in_specs=[pl.BlockSpec((1, tq, 4 * LANES), lambda b, p, i: (b, i, p)),
                  pl.BlockSpec((1, seq, 2 * LANES), lambda b, p, i: (b, 0, p)),
                  pl.BlockSpec((1, 1, LANES + ONES_ROWS, seq), lambda b, p, i: (b, p, 0, 0))],
        out_specs=pl.BlockSpec((1, tq, LANES), lambda b, p, i: (b, i, p)),
        out_shape=jax.ShapeDtypeStruct((bsz, seq, d), BF16),
        compiler_params=_params(("arbitrary", "arbitrary", "arbitrary")),
        name="fox_attention",
    )(qaug, kaug, vaug_t)


def _oproj_ln_kernel(o_ref, h_ref, wo_ref, g_ref, b_ref, out_ref, *, alpha):
    mix = jnp.dot(o_ref[...], wo_ref[...], preferred_element_type=F32)
    out_ref[...] = _layer_norm(alpha * h_ref[...] + mix, g_ref[...], b_ref[...])


def _oproj_ln(o, h, w_o, g, b, alpha, tile=512):
    n, d = h.shape
    vec = pl.BlockSpec((1, d), lambda i: (0, 0))
    tspec = pl.BlockSpec((tile, d), lambda i: (i, 0))
    return pl.pallas_call(
        functools.partial(_oproj_ln_kernel, alpha=alpha),
        grid=(n // tile,),
        in_specs=[tspec, tspec, pl.BlockSpec((d, d), lambda i: (0, 0)), vec, vec],
        out_specs=tspec,
        out_shape=jax.ShapeDtypeStruct((n, d), F32),
        compiler_params=_params(("arbitrary",)),
        name="oproj_ln",
    )(o, h, w_o.astype(BF16), g.reshape(1, d), b.reshape(1, d))


def kernel(x, pool_w, pool_scale, w_q, w_o, w_kvf, b_f, ln_g, ln_b, router_w, router_b,
           w_gate_up, w_down, ws_gate_up, ws_down):
    bsz, seq, d = x.shape
    depth = ln_g.shape[0]
    n_pool = pool_w.shape[0]
    alpha = float((2 * depth) ** 0.25)
    n = bsz * seq
    kaug = vaug_t = fpieces = None
    pk, ones_k, pq, ones_q = _gate_placement(d // HEAD_DIM)
    for l in range(depth):
        if l < n_pool:
            x = _pool_ln(x, pool_w[l], pool_scale[l], ln_g[l, 0], ln_b[l, 0], alpha)
        else:
            if l == n_pool:
                kaug, vaug_t, fpieces = _kvf(x, w_kvf, b_f, pk, ones_k)
            j = l - n_pool
            qaug = _qproj(x.reshape(n, d), w_q[j], fpieces.reshape(n, -1), pq, ones_q,
                          HEAD_DIM ** -0.5)
            o = _attention(qaug.reshape(bsz, seq, -1), kaug, vaug_t, d)
            x = _oproj_ln(o.reshape(n, d), x.reshape(n, d), w_o[j], ln_g[l, 0], ln_b[l, 0],
                          alpha).reshape(bsz, seq, d)
        x = _moe_ln(x.reshape(n, d), router_w[l], router_b[l], w_gate_up, w_down, l,
                    ws_gate_up[l], ws_down[l], ln_g[l, 1], ln_b[l, 1], alpha).reshape(bsz, seq, d)
    return x
```

```python
import functools

import jax
import jax.numpy as jnp
from jax import lax
from jax.experimental import pallas as pl
from jax.experimental.pallas import tpu as pltpu

POOL_WINDOWS = (2, 4, 8, 16)
POOL_HALO = 16
HEAD_DIM = 64
N_EXPERT_GROUPS = 8
TOPK_GROUPS = 4
TOP_K = 8
ROUTED_SCALE = 2.5
MOE_BLOCK = 256
DMA_UNROLL = 8
DMA_QUEUES = 2
DEST_TILE = 256
LN_EPS = 1e-5

LANES = 128
SUBLANES = 8
VMEM_LIMIT = 48 * 1024 * 1024

F32 = jnp.float32
BF16 = jnp.bfloat16
NEG_INF = float("-inf")


def _params(semantics):
    return pltpu.CompilerParams(dimension_semantics=semantics, vmem_limit_bytes=VMEM_LIMIT)


def _layer_norm(z, g, b):
    mu = jnp.mean(z, axis=-1, keepdims=True)
    zc = z - mu
    var = jnp.mean(zc * zc, axis=-1, keepdims=True)
    return zc * lax.rsqrt(var + LN_EPS) * g + b


def _silu(x):
    return x * jax.nn.sigmoid(x)


def _pool_ln_kernel(x_ref, w_ref, scale_ref, g_ref, b_ref, o_ref, ext_ref, *, alpha, tile):
    s = pl.program_id(1)
    d = x_ref.shape[2]
    c = d // len(POOL_WINDOWS)

    @pl.when(s == 0)
    def _():
        ext_ref[0:POOL_HALO, :] = jnp.zeros((POOL_HALO, d), F32)

    x = x_ref[0]
    ext_ref[POOL_HALO:POOL_HALO + tile, :] = x
    pos = (s * tile + 1 + lax.broadcasted_iota(jnp.int32, (tile, 1), 0)).astype(F32)
    ys = []
    for gi, w in enumerate(POOL_WINDOWS):
        cs = slice(gi * c, (gi + 1) * c)
        acc = ext_ref[:, cs]
        span = 1
        while span < w:
            acc = acc + pltpu.roll(acc, span, axis=0)
            span *= 2
        mean = acc[POOL_HALO:, :] / jnp.minimum(pos, float(w))
        mixed = mean - x[:, cs]
        ys.append(jnp.dot(mixed.astype(BF16), w_ref[gi], preferred_element_type=F32))
    y = jnp.concatenate(ys, axis=1) * scale_ref[...]
    o_ref[0] = _layer_norm(alpha * x + y, g_ref[...], b_ref[...])
    ext_ref[0:POOL_HALO, :] = x[tile - POOL_HALO:, :]


def _pool_ln(x, w, scale, g, b, alpha, tile=512):
    bsz, seq, d = x.shape
    ng, c, _ = w.shape
    vec = pl.BlockSpec((1, d), lambda i, j: (0, 0))
    return pl.pallas_call(
        functools.partial(_pool_ln_kernel, alpha=alpha, tile=tile),
        grid=(bsz, seq // tile),
        in_specs=[pl.BlockSpec((1, tile, d), lambda i, j: (i, j, 0)),
                  pl.BlockSpec((ng, c, c), lambda i, j: (0, 0, 0)),
                  vec, vec, vec],
        out_specs=pl.BlockSpec((1, tile, d), lambda i, j: (i, j, 0)),
        out_shape=jax.ShapeDtypeStruct(x.shape, F32),
        scratch_shapes=[pltpu.VMEM((POOL_HALO + tile, d), F32)],
        compiler_params=_params(("arbitrary", "arbitrary")),
        name="pool_ln",
    )(x, w.astype(BF16), scale.reshape(1, d), g.reshape(1, d), b.reshape(1, d))


def _first_argmax(v, iota, size):
    m = jnp.max(v, axis=0, keepdims=True)
    idx = jnp.min(jnp.where(v == m, iota, size), axis=0, keepdims=True)
    return m, idx


def _router_kernel(h_ref, wt_ref, b_ref, eidx_ref, gw_ref, rank_ref, cnt_ref, carry_ref, *, tile):
    i = pl.program_id(0)
    n_e = wt_ref.shape[0]
    per_group = n_e // N_EXPERT_GROUPS

    @pl.when(i == 0)
    def _():
        carry_ref[...] = jnp.zeros_like(carry_ref)

    logits = lax.dot_general(wt_ref[...], h_ref[...], (((1,), (1,)), ((), ())),
                             precision=lax.Precision.HIGHEST, preferred_element_type=F32)
    s = jax.nn.sigmoid(logits)
    sb = s + b_ref[...]

    iota_g = lax.broadcasted_iota(jnp.int32, (per_group, tile), 0)
    gscores = []
    for g in range(N_EXPERT_GROUPS):
        blk = sb[g * per_group:(g + 1) * per_group, :]
        m1, i1 = _first_argmax(blk, iota_g, per_group)
        m2 = jnp.max(jnp.where(iota_g == i1, NEG_INF, blk), axis=0, keepdims=True)
        gscores.append(m1 + m2)
    gs = jnp.concatenate(gscores, axis=0)
    iota_8 = lax.broadcasted_iota(jnp.int32, (N_EXPERT_GROUPS, tile), 0)
    gsel = jnp.zeros((N_EXPERT_GROUPS, tile), jnp.bool_)
    for _ in range(TOPK_GROUPS):
        _, gi = _first_argmax(gs, iota_8, N_EXPERT_GROUPS)
        hit = iota_8 == gi
        gsel = jnp.logical_or(gsel, hit)
        gs = jnp.where(hit, NEG_INF, gs)
    cand = jnp.concatenate(
        [jnp.where(gsel[g:g + 1, :], sb[g * per_group:(g + 1) * per_group, :], NEG_INF)
         for g in range(N_EXPERT_GROUPS)], axis=0)

    iota_e = lax.broadcasted_iota(jnp.int32, (n_e, tile), 0)
    member = jnp.zeros((n_e, tile), F32)
    idxs, svals = [], []
    for _ in range(TOP_K):
        _, ei = _first_argmax(cand, iota_e, n_e)
        hit = iota_e == ei
        idxs.append(ei)
        svals.append(jnp.sum(jnp.where(hit, s, 0.0), axis=0, keepdims=True))
        member = jnp.where(hit, 1.0, member)
        cand = jnp.where(hit, NEG_INF, cand)
    sv = jnp.concatenate(svals, axis=0)
    gw_ref[...] = sv / jnp.sum(sv, axis=0, keepdims=True) * ROUTED_SCALE
    eidx_ref[...] = jnp.concatenate(idxs, axis=0)

    r = lax.broadcasted_iota(jnp.int32, (tile, tile), 0)
    c = lax.broadcasted_iota(jnp.int32, (tile, tile), 1)
    earlier = (r < c).astype(BF16)
    before = jnp.dot(member.astype(BF16), earlier, preferred_element_type=F32) + carry_ref[...]
    ranks = [jnp.sum(jnp.where(iota_e == ei, before, 0.0), axis=0, keepdims=True) for ei in idxs]
    rank_ref[...] = jnp.concatenate(ranks, axis=0).astype(jnp.int32)
    carry_ref[...] = carry_ref[...] + jnp.sum(member, axis=1, keepdims=True)
    cnt_ref[...] = carry_ref[...].astype(jnp.int32)


def _router(h, w_router, b_router, tile=512):
    n, d = h.shape
    n_e = w_router.shape[1]
    out_kt = pl.BlockSpec((TOP_K, tile), lambda i: (0, i))
    return pl.pallas_call(
        functools.partial(_router_kernel, tile=tile),
        grid=(n // tile,),
        in_specs=[pl.BlockSpec((tile, d), lambda i: (i, 0)),
                  pl.BlockSpec((n_e, d), lambda i: (0, 0)),
                  pl.BlockSpec((n_e, 1), lambda i: (0, 0))],
        out_specs=[out_kt, out_kt, out_kt, pl.BlockSpec((n_e, 1), lambda i: (0, 0))],
        out_shape=[jax.ShapeDtypeStruct((TOP_K, n), jnp.int32),
                   jax.ShapeDtypeStruct((TOP_K, n), F32),
                   jax.ShapeDtypeStruct((TOP_K, n), jnp.int32),
                   jax.ShapeDtypeStruct((n_e, 1), jnp.int32)],
        scratch_shapes=[pltpu.VMEM((n_e, 1), F32)],
        compiler_params=_params(("arbitrary",)),
        name="router",
    )(h, w_router.T, b_router.reshape(n_e, 1))


def _dest_kernel(eidx_ref, rank_ref, pstart_ref, dest_ref):
    n_e = pstart_ref.shape[0]
    tile = eidx_ref.shape[1]
    iota_e = lax.broadcasted_iota(jnp.int32, (n_e, tile), 0)
    starts = pstart_ref[...]
    rows = [jnp.sum(jnp.where(iota_e == eidx_ref[k:k + 1, :], starts, 0), axis=0, keepdims=True)
            for k in range(TOP_K)]
    dest_ref[0] = jnp.concatenate(rows, axis=0) + rank_ref[...]


def _dest_rows(eidx, rank, pstart):
    n = eidx.shape[1]
    n_e = pstart.shape[0]
    kt = pl.BlockSpec((TOP_K, DEST_TILE), lambda i: (0, i))
    return pl.pallas_call(
        _dest_kernel,
        grid=(n // DEST_TILE,),
        in_specs=[kt, kt, pl.BlockSpec((n_e, 1), lambda i: (0, 0))],
        out_specs=pl.BlockSpec((1, TOP_K, DEST_TILE), lambda i: (i, 0, 0)),
        out_shape=jax.ShapeDtypeStruct((n // DEST_TILE, TOP_K, DEST_TILE), jnp.int32),
        compiler_params=_params(("arbitrary",)),
        name="dest_rows",
    )(eidx, rank, pstart.reshape(n_e, 1))


def _dispatch_kernel(dest_ref, h_ref, xg_hbm, sem):
    tiles = dest_ref.shape[0]
    for sub in range(tiles):
        def group(g, carry, sub=sub):
            off = pl.multiple_of(g * DMA_UNROLL, DMA_UNROLL)
            for u in range(DMA_UNROLL):
                for k in range(TOP_K):
                    pltpu.make_async_copy(h_ref.at[pl.ds(sub * DEST_TILE + off + u, 1)],
                                          xg_hbm.at[pl.ds(dest_ref[sub, k, off + u], 1)],
                                          sem).start(priority=k % DMA_QUEUES)
            return carry
        lax.fori_loop(0, DEST_TILE // DMA_UNROLL, group, 0)
    for _ in range(tiles * TOP_K):
        pltpu.make_async_copy(xg_hbm.at[pl.ds(0, DEST_TILE)], xg_hbm.at[pl.ds(0, DEST_TILE)],
                              sem).wait()


def _dispatch(dest, h, n_rows, tiles=2):
    n, d = h.shape
    rows = tiles * DEST_TILE
    return pl.pallas_call(
        _dispatch_kernel,
        grid=(n // rows,),
        in_specs=[pl.BlockSpec((tiles, TOP_K, DEST_TILE), lambda i: (i, 0, 0),
                               memory_space=pltpu.SMEM),
                  pl.BlockSpec((rows, d), lambda i: (i, 0))],
        out_specs=pl.BlockSpec(memory_space=pl.ANY),
        out_shape=jax.ShapeDtypeStruct((n_rows, d), F32),
        scratch_shapes=[pltpu.SemaphoreType.DMA(())],
        compiler_params=_params(("arbitrary",)),
        name="dispatch_rows",
    )(dest, h)


def _grouped_kernel(be_ref, nu_ref, nv_ref, x_ref, wgu_ref, wd_ref, y_ref, wgu_bf, wd_bf):
    b = pl.program_id(0)
    f = wd_ref.shape[2]

    @pl.when(b < nu_ref[0])
    def _():
        prev_e = be_ref[jnp.maximum(b - 1, 0)]

        @pl.when(jnp.logical_or(b == 0, be_ref[b] != prev_e))
        def _():
            wgu_bf[...] = wgu_ref[0, 0].astype(BF16)
            wd_bf[...] = wd_ref[0, 0].astype(BF16)

        row = lax.broadcasted_iota(jnp.int32, (MOE_BLOCK, 1), 0)
        x = jnp.where(row < nv_ref[b], x_ref[...], 0.0).astype(BF16)
        gu = jnp.dot(x, wgu_bf[...], preferred_element_type=F32)
        act = _silu(gu[:, :f]) * gu[:, f:]
        y_ref[...] = jnp.dot(act.astype(BF16), wd_bf[...], preferred_element_type=F32)

    @pl.when(b >= nu_ref[0])
    def _():
        y_ref[...] = jnp.zeros_like(y_ref)


def _grouped_experts(xg, block_e, n_used, n_valid, w_gate_up, w_down, layer):
    p, d = xg.shape
    f2 = w_gate_up.shape[3]
    f = f2 // 2

    def used(b, nu):
        return jnp.minimum(b, nu[0] - 1)

    grid_spec = pltpu.PrefetchScalarGridSpec(
        num_scalar_prefetch=3,
        grid=(p // MOE_BLOCK,),
        in_specs=[
            pl.BlockSpec((MOE_BLOCK, d), lambda b, be, nu, nv: (used(b, nu), 0)),
            pl.BlockSpec((1, 1, d, f2), lambda b, be, nu, nv: (layer, be[used(b, nu)], 0, 0)),
            pl.BlockSpec((1, 1, f, d), lambda b, be, nu, nv: (layer, be[used(b, nu)], 0, 0)),
        ],
        out_specs=pl.BlockSpec((MOE_BLOCK, d), lambda b, be, nu, nv: (b, 0)),
        scratch_shapes=[pltpu.VMEM((d, f2), BF16), pltpu.VMEM((f, d), BF16)],
    )
    return pl.pallas_call(
        _grouped_kernel,
        grid_spec=grid_spec,
        out_shape=jax.ShapeDtypeStruct((p, d), F32),
        compiler_params=_params(("arbitrary",)),
        name="grouped_experts",
    )(block_e, n_used, n_valid, xg, w_gate_up, w_down)


def _combine_ln_kernel(dpair_ref, dnext_ref, h_ref, gw_ref, y_hbm, wsgu_ref, wsd_ref, g_ref, b_ref,
                       o_ref, ybuf, sem, *, alpha):
    i = pl.program_id(0)
    n_steps = pl.num_programs(0)
    f = wsd_ref.shape[0]

    def start_gather(dref, sub, s):
        def group(g, carry):
            off = pl.multiple_of(g * DMA_UNROLL, DMA_UNROLL)
            for u in range(DMA_UNROLL):
                for k in range(TOP_K):
                    pltpu.make_async_copy(y_hbm.at[pl.ds(dref[sub, k, off + u], 1)],
                                          ybuf.at[s, k, pl.ds(off + u, 1)],
                                          sem.at[s]).start(priority=k % DMA_QUEUES)
            return carry
        lax.fori_loop(0, DEST_TILE // DMA_UNROLL, group, 0)

    def finish(s):
        rows = slice(s * DEST_TILE, (s + 1) * DEST_TILE)
        h = h_ref[rows, :]
        gu = jnp.dot(h.astype(BF16), wsgu_ref[...], preferred_element_type=F32)
        act = _silu(gu[:, :f]) * gu[:, f:]
        shared = jnp.dot(act.astype(BF16), wsd_ref[...], preferred_element_type=F32)
        for k in range(TOP_K):
            pltpu.make_async_copy(y_hbm.at[pl.ds(0, DEST_TILE)], ybuf.at[s, k], sem.at[s]).wait()
        gw = gw_ref[rows, :]
        routed = gw[:, 0:1] * ybuf[s, 0]
        for k in range(1, TOP_K):
            routed = routed + gw[:, k:k + 1] * ybuf[s, k]
        o_ref[rows, :] = _layer_norm(alpha * h + (routed + shared), g_ref[...], b_ref[...])

    @pl.when(i == 0)
    def _():
        start_gather(dpair_ref, 0, 0)

    start_gather(dpair_ref, 1, 1)
    finish(0)

    @pl.when(i + 1 < n_steps)
    def _():
        start_gather(dnext_ref, 0, 0)

    finish(1)


def _combine_ln(dest, h, gw, y_rows, ws_gate_up, ws_down, g, b, alpha):
    n, d = h.shape
    f2 = ws_gate_up.shape[1]
    nt = n // DEST_TILE
    rows = 2 * DEST_TILE
    vec = pl.BlockSpec((1, d), lambda i: (0, 0))
    return pl.pallas_call(
        functools.partial(_combine_ln_kernel, alpha=alpha),
        grid=(nt // 2,),
        in_specs=[pl.BlockSpec((2, TOP_K, DEST_TILE), lambda i: (i, 0, 0),
                               memory_space=pltpu.SMEM),
                  pl.BlockSpec((1, TOP_K, DEST_TILE),
                               lambda i: (jnp.minimum(2 * i + 2, nt - 1), 0, 0),
                               memory_space=pltpu.SMEM),
                  pl.BlockSpec((rows, d), lambda i: (i, 0)),
                  pl.BlockSpec((rows, TOP_K), lambda i: (i, 0)),
                  pl.BlockSpec(memory_space=pl.ANY),
                  pl.BlockSpec((d, f2), lambda i: (0, 0)),
                  pl.BlockSpec((f2 // 2, d), lambda i: (0, 0)), vec, vec],
        out_specs=pl.BlockSpec((rows, d), lambda i: (i, 0)),
        out_shape=jax.ShapeDtypeStruct((n, d), F32),
        scratch_shapes=[pltpu.VMEM((2, TOP_K, DEST_TILE, d), F32),
                        pltpu.SemaphoreType.DMA((2,))],
        compiler_params=_params(("arbitrary",)),
        name="combine_ln",
    )(dest, dest, h, gw, y_rows, ws_gate_up.astype(BF16), ws_down.astype(BF16),
      g.reshape(1, d), b.reshape(1, d))


def _moe_ln(h, w_router, b_router, w_gate_up, w_down, layer, ws_gate_up, ws_down, g, b, alpha):
    n, d = h.shape
    n_e = w_router.shape[1]
    eidx, gw, rank, counts = _router(h, w_router, b_router)

    nb = -(-(n * TOP_K + n_e * (MOE_BLOCK - 1)) // MOE_BLOCK)
    counts = counts[:, 0]
    padded = (counts + MOE_BLOCK - 1) // MOE_BLOCK * MOE_BLOCK
    pend = jnp.cumsum(padded)
    pstart = pend - padded
    block_start = jnp.arange(nb, dtype=jnp.int32) * MOE_BLOCK
    block_e = jnp.minimum(jnp.sum(pend[None, :] <= block_start[:, None], axis=1),
                          n_e - 1).astype(jnp.int32)
    n_valid = jnp.clip(pstart[block_e] + counts[block_e] - block_start, 0, MOE_BLOCK)
    n_used = (pend[-1:] // MOE_BLOCK).astype(jnp.int32)

    dest = _dest_rows(eidx, rank, pstart.astype(jnp.int32))
    xg = _dispatch(dest, h, nb * MOE_BLOCK)
    y_rows = _grouped_experts(xg, block_e, n_used, n_valid.astype(jnp.int32), w_gate_up, w_down,
                              layer)
    return _combine_ln(dest, h, gw.T, y_rows, ws_gate_up, ws_down, g, b, alpha)


F_PIECES = 3
ONES_ROWS = 16


def _gate_placement(n_heads):
    import numpy as np
    n_pairs = n_heads // 2
    pk = np.zeros((F_PIECES, LANES, n_pairs * LANES), np.float32)
    ones_k = np.zeros((1, n_pairs * LANES), np.float32)
    pq = np.zeros((F_PIECES, LANES, n_heads * LANES), np.float32)
    ones_q = np.zeros((1, n_heads * LANES), np.float32)
    for h in range(n_heads):
        p, e = divmod(h, 2)
        for j in range(F_PIECES):
            pk[j, h, p * LANES + F_PIECES * e + j] = -1.0
            pq[j, h, h * LANES + 2 * F_PIECES + j] = 1.0
            ones_q[0, h * LANES + F_PIECES * e + j] = 1.0
            ones_k[0, p * LANES + 2 * F_PIECES + j] = 1.0
    return (jnp.asarray(pk, BF16), jnp.asarray(ones_k), jnp.asarray(pq, BF16), jnp.asarray(ones_q))


def _kvf_kernel(h_ref, wk_ref, wvt_ref, wf_ref, bf_ref, pk_ref, onesk_ref, k_ref, vt_ref, fp_ref,
                carry_ref, *, tile):
    s = pl.program_id(1)
    n_pairs = vt_ref.shape[1]

    @pl.when(s == 0)
    def _():
        carry_ref[...] = jnp.zeros_like(carry_ref)

    h = h_ref[0]
    hb = h.astype(BF16)
    z = jnp.dot(h, wf_ref[...], precision=lax.Precision.HIGHEST,
                preferred_element_type=F32) + bf_ref[...]
    logf = jnp.minimum(z, 0.0) - jnp.log1p(jnp.exp(-jnp.abs(z)))
    row = lax.broadcasted_iota(jnp.int32, (tile, 1), 0)
    span = 1
    while span < tile:
        logf = logf + jnp.where(row >= span, pltpu.roll(logf, span, axis=0), 0.0)
        span *= 2
    cum = logf + carry_ref[...]
    carry_ref[...] = cum[tile - 1:tile, :]

    pieces = []
    rest = cum
    for _ in range(F_PIECES):
        piece = rest.astype(BF16)
        pieces.append(piece)
        rest = rest - piece.astype(F32)
    fp_ref[0] = jnp.concatenate(pieces, axis=1)
    kf = onesk_ref[...]
    for j in range(F_PIECES):
        kf = kf + jnp.dot(pieces[j], pk_ref[j], preferred_element_type=F32)

    k = jnp.dot(hb, wk_ref[...], preferred_element_type=F32)
    parts = []
    for p in range(n_pairs):
        parts += [k[:, p * LANES:(p + 1) * LANES], kf[:, p * LANES:(p + 1) * LANES]]
    k_ref[0] = jnp.concatenate(parts, axis=1).astype(BF16)

    vt = lax.dot_general(wvt_ref[...], hb, (((1,), (1,)), ((), ())),
                         preferred_element_type=F32).astype(BF16)
    for p in range(n_pairs):
        vt_ref[0, p, 0:LANES, :] = vt[p * LANES:(p + 1) * LANES, :]
        vt_ref[0, p, LANES:LANES + ONES_ROWS, :] = jnp.ones((ONES_ROWS, tile), BF16)


def _kvf(h3, w_kvf, b_f, pk, ones_k, tile=512):
    bsz, seq, d = h3.shape
    n_h = b_f.shape[0]
    n_pairs = n_h // 2
    wk = w_kvf[:, :d].astype(BF16)
    wvt = w_kvf[:, d:2 * d].T.astype(BF16)
    wf = jnp.pad(w_kvf[:, 2 * d:], ((0, 0), (0, LANES - n_h)))
    bf = jnp.pad(b_f, (0, LANES - n_h)).reshape(1, LANES)
    const2 = lambda i, j: (0, 0)
    return pl.pallas_call(
        functools.partial(_kvf_kernel, tile=tile),
        grid=(bsz, seq // tile),
        in_specs=[pl.BlockSpec((1, tile, d), lambda i, j: (i, j, 0)),
                  pl.BlockSpec((d, d), const2), pl.BlockSpec((d, d), const2),
                  pl.BlockSpec((d, LANES), const2), pl.BlockSpec((1, LANES), const2),
                  pl.BlockSpec(pk.shape, lambda i, j: (0, 0, 0)),
                  pl.BlockSpec(ones_k.shape, const2)],
        out_specs=[pl.BlockSpec((1, tile, 2 * d), lambda i, j: (i, j, 0)),
                   pl.BlockSpec((1, n_pairs, LANES + ONES_ROWS, tile), lambda i, j: (i, 0, 0, j)),
                   pl.BlockSpec((1, tile, F_PIECES * LANES), lambda i, j: (i, j, 0))],
        out_shape=[jax.ShapeDtypeStruct((bsz, seq, 2 * d), BF16),
                   jax.ShapeDtypeStruct((bsz, n_pairs, LANES + ONES_ROWS, seq), BF16),
                   jax.ShapeDtypeStruct((bsz, seq, F_PIECES * LANES), BF16)],
        scratch_shapes=[pltpu.VMEM((1, LANES), F32)],
        compiler_params=_params(("arbitrary", "arbitrary")),
        name="kvf_proj",
    )(h3, wk, wvt, wf, bf, pk, ones_k)


def _qproj_kernel(h_ref, wq_ref, fp_ref, pq_ref, onesq_ref, q_ref, *, scale):
    n_h = q_ref.shape[1] // (2 * LANES)
    q = jnp.dot(h_ref[...].astype(BF16), wq_ref[...], preferred_element_type=F32) * scale
    qf = onesq_ref[...]
    for j in range(F_PIECES):
        qf = qf + jnp.dot(fp_ref[:, j * LANES:(j + 1) * LANES], pq_ref[j],
                          preferred_element_type=F32)
    first = lax.broadcasted_iota(jnp.int32, (1, LANES), 1) < HEAD_DIM
    parts = []
    for hd in range(n_h):
        p, e = divmod(hd, 2)
        qp = q[:, p * LANES:(p + 1) * LANES]
        parts += [jnp.where(first if e == 0 else jnp.logical_not(first), qp, 0.0),
                  qf[:, hd * LANES:(hd + 1) * LANES]]
    q_ref[...] = jnp.concatenate(parts, axis=1).astype(BF16)


def _qproj(h, w_q, fpieces, pq, ones_q, scale, tile=512):
    n, d = h.shape
    n_h = d // HEAD_DIM
    const2 = lambda i: (0, 0)
    return pl.pallas_call(
        functools.partial(_qproj_kernel, scale=scale),
        grid=(n // tile,),
        in_specs=[pl.BlockSpec((tile, d), lambda i: (i, 0)),
                  pl.BlockSpec((d, d), const2),
                  pl.BlockSpec((tile, F_PIECES * LANES), lambda i: (i, 0)),
                  pl.BlockSpec(pq.shape, lambda i: (0, 0, 0)),
                  pl.BlockSpec(ones_q.shape, const2)],
        out_specs=pl.BlockSpec((tile, n_h * 2 * LANES), lambda i: (i, 0)),
        out_shape=jax.ShapeDtypeStruct((n, n_h * 2 * LANES), BF16),
        compiler_params=_params(("arbitrary",)),
        name="q_proj",
    )(h, w_q.astype(BF16), fpieces, pq, ones_q)


def _attn_kernel(q_ref, k_ref, vt_ref, o_ref, sa_ref, sb_ref, *, tq, tk):
    i = pl.program_id(2)
    per_q = tq // tk
    qa = q_ref[0]
    qs = (qa[:, :2 * LANES], qa[:, 2 * LANES:])
    row = lax.broadcasted_iota(jnp.int32, (tk, tq), 0)
    col = lax.broadcasted_iota(jnp.int32, (tk, tq), 1)
    upper = lax.broadcasted_iota(jnp.int32, (LANES, 1), 0) < HEAD_DIM

    def logits(j, s_ref):
        kj = k_ref[0, pl.ds(pl.multiple_of(j * tk, tk), tk), :]
        for hd in range(2):
            s_ref[hd] = lax.dot_general(kj, qs[hd], (((1,), (1,)), ((), ())),
                                        preferred_element_type=F32)

    def softmax_pv(j, s_ref, carry, diag_offset):
        ms, ls, acc = carry
        vtj = vt_ref[0, 0, :, pl.ds(pl.multiple_of(j * tk, tk), tk)]
        new_ms, new_ls, alphas, outs = [], [], [], []
        for hd in range(2):
            sc = s_ref[hd]
            if diag_offset is not None:
                sc = jnp.where(row + diag_offset <= col, sc, NEG_INF)
            m_new = jnp.maximum(ms[hd], jnp.max(sc, axis=0, keepdims=True))
            pt = jnp.exp(sc - m_new).astype(BF16)
            alpha = jnp.exp(ms[hd] - m_new)
            r = jnp.dot(vtj, pt, preferred_element_type=F32)
            new_ms.append(m_new)
            new_ls.append(alpha * ls[hd] + r[LANES:LANES + 1, :])
            alphas.append(alpha)
            outs.append(r[:LANES, :])
        acc = jnp.where(upper, alphas[0], alphas[1]) * acc + jnp.where(upper, outs[0], outs[1])
        return tuple(new_ms), tuple(new_ls), acc

    def two_blocks(t, carry):
        j = t * 2
        logits(j + 1, sb_ref)
        carry = softmax_pv(j, sa_ref, carry, None)
        logits(j + 2, sa_ref)
        return softmax_pv(j + 1, sb_ref, carry, None)

    init = ((jnp.full((1, tq), NEG_INF, F32),) * 2, (jnp.zeros((1, tq), F32),) * 2,
            jnp.zeros((LANES, tq), F32))
    logits(0, sa_ref)
    first_diag = i * per_q
    carry = lax.fori_loop(0, first_diag // 2, two_blocks, init)
    bufs = (sa_ref, sb_ref)
    for u in range(per_q):
        if u + 1 < per_q:
            logits(first_diag + u + 1, bufs[(u + 1) % 2])
        carry = softmax_pv(first_diag + u, bufs[u % 2], carry, u * tk)
    _, ls, acc = carry
    out_t = acc / jnp.where(upper, ls[0], ls[1])
    o_ref[0] = out_t.T.astype(BF16)


def _attention(qaug, kaug, vaug_t, d, tq=512, tk=256):
    bsz, seq, _ = kaug.shape
    n_pairs = vaug_t.shape[1]
    assert (tq // tk) % 2 == 0
    return pl.pallas_call(
        functools.partial(_attn_kernel, tq=tq, tk=tk),
        grid=(bsz, n_pairs, seq // tq),
        in_specs=[pl.BlockSpec((1, tq, 4 * LANES), lambda b, p, i: (b, i, p)),
                  pl.BlockSpec((1, seq, 2 * LANES), lambda b, p, i: (b, 0, p)),
                  pl.BlockSpec((1, 1, LANES + ONES_ROWS, seq), lambda b, p, i: (b, p, 0, 0))],
        out_specs=pl.BlockSpec((1, tq, LANES), lambda b, p, i: (b, i, p)),
        out_shape=jax.ShapeDtypeStruct((bsz, seq, d), BF16),
        scratch_shapes=[pltpu.VMEM((2, tk, tq), F32), pltpu.VMEM((2, tk, tq), F32)],
        compiler_params=_params(("arbitrary", "arbitrary", "arbitrary")),
        name="fox_attention",
    )(qaug, kaug, vaug_t)


def _oproj_ln_kernel(o_ref, h_ref, wo_ref, g_ref, b_ref, out_ref, *, alpha):
    mix = jnp.dot(o_ref[...], wo_ref[...], preferred_element_type=F32)
    out_ref[...] = _layer_norm(alpha * h_ref[...] + mix, g_ref[...], b_ref[...])


def _oproj_ln(o, h, w_o, g, b, alpha, tile=512):
    n, d = h.shape
    vec = pl.BlockSpec((1, d), lambda i: (0, 0))
    tspec = pl.BlockSpec((tile, d), lambda i: (i, 0))
    return pl.pallas_call(
        functools.partial(_oproj_ln_kernel, alpha=alpha),
        grid=(n // tile,),
        in_specs=[tspec, tspec, pl.BlockSpec((d, d), lambda i: (0, 0)), vec, vec],
        out_specs=tspec,
        out_shape=jax.ShapeDtypeStruct((n, d), F32),
        compiler_params=_params(("arbitrary",)),
        name="oproj_ln",
    )(o, h, w_o.astype(BF16), g.reshape(1, d), b.reshape(1, d))


def kernel(x, pool_w, pool_scale, w_q, w_o, w_kvf, b_f, ln_g, ln_b, router_w, router_b,
           w_gate_up, w_down, ws_gate_up, ws_down):
    bsz, seq, d = x.shape
    depth = ln_g.shape[0]
    n_pool = pool_w.shape[0]
    alpha = float((2 * depth) ** 0.25)
    n = bsz * seq
    kaug = vaug_t = fpieces = None
    pk, ones_k, pq, ones_q = _gate_placement(d // HEAD_DIM)
    for l in range(depth):
        if l < n_pool:
            x = _pool_ln(x, pool_w[l], pool_scale[l], ln_g[l, 0], ln_b[l, 0], alpha)
        else:
            if l == n_pool:
                kaug, vaug_t, fpieces = _kvf(x, w_kvf, b_f, pk, ones_k)
            j = l - n_pool
            qaug = _qproj(x.reshape(n, d), w_q[j], fpieces.reshape(n, -1), pq, ones_q,
                          HEAD_DIM ** -0.5)
            o = _attention(qaug.reshape(bsz, seq, -1), kaug, vaug_t, d)
            x = _oproj_ln(o.reshape(n, d), x.reshape(n, d), w_o[j], ln_g[l, 0], ln_b[l, 0],
                          alpha).reshape(bsz, seq, d)
        x = _moe_ln(x.reshape(n, d), router_w[l], router_b[l], w_gate_up, w_down, l,
                    ws_gate_up[l], ws_down[l], ln_g[l, 1], ln_b[l, 1], alpha).reshape(bsz, seq, d)
    return x
```

```python
import functools

import jax
import jax.numpy as jnp
from jax import lax
from jax.experimental import pallas as pl
from jax.experimental.pallas import tpu as pltpu

POOL_WINDOWS = (2, 4, 8, 16)
POOL_HALO = 16
HEAD_DIM = 64
N_EXPERT_GROUPS = 8
TOPK_GROUPS = 4
TOP_K = 8
ROUTED_SCALE = 2.5
MOE_BLOCK = 256
DMA_UNROLL = 8
DMA_QUEUES = 2
DEST_TILE = 256
LN_EPS = 1e-5

LANES = 128
SUBLANES = 8
VMEM_LIMIT = 48 * 1024 * 1024

F32 = jnp.float32
BF16 = jnp.bfloat16
NEG_INF = float("-inf")


def _params(semantics):
    return pltpu.CompilerParams(dimension_semantics=semantics, vmem_limit_bytes=VMEM_LIMIT)


def _layer_norm(z, g, b):
    mu = jnp.mean(z, axis=-1, keepdims=True)
    zc = z - mu
    var = jnp.mean(zc * zc, axis=-1, keepdims=True)
    return zc * lax.rsqrt(var + LN_EPS) * g + b


def _silu(x):
    return x * jax.nn.sigmoid(x)


def _pool_ln_kernel(x_ref, w_ref, scale_ref, g_ref, b_ref, o_ref, ext_ref, *, alpha, tile):
    s = pl.program_id(1)
    d = x_ref.shape[2]
    c = d // len(POOL_WINDOWS)

    @pl.when(s == 0)
    def _():
        ext_ref[0:POOL_HALO, :] = jnp.zeros((POOL_HALO, d), F32)

    x = x_ref[0]
    ext_ref[POOL_HALO:POOL_HALO + tile, :] = x
    pos = (s * tile + 1 + lax.broadcasted_iota(jnp.int32, (tile, 1), 0)).astype(F32)
    ys = []
    for gi, w in enumerate(POOL_WINDOWS):
        cs = slice(gi * c, (gi + 1) * c)
        acc = ext_ref[:, cs]
        span = 1
        while span < w:
            acc = acc + pltpu.roll(acc, span, axis=0)
            span *= 2
        mean = acc[POOL_HALO:, :] / jnp.minimum(pos, float(w))
        mixed = mean - x[:, cs]
        ys.append(jnp.dot(mixed.astype(BF16), w_ref[gi], preferred_element_type=F32))
    y = jnp.concatenate(ys, axis=1) * scale_ref[...]
    o_ref[0] = _layer_norm(alpha * x + y, g_ref[...], b_ref[...])
    ext_ref[0:POOL_HALO, :] = x[tile - POOL_HALO:, :]


def _pool_ln(x, w, scale, g, b, alpha, tile=512):
    bsz, seq, d = x.shape
    ng, c, _ = w.shape
    vec = pl.BlockSpec((1, d), lambda i, j: (0, 0))
    return pl.pallas_call(
        functools.partial(_pool_ln_kernel, alpha=alpha, tile=tile),
        grid=(bsz, seq // tile),
        in_specs=[pl.BlockSpec((1, tile, d), lambda i, j: (i, j, 0)),
                  pl.BlockSpec((ng, c, c), lambda i, j: (0, 0, 0)),
                  vec, vec, vec],
        out_specs=pl.BlockSpec((1, tile, d), lambda i, j: (i, j, 0)),
        out_shape=jax.ShapeDtypeStruct(x.shape, F32),
        scratch_shapes=[pltpu.VMEM((POOL_HALO + tile, d), F32)],
        compiler_params=_params(("arbitrary", "arbitrary")),
        name="pool_ln",
    )(x, w.astype(BF16), scale.reshape(1, d), g.reshape(1, d), b.reshape(1, d))


def _first_argmax(v, iota, size):
    m = jnp.max(v, axis=0, keepdims=True)
    idx = jnp.min(jnp.where(v == m, iota, size), axis=0, keepdims=True)
    return m, idx


def _router_kernel(h_ref, wt_ref, b_ref, eidx_ref, gw_ref, rank_ref, cnt_ref, carry_ref, *, tile):
    i = pl.program_id(0)
    n_e = wt_ref.shape[0]
    per_group = n_e // N_EXPERT_GROUPS

    @pl.when(i == 0)
    def _():
        carry_ref[...] = jnp.zeros_like(carry_ref)

    logits = lax.dot_general(wt_ref[...], h_ref[...], (((1,), (1,)), ((), ())),
                             precision=lax.Precision.HIGHEST, preferred_element_type=F32)
    s = jax.nn.sigmoid(logits)
    sb = s + b_ref[...]

    iota_g = lax.broadcasted_iota(jnp.int32, (per_group, tile), 0)
    gscores = []
    for g in range(N_EXPERT_GROUPS):
        blk = sb[g * per_group:(g + 1) * per_group, :]
        m1, i1 = _first_argmax(blk, iota_g, per_group)
        m2 = jnp.max(jnp.where(iota_g == i1, NEG_INF, blk), axis=0, keepdims=True)
        gscores.append(m1 + m2)
    gs = jnp.concatenate(gscores, axis=0)
    iota_8 = lax.broadcasted_iota(jnp.int32, (N_EXPERT_GROUPS, tile), 0)
    gsel = jnp.zeros((N_EXPERT_GROUPS, tile), jnp.bool_)
    for _ in range(TOPK_GROUPS):
        _, gi = _first_argmax(gs, iota_8, N_EXPERT_GROUPS)
        hit = iota_8 == gi
        gsel = jnp.logical_or(gsel, hit)
        gs = jnp.where(hit, NEG_INF, gs)
    cand = jnp.concatenate(
        [jnp.where(gsel[g:g + 1, :], sb[g * per_group:(g + 1) * per_group, :], NEG_INF)
         for g in range(N_EXPERT_GROUPS)], axis=0)

    iota_e = lax.broadcasted_iota(jnp.int32, (n_e, tile), 0)
    member = jnp.zeros((n_e, tile), F32)
    idxs, svals = [], []
    for _ in range(TOP_K):
        _, ei = _first_argmax(cand, iota_e, n_e)
        hit = iota_e == ei
        idxs.append(ei)
        svals.append(jnp.sum(jnp.where(hit, s, 0.0), axis=0, keepdims=True))
        member = jnp.where(hit, 1.0, member)
        cand = jnp.where(hit, NEG_INF, cand)
    sv = jnp.concatenate(svals, axis=0)
    gw_ref[...] = sv / jnp.sum(sv, axis=0, keepdims=True) * ROUTED_SCALE
    eidx_ref[...] = jnp.concatenate(idxs, axis=0)

    r = lax.broadcasted_iota(jnp.int32, (tile, tile), 0)
    c = lax.broadcasted_iota(jnp.int32, (tile, tile), 1)
    earlier = (r < c).astype(BF16)
    before = jnp.dot(member.astype(BF16), earlier, preferred_element_type=F32) + carry_ref[...]
    ranks = [jnp.sum(jnp.where(iota_e == ei, before, 0.0), axis=0, keepdims=True) for ei in idxs]
    rank_ref[...] = jnp.concatenate(ranks, axis=0).astype(jnp.int32)
    carry_ref[...] = carry_ref[...] + jnp.sum(member, axis=1, keepdims=True)
    cnt_ref[...] = carry_ref[...].astype(jnp.int32)


def _router(h, w_router, b_router, tile=512):
    n, d = h.shape
    n_e = w_router.shape[1]
    out_kt = pl.BlockSpec((TOP_K, tile), lambda i: (0, i))
    return pl.pallas_call(
        functools.partial(_router_kernel, tile=tile),
        grid=(n // tile,),
        in_specs=[pl.BlockSpec((tile, d), lambda i: (i, 0)),
                  pl.BlockSpec((n_e, d), lambda i: (0, 0)),
                  pl.BlockSpec((n_e, 1), lambda i: (0, 0))],
        out_specs=[out_kt, out_kt, out_kt, pl.BlockSpec((n_e, 1), lambda i: (0, 0))],
        out_shape=[jax.ShapeDtypeStruct((TOP_K, n), jnp.int32),
                   jax.ShapeDtypeStruct((TOP_K, n), F32),
                   jax.ShapeDtypeStruct((TOP_K, n), jnp.int32),
                   jax.ShapeDtypeStruct((n_e, 1), jnp.int32)],
        scratch_shapes=[pltpu.VMEM((n_e, 1), F32)],
        compiler_params=_params(("arbitrary",)),
        name="router",
    )(h, w_router.T, b_router.reshape(n_e, 1))


def _dest_kernel(eidx_ref, rank_ref, pstart_ref, dest_ref):
    n_e = pstart_ref.shape[0]
    tile = eidx_ref.shape[1]
    iota_e = lax.broadcasted_iota(jnp.int32, (n_e, tile), 0)
    starts = pstart_ref[...]
    rows = [jnp.sum(jnp.where(iota_e == eidx_ref[k:k + 1, :], starts, 0), axis=0, keepdims=True)
            for k in range(TOP_K)]
    dest_ref[0] = jnp.concatenate(rows, axis=0) + rank_ref[...]


def _dest_rows(eidx, rank, pstart):
    n = eidx.shape[1]
    n_e = pstart.shape[0]
    kt = pl.BlockSpec((TOP_K, DEST_TILE), lambda i: (0, i))
    return pl.pallas_call(
        _dest_kernel,
        grid=(n // DEST_TILE,),
        in_specs=[kt, kt, pl.BlockSpec((n_e, 1), lambda i: (0, 0))],
        out_specs=pl.BlockSpec((1, TOP_K, DEST_TILE), lambda i: (i, 0, 0)),
        out_shape=jax.ShapeDtypeStruct((n // DEST_TILE, TOP_K, DEST_TILE), jnp.int32),
        compiler_params=_params(("arbitrary",)),
        name="dest_rows",
    )(eidx, rank, pstart.reshape(n_e, 1))


def _dispatch_kernel(dest_ref, h_ref, xg_hbm, sem):
    tiles = dest_ref.shape[0]
    for sub in range(tiles):
        def group(g, carry, sub=sub):
            off = pl.multiple_of(g * DMA_UNROLL, DMA_UNROLL)
            for u in range(DMA_UNROLL):
                for k in range(TOP_K):
                    pltpu.make_async_copy(h_ref.at[pl.ds(sub * DEST_TILE + off + u, 1)],
                                          xg_hbm.at[pl.ds(dest_ref[sub, k, off + u], 1)],
                                          sem).start(priority=k % DMA_QUEUES)
            return carry
        lax.fori_loop(0, DEST_TILE // DMA_UNROLL, group, 0)
    for _ in range(tiles * TOP_K):
        pltpu.make_async_copy(xg_hbm.at[pl.ds(0, DEST_TILE)], xg_hbm.at[pl.ds(0, DEST_TILE)],
                              sem).wait()


def _dispatch(dest, h, n_rows, tiles=2):
    n, d = h.shape
    rows = tiles * DEST_TILE
    return pl.pallas_call(
        _dispatch_kernel,
        grid=(n // rows,),
        in_specs=[pl.BlockSpec((tiles, TOP_K, DEST_TILE), lambda i: (i, 0, 0),
                               memory_space=pltpu.SMEM),
                  pl.BlockSpec((rows, d), lambda i: (i, 0))],
        out_specs=pl.BlockSpec(memory_space=pl.ANY),
        out_shape=jax.ShapeDtypeStruct((n_rows, d), F32),
        scratch_shapes=[pltpu.SemaphoreType.DMA(())],
        compiler_params=_params(("arbitrary",)),
        name="dispatch_rows",
    )(dest, h)


def _grouped_kernel(ps_ref, cnt_ref, xg_hbm, wgu_ref, wd_ref, y_hbm, xbuf, ybuf, wgu_bf, wd_bf,
                    isem, osem, done_ref, osize_ref):
    e = pl.program_id(0)
    n_rows = cnt_ref[e]
    base = ps_ref[e]
    n_chunks = (n_rows + MOE_BLOCK - 1) // MOE_BLOCK
    f = wd_ref.shape[2]
    row = lax.broadcasted_iota(jnp.int32, (MOE_BLOCK, 1), 0)

    def chunk_rows(c):
        valid = jnp.minimum(n_rows - c * MOE_BLOCK, MOE_BLOCK)
        return pl.multiple_of((valid + SUBLANES - 1) // SUBLANES * SUBLANES, SUBLANES)

    def in_copy(c, slot):
        size = chunk_rows(c)
        start = pl.multiple_of(base + c * MOE_BLOCK, MOE_BLOCK)
        return pltpu.make_async_copy(xg_hbm.at[pl.ds(start, size)],
                                     xbuf.at[slot, pl.ds(0, size)], isem.at[slot])

    def out_copy(start, size, slot):
        return pltpu.make_async_copy(ybuf.at[slot, pl.ds(0, size)],
                                     y_hbm.at[pl.ds(start, size)], osem.at[slot])

    @pl.when(e == 0)
    def _():
        done_ref[0] = 0
        xbuf[...] = jnp.zeros_like(xbuf)

    @pl.when(n_chunks > 0)
    def _():
        in_copy(0, lax.rem(done_ref[0], 2)).start()
        wgu_bf[...] = wgu_ref[0, 0].astype(BF16)
        wd_bf[...] = wd_ref[0, 0].astype(BF16)

    def chunk(c, carry):
        done = done_ref[0]
        slot = lax.rem(done, 2)

        @pl.when(c + 1 < n_chunks)
        def _():
            in_copy(c + 1, 1 - slot).start()

        in_copy(c, slot).wait()
        x = jnp.where(row < n_rows - c * MOE_BLOCK, xbuf[slot], 0.0).astype(BF16)
        gu = jnp.dot(x, wgu_bf[...], preferred_element_type=F32)
        act = _silu(gu[:, :f]) * gu[:, f:]
        y = jnp.dot(act.astype(BF16), wd_bf[...], preferred_element_type=F32)

        @pl.when(done >= 2)
        def _():
            out_copy(0, pl.multiple_of(osize_ref[slot], SUBLANES), slot).wait()

        ybuf[slot] = y
        size = chunk_rows(c)
        osize_ref[slot] = size
        out_copy(pl.multiple_of(base + c * MOE_BLOCK, MOE_BLOCK), size, slot).start()
        done_ref[0] = done + 1
        return carry

    lax.fori_loop(0, n_chunks, chunk, 0)

    @pl.when(e == pl.num_programs(0) - 1)
    def _():
        for back in (1, 2):
            @pl.when(done_ref[0] >= back)
            def _():
                slot = lax.rem(done_ref[0] - back, 2)
                out_copy(0, pl.multiple_of(osize_ref[slot], SUBLANES), slot).wait()


def _grouped_experts(xg, pstart, counts, w_gate_up, w_down, layer):
    p, d = xg.shape
    _, n_e, _, f2 = w_gate_up.shape
    f = f2 // 2
    grid_spec = pltpu.PrefetchScalarGridSpec(
        num_scalar_prefetch=2,
        grid=(n_e,),
        in_specs=[
            pl.BlockSpec(memory_space=pl.ANY),
            pl.BlockSpec((1, 1, d, f2), lambda e, ps, cnt: (layer, e, 0, 0)),
            pl.BlockSpec((1, 1, f, d), lambda e, ps, cnt: (layer, e, 0, 0)),
        ],
        out_specs=pl.BlockSpec(memory_space=pl.ANY),
        scratch_shapes=[pltpu.VMEM((2, MOE_BLOCK, d), F32), pltpu.VMEM((2, MOE_BLOCK, d), F32),
                        pltpu.VMEM((d, f2), BF16), pltpu.VMEM((f, d), BF16),
                        pltpu.SemaphoreType.DMA((2,)), pltpu.SemaphoreType.DMA((2,)),
                        pltpu.SMEM((1,), jnp.int32), pltpu.SMEM((2,), jnp.int32)],
    )
    return pl.pallas_call(
        _grouped_kernel,
        grid_spec=grid_spec,
        out_shape=jax.ShapeDtypeStruct((p, d), F32),
        compiler_params=_params(("arbitrary",)),
        name="grouped_experts",
    )(pstart, counts, xg, w_gate_up, w_down)


def _combine_ln_kernel(dpair_ref, dnext_ref, h_ref, gw_ref, y_hbm, wsgu_ref, wsd_ref, g_ref, b_ref,
                       o_ref, ybuf, sem, *, alpha):
    i = pl.program_id(0)
    n_steps = pl.num_programs(0)
    f = wsd_ref.shape[0]

    def start_gather(dref, sub, s):
        def group(g, carry):
            off = pl.multiple_of(g * DMA_UNROLL, DMA_UNROLL)
            for u in range(DMA_UNROLL):
                for k in range(TOP_K):
                    pltpu.make_async_copy(y_hbm.at[pl.ds(dref[sub, k, off + u], 1)],
                                          ybuf.at[s, k, pl.ds(off + u, 1)],
                                          sem.at[s]).start(priority=k % DMA_QUEUES)
            return carry
        lax.fori_loop(0, DEST_TILE // DMA_UNROLL, group, 0)

    def finish(s):
        rows = slice(s * DEST_TILE, (s + 1) * DEST_TILE)
        h = h_ref[rows, :]
        gu = jnp.dot(h.astype(BF16), wsgu_ref[...], preferred_element_type=F32)
        act = _silu(gu[:, :f]) * gu[:, f:]
        shared = jnp.dot(act.astype(BF16), wsd_ref[...], preferred_element_type=F32)
        for k in range(TOP_K):
            pltpu.make_async_copy(y_hbm.at[pl.ds(0, DEST_TILE)], ybuf.at[s, k], sem.at[s]).wait()
        gw = gw_ref[rows, :]
        routed = gw[:, 0:1] * ybuf[s, 0]
        for k in range(1, TOP_K):
            routed = routed + gw[:, k:k + 1] * ybuf[s, k]
        o_ref[rows, :] = _layer_norm(alpha * h + (routed + shared), g_ref[...], b_ref[...])

    @pl.when(i == 0)
    def _():
        start_gather(dpair_ref, 0, 0)

    start_gather(dpair_ref, 1, 1)
    finish(0)

    @pl.when(i + 1 < n_steps)
    def _():
        start_gather(dnext_ref, 0, 0)

    finish(1)


def _combine_ln(dest, h, gw, y_rows, ws_gate_up, ws_down, g, b, alpha):
    n, d = h.shape
    f2 = ws_gate_up.shape[1]
    nt = n // DEST_TILE
    rows = 2 * DEST_TILE
    vec = pl.BlockSpec((1, d), lambda i: (0, 0))
    return pl.pallas_call(
        functools.partial(_combine_ln_kernel, alpha=alpha),
        grid=(nt // 2,),
        in_specs=[pl.BlockSpec((2, TOP_K, DEST_TILE), lambda i: (i, 0, 0),
                               memory_space=pltpu.SMEM),
                  pl.BlockSpec((1, TOP_K, DEST_TILE),
                               lambda i: (jnp.minimum(2 * i + 2, nt - 1), 0, 0),
                               memory_space=pltpu.SMEM),
                  pl.BlockSpec((rows, d), lambda i: (i, 0)),
                  pl.BlockSpec((rows, TOP_K), lambda i: (i, 0)),
                  pl.BlockSpec(memory_space=pl.ANY),
                  pl.BlockSpec((d, f2), lambda i: (0, 0)),
                  pl.BlockSpec((f2 // 2, d), lambda i: (0, 0)), vec, vec],
        out_specs=pl.BlockSpec((rows, d), lambda i: (i, 0)),
        out_shape=jax.ShapeDtypeStruct((n, d), F32),
        scratch_shapes=[pltpu.VMEM((2, TOP_K, DEST_TILE, d), F32),
                        pltpu.SemaphoreType.DMA((2,))],
        compiler_params=_params(("arbitrary",)),
        name="combine_ln",
    )(dest, dest, h, gw, y_rows, ws_gate_up.astype(BF16), ws_down.astype(BF16),
      g.reshape(1, d), b.reshape(1, d))


def _moe_ln(h, w_router, b_router, w_gate_up, w_down, layer, ws_gate_up, ws_down, g, b, alpha):
    n, d = h.shape
    n_e = w_router.shape[1]
    eidx, gw, rank, counts = _router(h, w_router, b_router)

    nb = -(-(n * TOP_K + n_e * (MOE_BLOCK - 1)) // MOE_BLOCK)
    counts = counts[:, 0]
    padded = (counts + MOE_BLOCK - 1) // MOE_BLOCK * MOE_BLOCK
    pstart = (jnp.cumsum(padded) - padded).astype(jnp.int32)

    dest = _dest_rows(eidx, rank, pstart)
    xg = _dispatch(dest, h, nb * MOE_BLOCK)
    y_rows = _grouped_experts(xg, pstart, counts, w_gate_up, w_down, layer)
    return _combine_ln(dest, h, gw.T, y_rows, ws_gate_up, ws_down, g, b, alpha)


F_PIECES = 3
ONES_ROWS = 16


def _gate_placement(n_heads):
    import numpy as np
    n_pairs = n_heads // 2
    pk = np.zeros((F_PIECES, LANES, n_pairs * LANES), np.float32)
    ones_k = np.zeros((1, n_pairs * LANES), np.float32)
    pq = np.zeros((F_PIECES, LANES, n_heads * LANES), np.float32)
    ones_q = np.zeros((1, n_heads * LANES), np.float32)
    for h in range(n_heads):
        p, e = divmod(h, 2)
        for j in range(F_PIECES):
            pk[j, h, p * LANES + F_PIECES * e + j] = -1.0
            pq[j, h, h * LANES + 2 * F_PIECES + j] = 1.0
            ones_q[0, h * LANES + F_PIECES * e + j] = 1.0
            ones_k[0, p * LANES + 2 * F_PIECES + j] = 1.0
    return (jnp.asarray(pk, BF16), jnp.asarray(ones_k), jnp.asarray(pq, BF16), jnp.asarray(ones_q))


def _kvf_kernel(h_ref, wk_ref, wvt_ref, wf_ref, bf_ref, pk_ref, onesk_ref, k_ref, vt_ref, fp_ref,
                carry_ref, *, tile):
    s = pl.program_id(1)
    n_pairs = vt_ref.shape[1]

    @pl.when(s == 0)
    def _():
        carry_ref[...] = jnp.zeros_like(carry_ref)

    h = h_ref[0]
    hb = h.astype(BF16)
    z = jnp.dot(h, wf_ref[...], precision=lax.Precision.HIGHEST,
                preferred_element_type=F32) + bf_ref[...]
    logf = jnp.minimum(z, 0.0) - jnp.log1p(jnp.exp(-jnp.abs(z)))
    row = lax.broadcasted_iota(jnp.int32, (tile, 1), 0)
    span = 1
    while span < tile:
        logf = logf + jnp.where(row >= span, pltpu.roll(logf, span, axis=0), 0.0)
        span *= 2
    cum = logf + carry_ref[...]
    carry_ref[...] = cum[tile - 1:tile, :]

    pieces = []
    rest = cum
    for _ in range(F_PIECES):
        piece = rest.astype(BF16)
        pieces.append(piece)
        rest = rest - piece.astype(F32)
    fp_ref[0] = jnp.concatenate(pieces, axis=1)
    kf = onesk_ref[...]
    for j in range(F_PIECES):
        kf = kf + jnp.dot(pieces[j], pk_ref[j], preferred_element_type=F32)

    k = jnp.dot(hb, wk_ref[...], preferred_element_type=F32)
    parts = []
    for p in range(n_pairs):
        parts += [k[:, p * LANES:(p + 1) * LANES], kf[:, p * LANES:(p + 1) * LANES]]
    k_ref[0] = jnp.concatenate(parts, axis=1).astype(BF16)

    vt = lax.dot_general(wvt_ref[...], hb, (((1,), (1,)), ((), ())),
                         preferred_element_type=F32).astype(BF16)
    for p in range(n_pairs):
        vt_ref[0, p, 0:LANES, :] = vt[p * LANES:(p + 1) * LANES, :]
        vt_ref[0, p, LANES:LANES + ONES_ROWS, :] = jnp.ones((ONES_ROWS, tile), BF16)


def _kvf(h3, w_kvf, b_f, pk, ones_k, tile=512):
    bsz, seq, d = h3.shape
    n_h = b_f.shape[0]
    n_pairs = n_h // 2
    wk = w_kvf[:, :d].astype(BF16)
    wvt = w_kvf[:, d:2 * d].T.astype(BF16)
    wf = jnp.pad(w_kvf[:, 2 * d:], ((0, 0), (0, LANES - n_h)))
    bf = jnp.pad(b_f, (0, LANES - n_h)).reshape(1, LANES)
    const2 = lambda i, j: (0, 0)
    return pl.pallas_call(
        functools.partial(_kvf_kernel, tile=tile),
        grid=(bsz, seq // tile),
        in_specs=[pl.BlockSpec((1, tile, d), lambda i, j: (i, j, 0)),
                  pl.BlockSpec((d, d), const2), pl.BlockSpec((d, d), const2),
                  pl.BlockSpec((d, LANES), const2), pl.BlockSpec((1, LANES), const2),
                  pl.BlockSpec(pk.shape, lambda i, j: (0, 0, 0)),
                  pl.BlockSpec(ones_k.shape, const2)],
        out_specs=[pl.BlockSpec((1, tile, 2 * d), lambda i, j: (i, j, 0)),
                   pl.BlockSpec((1, n_pairs, LANES + ONES_ROWS, tile), lambda i, j: (i, 0, 0, j)),
                   pl.BlockSpec((1, tile, F_PIECES * LANES), lambda i, j: (i, j, 0))],
        out_shape=[jax.ShapeDtypeStruct((bsz, seq, 2 * d), BF16),
                   jax.ShapeDtypeStruct((bsz, n_pairs, LANES + ONES_ROWS, seq), BF16),
                   jax.ShapeDtypeStruct((bsz, seq, F_PIECES * LANES), BF16)],
        scratch_shapes=[pltpu.VMEM((1, LANES), F32)],
        compiler_params=_params(("arbitrary", "arbitrary")),
        name="kvf_proj",
    )(h3, wk, wvt, wf, bf, pk, ones_k)


def _qproj_kernel(h_ref, wq_ref, fp_ref, pq_ref, onesq_ref, q_ref, *, scale):
    n_h = q_ref.shape[1] // (2 * LANES)
    q = jnp.dot(h_ref[...].astype(BF16), wq_ref[...], preferred_element_type=F32) * scale
    qf = onesq_ref[...]
    for j in range(F_PIECES):
        qf = qf + jnp.dot(fp_ref[:, j * LANES:(j + 1) * LANES], pq_ref[j],
                          preferred_element_type=F32)
    first = lax.broadcasted_iota(jnp.int32, (1, LANES), 1) < HEAD_DIM
    parts = []
    for hd in range(n_h):
        p, e = divmod(hd, 2)
        qp = q[:, p * LANES:(p + 1) * LANES]
        parts += [jnp.where(first if e == 0 else jnp.logical_not(first), qp, 0.0),
                  qf[:, hd * LANES:(hd + 1) * LANES]]
    q_ref[...] = jnp.concatenate(parts, axis=1).astype(BF16)


def _qproj(h, w_q, fpieces, pq, ones_q, scale, tile=512):
    n, d = h.shape
    n_h = d // HEAD_DIM
    const2 = lambda i: (0, 0)
    return pl.pallas_call(
        functools.partial(_qproj_kernel, scale=scale),
        grid=(n // tile,),
        in_specs=[pl.BlockSpec((tile, d), lambda i: (i, 0)),
                  pl.BlockSpec((d, d), const2),
                  pl.BlockSpec((tile, F_PIECES * LANES), lambda i: (i, 0)),
                  pl.BlockSpec(pq.shape, lambda i: (0, 0, 0)),
                  pl.BlockSpec(ones_q.shape, const2)],
        out_specs=pl.BlockSpec((tile, n_h * 2 * LANES), lambda i: (i, 0)),
        out_shape=jax.ShapeDtypeStruct((n, n_h * 2 * LANES), BF16),
        compiler_params=_params(("arbitrary",)),
        name="q_proj",
    )(h, w_q.astype(BF16), fpieces, pq, ones_q)


def _attn_kernel(q_ref, k_ref, vt_ref, o_ref, sa_ref, sb_ref, *, tq, tk):
    i = pl.program_id(2)
    per_q = tq // tk
    qa = q_ref[0]
    qs = (qa[:, :2 * LANES], qa[:, 2 * LANES:])
    row = lax.broadcasted_iota(jnp.int32, (tk, tq), 0)
    col = lax.broadcasted_iota(jnp.int32, (tk, tq), 1)
    upper = lax.broadcasted_iota(jnp.int32, (LANES, 1), 0) < HEAD_DIM

    def logits(j, s_ref):
        kj = k_ref[0, pl.ds(pl.multiple_of(j * tk, tk), tk), :]
        for hd in range(2):
            s_ref[hd] = lax.dot_general(kj, qs[hd], (((1,), (1,)), ((), ())),
                                        preferred_element_type=F32)

    def softmax_pv(j, s_ref, carry, diag_offset):
        ms, ls, acc = carry
        vtj = vt_ref[0, 0, :, pl.ds(pl.multiple_of(j * tk, tk), tk)]
        new_ms, new_ls, alphas, outs = [], [], [], []
        for hd in range(2):
            sc = s_ref[hd]
            if diag_offset is not None:
                sc = jnp.where(row + diag_offset <= col, sc, NEG_INF)
            m_new = jnp.maximum(ms[hd], jnp.max(sc, axis=0, keepdims=True))
            pt = jnp.exp(sc - m_new).astype(BF16)
            alpha = jnp.exp(ms[hd] - m_new)
            r = jnp.dot(vtj, pt, preferred_element_type=F32)
            new_ms.append(m_new)
            new_ls.append(alpha * ls[hd] + r[LANES:LANES + 1, :])
            alphas.append(alpha)
            outs.append(r[:LANES, :])
        acc = jnp.where(upper, alphas[0], alphas[1]) * acc + jnp.where(upper, outs[0], outs[1])
        return tuple(new_ms), tuple(new_ls), acc

    def two_blocks(t, carry):
        j = t * 2
        logits(j + 1, sb_ref)
        carry = softmax_pv(j, sa_ref, carry, None)
        logits(j + 2, sa_ref)
        return softmax_pv(j + 1, sb_ref, carry, None)

    init = ((jnp.full((1, tq), NEG_INF, F32),) * 2, (jnp.zeros((1, tq), F32),) * 2,
            jnp.zeros((LANES, tq), F32))
    logits(0, sa_ref)
    first_diag = i * per_q
    carry = lax.fori_loop(0, first_diag // 2, two_blocks, init)
    bufs = (sa_ref, sb_ref)
    for u in range(per_q):
        if u + 1 < per_q:
            logits(first_diag + u + 1, bufs[(u + 1) % 2])
        carry = softmax_pv(first_diag + u, bufs[u % 2], carry, u * tk)
    _, ls, acc = carry
    out_t = acc / jnp.where(upper, ls[0], ls[1])
    o_ref[0] = out_t.T.astype(BF16)


def _attention(qaug, kaug, vaug_t, d, tq=512, tk=256):
    bsz, seq, _ = kaug.shape
    n_pairs = vaug_t.shape[1]
    assert (tq // tk) % 2 == 0
    return pl.pallas_call(
        functools.partial(_attn_kernel, tq=tq, tk=tk),
        grid=(bsz, n_pairs, seq // tq),
        in_specs=[pl.BlockSpec((1, tq, 4 * LANES), lambda b, p, i: (b, i, p)),
                  pl.BlockSpec((1, seq, 2 * LANES), lambda b, p, i: (b, 0, p)),
                  pl.BlockSpec((1, 1, LANES + ONES_ROWS, seq), lambda b, p, i: (b, p, 0, 0))],
        out_specs=pl.BlockSpec((1, tq, LANES), lambda b, p, i: (b, i, p)),
        out_shape=jax.ShapeDtypeStruct((bsz, seq, d), BF16),
        scratch_shapes=[pltpu.VMEM((2, tk, tq), F32), pltpu.VMEM((2, tk, tq), F32)],
        compiler_params=_params(("arbitrary", "arbitrary", "arbitrary")),
        name="fox_attention",
    )(qaug, kaug, vaug_t)


def _oproj_ln_kernel(o_ref, h_ref, wo_ref, g_ref, b_ref, out_ref, *, alpha):
    mix = jnp.dot(o_ref[...], wo_ref[...], preferred_element_type=F32)
    out_ref[...] = _layer_norm(alpha * h_ref[...] + mix, g_ref[...], b_ref[...])


def _oproj_ln(o, h, w_o, g, b, alpha, tile=512):
    n, d = h.shape
    vec = pl.BlockSpec((1, d), lambda i: (0, 0))
    tspec = pl.BlockSpec((tile, d), lambda i: (i, 0))
    return pl.pallas_call(
        functools.partial(_oproj_ln_kernel, alpha=alpha),
        grid=(n // tile,),
        in_specs=[tspec, tspec, pl.BlockSpec((d, d), lambda i: (0, 0)), vec, vec],
        out_specs=tspec,
        out_shape=jax.ShapeDtypeStruct((n, d), F32),
        compiler_params=_params(("arbitrary",)),
        name="oproj_ln",
    )(o, h, w_o.astype(BF16), g.reshape(1, d), b.reshape(1, d))


def kernel(x, pool_w, pool_scale, w_q, w_o, w_kvf, b_f, ln_g, ln_b, router_w, router_b,
           w_gate_up, w_down, ws_gate_up, ws_down):
    bsz, seq, d = x.shape
    depth = ln_g.shape[0]
    n_pool = pool_w.shape[0]
    alpha = float((2 * depth) ** 0.25)
    n = bsz * seq
    kaug = vaug_t = fpieces = None
    pk, ones_k, pq, ones_q = _gate_placement(d // HEAD_DIM)
    for l in range(depth):
        if l < n_pool:
            x = _pool_ln(x, pool_w[l], pool_scale[l], ln_g[l, 0], ln_b[l, 0], alpha)
        else:
            if l == n_pool:
                kaug, vaug_t, fpieces = _kvf(x, w_kvf, b_f, pk, ones_k)
            j = l - n_pool
            qaug = _qproj(x.reshape(n, d), w_q[j], fpieces.reshape(n, -1), pq, ones_q,
                          HEAD_DIM ** -0.5)
            o = _attention(qaug.reshape(bsz, seq, -1), kaug, vaug_t, d)
            x = _oproj_ln(o.reshape(n, d), x.reshape(n, d), w_o[j], ln_g[l, 0], ln_b[l, 0],
                          alpha).reshape(bsz, seq, d)
        x = _moe_ln(x.reshape(n, d), router_w[l], router_b[l], w_gate_up, w_down, l,
                    ws_gate_up[l], ws_down[l], ln_g[l, 1], ln_b[l, 1], alpha).reshape(bsz, seq, d)
    return x
```

```python
import functools

import jax
import jax.numpy as jnp
from jax import lax
from jax.experimental import pallas as pl
from jax.experimental.pallas import tpu as pltpu

POOL_WINDOWS = (2, 4, 8, 16)
POOL_HALO = 16
HEAD_DIM = 64
N_EXPERT_GROUPS = 8
TOPK_GROUPS = 4
TOP_K = 8
ROUTED_SCALE = 2.5
MOE_BLOCK = 256
DMA_UNROLL = 8
DMA_QUEUES = 2
DEST_TILE = 256
LN_EPS = 1e-5

LANES = 128
SUBLANES = 8
VMEM_LIMIT = 48 * 1024 * 1024

F32 = jnp.float32
BF16 = jnp.bfloat16
NEG_INF = float("-inf")


def _params(semantics):
    return pltpu.CompilerParams(dimension_semantics=semantics, vmem_limit_bytes=VMEM_LIMIT)


def _layer_norm(z, g, b):
    mu = jnp.mean(z, axis=-1, keepdims=True)
    zc = z - mu
    var = jnp.mean(zc * zc, axis=-1, keepdims=True)
    return zc * lax.rsqrt(var + LN_EPS) * g + b


def _silu(x):
    return x * jax.nn.sigmoid(x)


def _pool_ln_kernel(x_ref, w_ref, scale_ref, g_ref, b_ref, o_ref, ext_ref, *, alpha, tile):
    s = pl.program_id(1)
    d = x_ref.shape[2]
    c = d // len(POOL_WINDOWS)

    @pl.when(s == 0)
    def _():
        ext_ref[0:POOL_HALO, :] = jnp.zeros((POOL_HALO, d), F32)

    x = x_ref[0]
    ext_ref[POOL_HALO:POOL_HALO + tile, :] = x
    pos = (s * tile + 1 + lax.broadcasted_iota(jnp.int32, (tile, 1), 0)).astype(F32)
    ys = []
    for gi, w in enumerate(POOL_WINDOWS):
        cs = slice(gi * c, (gi + 1) * c)
        acc = ext_ref[:, cs]
        span = 1
        while span < w:
            acc = acc + pltpu.roll(acc, span, axis=0)
            span *= 2
        mean = acc[POOL_HALO:, :] / jnp.minimum(pos, float(w))
        mixed = mean - x[:, cs]
        ys.append(jnp.dot(mixed.astype(BF16), w_ref[gi], preferred_element_type=F32))
    y = jnp.concatenate(ys, axis=1) * scale_ref[...]
    o_ref[0] = _layer_norm(alpha * x + y, g_ref[...], b_ref[...])
    ext_ref[0:POOL_HALO, :] = x[tile - POOL_HALO:, :]


def _pool_ln(x, w, scale, g, b, alpha, tile=512):
    bsz, seq, d = x.shape
    ng, c, _ = w.shape
    vec = pl.BlockSpec((1, d), lambda i, j: (0, 0))
    return pl.pallas_call(
        functools.partial(_pool_ln_kernel, alpha=alpha, tile=tile),
        grid=(bsz, seq // tile),
        in_specs=[pl.BlockSpec((1, tile, d), lambda i, j: (i, j, 0)),
                  pl.BlockSpec((ng, c, c), lambda i, j: (0, 0, 0)),
                  vec, vec, vec],
        out_specs=pl.BlockSpec((1, tile, d), lambda i, j: (i, j, 0)),
        out_shape=jax.ShapeDtypeStruct(x.shape, F32),
        scratch_shapes=[pltpu.VMEM((POOL_HALO + tile, d), F32)],
        compiler_params=_params(("arbitrary", "arbitrary")),
        name="pool_ln",
    )(x, w.astype(BF16), scale.reshape(1, d), g.reshape(1, d), b.reshape(1, d))


def _first_argmax(v, iota, size):
    m = jnp.max(v, axis=0, keepdims=True)
    idx = jnp.min(jnp.where(v == m, iota, size), axis=0, keepdims=True)
    return m, idx


def _router_kernel(h_ref, wt_ref, b_ref, eidx_ref, gw_ref, rank_ref, cnt_ref, carry_ref, *, tile):
    i = pl.program_id(0)
    n_e = wt_ref.shape[0]
    per_group = n_e // N_EXPERT_GROUPS

    @pl.when(i == 0)
    def _():
        carry_ref[...] = jnp.zeros_like(carry_ref)

    logits = lax.dot_general(wt_ref[...], h_ref[...], (((1,), (1,)), ((), ())),
                             precision=lax.Precision.HIGHEST, preferred_element_type=F32)
    s = jax.nn.sigmoid(logits)
    sb = s + b_ref[...]

    iota_g = lax.broadcasted_iota(jnp.int32, (per_group, tile), 0)
    gscores = []
    for g in range(N_EXPERT_GROUPS):
        blk = sb[g * per_group:(g + 1) * per_group, :]
        m1, i1 = _first_argmax(blk, iota_g, per_group)
        m2 = jnp.max(jnp.where(iota_g == i1, NEG_INF, blk), axis=0, keepdims=True)
        gscores.append(m1 + m2)
    gs = jnp.concatenate(gscores, axis=0)
    iota_8 = lax.broadcasted_iota(jnp.int32, (N_EXPERT_GROUPS, tile), 0)
    gsel = jnp.zeros((N_EXPERT_GROUPS, tile), jnp.bool_)
    for _ in range(TOPK_GROUPS):
        _, gi = _first_argmax(gs, iota_8, N_EXPERT_GROUPS)
        hit = iota_8 == gi
        gsel = jnp.logical_or(gsel, hit)
        gs = jnp.where(hit, NEG_INF, gs)
    cand = jnp.concatenate(
        [jnp.where(gsel[g:g + 1, :], sb[g * per_group:(g + 1) * per_group, :], NEG_INF)
         for g in range(N_EXPERT_GROUPS)], axis=0)

    iota_e = lax.broadcasted_iota(jnp.int32, (n_e, tile), 0)
    member = jnp.zeros((n_e, tile), F32)
    idxs, svals = [], []
    for _ in range(TOP_K):
        _, ei = _first_argmax(cand, iota_e, n_e)
        hit = iota_e == ei
        idxs.append(ei)
        svals.append(jnp.sum(jnp.where(hit, s, 0.0), axis=0, keepdims=True))
        member = jnp.where(hit, 1.0, member)
        cand = jnp.where(hit, NEG_INF, cand)
    sv = jnp.concatenate(svals, axis=0)
    gw_ref[...] = sv / jnp.sum(sv, axis=0, keepdims=True) * ROUTED_SCALE
    eidx_ref[...] = jnp.concatenate(idxs, axis=0)

    r = lax.broadcasted_iota(jnp.int32, (tile, tile), 0)
    c = lax.broadcasted_iota(jnp.int32, (tile, tile), 1)
    earlier = (r < c).astype(BF16)
    before = jnp.dot(member.astype(BF16), earlier, preferred_element_type=F32) + carry_ref[...]
    ranks = [jnp.sum(jnp.where(iota_e == ei, before, 0.0), axis=0, keepdims=True) for ei in idxs]
    rank_ref[...] = jnp.concatenate(ranks, axis=0).astype(jnp.int32)
    carry_ref[...] = carry_ref[...] + jnp.sum(member, axis=1, keepdims=True)
    cnt_ref[...] = carry_ref[...].astype(jnp.int32)


def _router(h, w_router, b_router, tile=512):
    n, d = h.shape
    n_e = w_router.shape[1]
    out_kt = pl.BlockSpec((TOP_K, tile), lambda i: (0, i))
    return pl.pallas_call(
        functools.partial(_router_kernel, tile=tile),
        grid=(n // tile,),
        in_specs=[pl.BlockSpec((tile, d), lambda i: (i, 0)),
                  pl.BlockSpec((n_e, d), lambda i: (0, 0)),
                  pl.BlockSpec((n_e, 1), lambda i: (0, 0))],
        out_specs=[out_kt, out_kt, out_kt, pl.BlockSpec((n_e, 1), lambda i: (0, 0))],
        out_shape=[jax.ShapeDtypeStruct((TOP_K, n), jnp.int32),
                   jax.ShapeDtypeStruct((TOP_K, n), F32),
                   jax.ShapeDtypeStruct((TOP_K, n), jnp.int32),
                   jax.ShapeDtypeStruct((n_e, 1), jnp.int32)],
        scratch_shapes=[pltpu.VMEM((n_e, 1), F32)],
        compiler_params=_params(("arbitrary",)),
        name="router",
    )(h, w_router.T, b_router.reshape(n_e, 1))


def _dest_kernel(eidx_ref, rank_ref, pstart_ref, dest_ref):
    n_e = pstart_ref.shape[0]
    tile = eidx_ref.shape[1]
    iota_e = lax.broadcasted_iota(jnp.int32, (n_e, tile), 0)
    starts = pstart_ref[...]
    rows = [jnp.sum(jnp.where(iota_e == eidx_ref[k:k + 1, :], starts, 0), axis=0, keepdims=True)
            for k in range(TOP_K)]
    dest_ref[0] = jnp.concatenate(rows, axis=0) + rank_ref[...]


def _dest_rows(eidx, rank, pstart):
    n = eidx.shape[1]
    n_e = pstart.shape[0]
    kt = pl.BlockSpec((TOP_K, DEST_TILE), lambda i: (0, i))
    return pl.pallas_call(
        _dest_kernel,
        grid=(n // DEST_TILE,),
        in_specs=[kt, kt, pl.BlockSpec((n_e, 1), lambda i: (0, 0))],
        out_specs=pl.BlockSpec((1, TOP_K, DEST_TILE), lambda i: (i, 0, 0)),
        out_shape=jax.ShapeDtypeStruct((n // DEST_TILE, TOP_K, DEST_TILE), jnp.int32),
        compiler_params=_params(("arbitrary",)),
        name="dest_rows",
    )(eidx, rank, pstart.reshape(n_e, 1))


def _dispatch_kernel(dest_ref, h_ref, xg_hbm, sem):
    tiles = dest_ref.shape[0]
    for sub in range(tiles):
        def group(g, carry, sub=sub):
            off = pl.multiple_of(g * DMA_UNROLL, DMA_UNROLL)
            for u in range(DMA_UNROLL):
                for k in range(TOP_K):
                    pltpu.make_async_copy(h_ref.at[pl.ds(sub * DEST_TILE + off + u, 1)],
                                          xg_hbm.at[pl.ds(dest_ref[sub, k, off + u], 1)],
                                          sem).start(priority=k % DMA_QUEUES)
            return carry
        lax.fori_loop(0, DEST_TILE // DMA_UNROLL, group, 0)
    for _ in range(tiles * TOP_K):
        pltpu.make_async_copy(xg_hbm.at[pl.ds(0, DEST_TILE)], xg_hbm.at[pl.ds(0, DEST_TILE)],
                              sem).wait()


def _dispatch(dest, h, n_rows, tiles=2):
    n, d = h.shape
    rows = tiles * DEST_TILE
    return pl.pallas_call(
        _dispatch_kernel,
        grid=(n // rows,),
        in_specs=[pl.BlockSpec((tiles, TOP_K, DEST_TILE), lambda i: (i, 0, 0),
                               memory_space=pltpu.SMEM),
                  pl.BlockSpec((rows, d), lambda i: (i, 0))],
        out_specs=pl.BlockSpec(memory_space=pl.ANY),
        out_shape=jax.ShapeDtypeStruct((n_rows, d), F32),
        scratch_shapes=[pltpu.SemaphoreType.DMA(())],
        compiler_params=_params(("arbitrary",)),
        name="dispatch_rows",
    )(dest, h)


def _grouped_kernel(ps_ref, cnt_ref, xg_hbm, wgu_ref, wd_ref, y_hbm, xfirst, xbuf, ybuf, wgu_bf,
                    wd_bf, fsem, isem, osem, done_ref, osize_ref):
    e = pl.program_id(0)
    last = pl.num_programs(0) - 1
    n_rows = cnt_ref[e]
    base = ps_ref[e]
    n_chunks = (n_rows + MOE_BLOCK - 1) // MOE_BLOCK
    f = wd_ref.shape[2]
    row = lax.broadcasted_iota(jnp.int32, (MOE_BLOCK, 1), 0)

    def dma_rows(count, c):
        valid = jnp.minimum(count - c * MOE_BLOCK, MOE_BLOCK)
        return pl.multiple_of((valid + SUBLANES - 1) // SUBLANES * SUBLANES, SUBLANES)

    def first_copy(ex):
        size = dma_rows(cnt_ref[ex], 0)
        start = pl.multiple_of(ps_ref[ex], MOE_BLOCK)
        slot = lax.rem(ex, 2)
        return pltpu.make_async_copy(xg_hbm.at[pl.ds(start, size)],
                                     xfirst.at[slot, pl.ds(0, size)], fsem.at[slot])

    def in_copy(c, slot):
        size = dma_rows(n_rows, c)
        start = pl.multiple_of(base + c * MOE_BLOCK, MOE_BLOCK)
        return pltpu.make_async_copy(xg_hbm.at[pl.ds(start, size)],
                                     xbuf.at[slot, pl.ds(0, size)], isem.at[slot])

    def out_copy(start, size, slot):
        return pltpu.make_async_copy(ybuf.at[slot, pl.ds(0, size)],
                                     y_hbm.at[pl.ds(start, size)], osem.at[slot])

    def process(x32, c):
        x = jnp.where(row < n_rows - c * MOE_BLOCK, x32, 0.0).astype(BF16)
        gu = jnp.dot(x, wgu_bf[...], preferred_element_type=F32)
        act = _silu(gu[:, :f]) * gu[:, f:]
        y = jnp.dot(act.astype(BF16), wd_bf[...], preferred_element_type=F32)
        done = done_ref[0]
        slot = lax.rem(done, 2)

        @pl.when(done >= 2)
        def _():
            out_copy(0, pl.multiple_of(osize_ref[slot], SUBLANES), slot).wait()

        ybuf[slot] = y
        size = dma_rows(n_rows, c)
        osize_ref[slot] = size
        out_copy(pl.multiple_of(base + c * MOE_BLOCK, MOE_BLOCK), size, slot).start()
        done_ref[0] = done + 1

    @pl.when(e == 0)
    def _():
        done_ref[0] = 0
        xfirst[...] = jnp.zeros_like(xfirst)
        xbuf[...] = jnp.zeros_like(xbuf)

        @pl.when(n_rows > 0)
        def _():
            first_copy(0).start()

    nxt = jnp.minimum(e + 1, last)

    @pl.when(jnp.logical_and(e < last, cnt_ref[nxt] > 0))
    def _():
        first_copy(nxt).start()

    @pl.when(n_chunks > 0)
    def _():
        wgu_bf[...] = wgu_ref[0, 0].astype(BF16)
        wd_bf[...] = wd_ref[0, 0].astype(BF16)

        @pl.when(n_chunks > 1)
        def _():
            in_copy(1, 1).start()

        first_copy(e).wait()
        process(xfirst[lax.rem(e, 2)], 0)

        def chunk(c, carry):
            slot = lax.rem(c, 2)

            @pl.when(c + 1 < n_chunks)
            def _():
                in_copy(c + 1, 1 - slot).start()

            in_copy(c, slot).wait()
            process(xbuf[slot], c)
            return carry

        lax.fori_loop(1, n_chunks, chunk, 0)

    @pl.when(e == last)
    def _():
        for back in (1, 2):
            @pl.when(done_ref[0] >= back)
            def _():
                slot = lax.rem(done_ref[0] - back, 2)
                out_copy(0, pl.multiple_of(osize_ref[slot], SUBLANES), slot).wait()


def _grouped_experts(xg, pstart, counts, w_gate_up, w_down, layer):
    p, d = xg.shape
    _, n_e, _, f2 = w_gate_up.shape
    f = f2 // 2
    grid_spec = pltpu.PrefetchScalarGridSpec(
        num_scalar_prefetch=2,
        grid=(n_e,),
        in_specs=[
            pl.BlockSpec(memory_space=pl.ANY),
            pl.BlockSpec((1, 1, d, f2), lambda e, ps, cnt: (layer, e, 0, 0)),
            pl.BlockSpec((1, 1, f, d), lambda e, ps, cnt: (layer, e, 0, 0)),
        ],
        out_specs=pl.BlockSpec(memory_space=pl.ANY),
        scratch_shapes=[pltpu.VMEM((2, MOE_BLOCK, d), F32), pltpu.VMEM((2, MOE_BLOCK, d), F32),
                        pltpu.VMEM((2, MOE_BLOCK, d), F32),
                        pltpu.VMEM((d, f2), BF16), pltpu.VMEM((f, d), BF16),
                        pltpu.SemaphoreType.DMA((2,)), pltpu.SemaphoreType.DMA((2,)),
                        pltpu.SemaphoreType.DMA((2,)),
                        pltpu.SMEM((1,), jnp.int32), pltpu.SMEM((2,), jnp.int32)],
    )
    return pl.pallas_call(
        _grouped_kernel,
        grid_spec=grid_spec,
        out_shape=jax.ShapeDtypeStruct((p, d), F32),
        compiler_params=_params(("arbitrary",)),
        name="grouped_experts",
    )(pstart, counts, xg, w_gate_up, w_down)


def _combine_ln_kernel(dpair_ref, dnext_ref, h_ref, gw_ref, y_hbm, wsgu_ref, wsd_ref, g_ref, b_ref,
                       o_ref, ybuf, sem, *, alpha):
    i = pl.program_id(0)
    n_steps = pl.num_programs(0)
    f = wsd_ref.shape[0]

    def start_gather(dref, sub, s):
        def group(g, carry):
            off = pl.multiple_of(g * DMA_UNROLL, DMA_UNROLL)
            for u in range(DMA_UNROLL):
                for k in range(TOP_K):
                    pltpu.make_async_copy(y_hbm.at[pl.ds(dref[sub, k, off + u], 1)],
                                          ybuf.at[s, k, pl.ds(off + u, 1)],
                                          sem.at[s]).start(priority=k % DMA_QUEUES)
            return carry
        lax.fori_loop(0, DEST_TILE // DMA_UNROLL, group, 0)

    def finish(s):
        rows = slice(s * DEST_TILE, (s + 1) * DEST_TILE)
        h = h_ref[rows, :]
        gu = jnp.dot(h.astype(BF16), wsgu_ref[...], preferred_element_type=F32)
        act = _silu(gu[:, :f]) * gu[:, f:]
        shared = jnp.dot(act.astype(BF16), wsd_ref[...], preferred_element_type=F32)
        for k in range(TOP_K):
            pltpu.make_async_copy(y_hbm.at[pl.ds(0, DEST_TILE)], ybuf.at[s, k], sem.at[s]).wait()
        gw = gw_ref[rows, :]
        routed = gw[:, 0:1] * ybuf[s, 0]
        for k in range(1, TOP_K):
            routed = routed + gw[:, k:k + 1] * ybuf[s, k]
        o_ref[rows, :] = _layer_norm(alpha * h + (routed + shared), g_ref[...], b_ref[...])

    @pl.when(i == 0)
    def _():
        start_gather(dpair_ref, 0, 0)

    start_gather(dpair_ref, 1, 1)
    finish(0)

    @pl.when(i + 1 < n_steps)
    def _():
        start_gather(dnext_ref, 0, 0)

    finish(1)


def _combine_ln(dest, h, gw, y_rows, ws_gate_up, ws_down, g, b, alpha):
    n, d = h.shape
    f2 = ws_gate_up.shape[1]
    nt = n // DEST_TILE
    rows = 2 * DEST_TILE
    vec = pl.BlockSpec((1, d), lambda i: (0, 0))
    return pl.pallas_call(
        functools.partial(_combine_ln_kernel, alpha=alpha),
        grid=(nt // 2,),
        in_specs=[pl.BlockSpec((2, TOP_K, DEST_TILE), lambda i: (i, 0, 0),
                               memory_space=pltpu.SMEM),
                  pl.BlockSpec((1, TOP_K, DEST_TILE),
                               lambda i: (jnp.minimum(2 * i + 2, nt - 1), 0, 0),
                               memory_space=pltpu.SMEM),
                  pl.BlockSpec((rows, d), lambda i: (i, 0)),
                  pl.BlockSpec((rows, TOP_K), lambda i: (i, 0)),
                  pl.BlockSpec(memory_space=pl.ANY),
                  pl.BlockSpec((d, f2), lambda i: (0, 0)),
                  pl.BlockSpec((f2 // 2, d), lambda i: (0, 0)), vec, vec],
        out_specs=pl.BlockSpec((rows, d), lambda i: (i, 0)),
        out_shape=jax.ShapeDtypeStruct((n, d), F32),
        scratch_shapes=[pltpu.VMEM((2, TOP_K, DEST_TILE, d), F32),
                        pltpu.SemaphoreType.DMA((2,))],
        compiler_params=_params(("arbitrary",)),
        name="combine_ln",
    )(dest, dest, h, gw, y_rows, ws_gate_up.astype(BF16), ws_down.astype(BF16),
      g.reshape(1, d), b.reshape(1, d))


def _moe_ln(h, w_router, b_router, w_gate_up, w_down, layer, ws_gate_up, ws_down, g, b, alpha):
    n, d = h.shape
    n_e = w_router.shape[1]
    eidx, gw, rank, counts = _router(h, w_router, b_router)

    nb = -(-(n * TOP_K + n_e * (MOE_BLOCK - 1)) // MOE_BLOCK)
    counts = counts[:, 0]
    padded = (counts + MOE_BLOCK - 1) // MOE_BLOCK * MOE_BLOCK
    pstart = (jnp.cumsum(padded) - padded).astype(jnp.int32)

    dest = _dest_rows(eidx, rank, pstart)
    xg = _dispatch(dest, h, nb * MOE_BLOCK)
    y_rows = _grouped_experts(xg, pstart, counts, w_gate_up, w_down, layer)
    return _combine_ln(dest, h, gw.T, y_rows, ws_gate_up, ws_down, g, b, alpha)


F_PIECES = 3
ONES_ROWS = 16


def _gate_placement(n_heads):
    import numpy as np
    n_pairs = n_heads // 2
    pk = np.zeros((F_PIECES, LANES, n_pairs * LANES), np.float32)
    ones_k = np.zeros((1, n_pairs * LANES), np.float32)
    pq = np.zeros((F_PIECES, LANES, n_heads * LANES), np.float32)
    ones_q = np.zeros((1, n_heads * LANES), np.float32)
    for h in range(n_heads):
        p, e = divmod(h, 2)
        for j in range(F_PIECES):
            pk[j, h, p * LANES + F_PIECES * e + j] = -1.0
            pq[j, h, h * LANES + 2 * F_PIECES + j] = 1.0
            ones_q[0, h * LANES + F_PIECES * e + j] = 1.0
            ones_k[0, p * LANES + 2 * F_PIECES + j] = 1.0
    return (jnp.asarray(pk, BF16), jnp.asarray(ones_k), jnp.asarray(pq, BF16), jnp.asarray(ones_q))


def _kvf_kernel(h_ref, wk_ref, wvt_ref, wf_ref, bf_ref, pk_ref, onesk_ref, k_ref, vt_ref, fp_ref,
                carry_ref, *, tile):
    s = pl.program_id(1)
    n_pairs = vt_ref.shape[1]

    @pl.when(s == 0)
    def _():
        carry_ref[...] = jnp.zeros_like(carry_ref)

    h = h_ref[0]
    hb = h.astype(BF16)
    z = jnp.dot(h, wf_ref[...], precision=lax.Precision.HIGHEST,
                preferred_element_type=F32) + bf_ref[...]
    logf = jnp.minimum(z, 0.0) - jnp.log1p(jnp.exp(-jnp.abs(z)))
    row = lax.broadcasted_iota(jnp.int32, (tile, 1), 0)
    span = 1
    while span < tile:
        logf = logf + jnp.where(row >= span, pltpu.roll(logf, span, axis=0), 0.0)
        span *= 2
    cum = logf + carry_ref[...]
    carry_ref[...] = cum[tile - 1:tile, :]

    pieces = []
    rest = cum
    for _ in range(F_PIECES):
        piece = rest.astype(BF16)
        pieces.append(piece)
        rest = rest - piece.astype(F32)
    fp_ref[0] = jnp.concatenate(pieces, axis=1)
    kf = onesk_ref[...]
    for j in range(F_PIECES):
        kf = kf + jnp.dot(pieces[j], pk_ref[j], preferred_element_type=F32)

    k = jnp.dot(hb, wk_ref[...], preferred_element_type=F32)
    parts = []
    for p in range(n_pairs):
        parts += [k[:, p * LANES:(p + 1) * LANES], kf[:, p * LANES:(p + 1) * LANES]]
    k_ref[0] = jnp.concatenate(parts, axis=1).astype(BF16)

    vt = lax.dot_general(wvt_ref[...], hb, (((1,), (1,)), ((), ())),
                         preferred_element_type=F32).astype(BF16)
    for p in range(n_pairs):
        vt_ref[0, p, 0:LANES, :] = vt[p * LANES:(p + 1) * LANES, :]
        vt_ref[0, p, LANES:LANES + ONES_ROWS, :] = jnp.ones((ONES_ROWS, tile), BF16)


def _kvf(h3, w_kvf, b_f, pk, ones_k, tile=512):
    bsz, seq, d = h3.shape
    n_h = b_f.shape[0]
    n_pairs = n_h // 2
    wk = w_kvf[:, :d].astype(BF16)
    wvt = w_kvf[:, d:2 * d].T.astype(BF16)
    wf = jnp.pad(w_kvf[:, 2 * d:], ((0, 0), (0, LANES - n_h)))
    bf = jnp.pad(b_f, (0, LANES - n_h)).reshape(1, LANES)
    const2 = lambda i, j: (0, 0)
    return pl.pallas_call(
        functools.partial(_kvf_kernel, tile=tile),
        grid=(bsz, seq // tile),
        in_specs=[pl.BlockSpec((1, tile, d), lambda i, j: (i, j, 0)),
                  pl.BlockSpec((d, d), const2), pl.BlockSpec((d, d), const2),
                  pl.BlockSpec((d, LANES), const2), pl.BlockSpec((1, LANES), const2),
                  pl.BlockSpec(pk.shape, lambda i, j: (0, 0, 0)),
                  pl.BlockSpec(ones_k.shape, const2)],
        out_specs=[pl.BlockSpec((1, tile, 2 * d), lambda i, j: (i, j, 0)),
                   pl.BlockSpec((1, n_pairs, LANES + ONES_ROWS, tile), lambda i, j: (i, 0, 0, j)),
                   pl.BlockSpec((1, tile, F_PIECES * LANES), lambda i, j: (i, j, 0))],
        out_shape=[jax.ShapeDtypeStruct((bsz, seq, 2 * d), BF16),
                   jax.ShapeDtypeStruct((bsz, n_pairs, LANES + ONES_ROWS, seq), BF16),
                   jax.ShapeDtypeStruct((bsz, seq, F_PIECES * LANES), BF16)],
        scratch_shapes=[pltpu.VMEM((1, LANES), F32)],
        compiler_params=_params(("arbitrary", "arbitrary")),
        name="kvf_proj",
    )(h3, wk, wvt, wf, bf, pk, ones_k)


def _qproj_kernel(h_ref, wq_ref, fp_ref, pq_ref, onesq_ref, q_ref, *, scale):
    n_h = q_ref.shape[1] // (2 * LANES)
    q = jnp.dot(h_ref[...].astype(BF16), wq_ref[...], preferred_element_type=F32) * scale
    qf = onesq_ref[...]
    for j in range(F_PIECES):
        qf = qf + jnp.dot(fp_ref[:, j * LANES:(j + 1) * LANES], pq_ref[j],
                          preferred_element_type=F32)
    first = lax.broadcasted_iota(jnp.int32, (1, LANES), 1) < HEAD_DIM
    parts = []
    for hd in range(n_h):
        p, e = divmod(hd, 2)
        qp = q[:, p * LANES:(p + 1) * LANES]
        parts += [jnp.where(first if e == 0 else jnp.logical_not(first), qp, 0.0),
                  qf[:, hd * LANES:(hd + 1) * LANES]]
    q_ref[...] = jnp.concatenate(parts, axis=1).astype(BF16)


def _qproj(h, w_q, fpieces, pq, ones_q, scale, tile=512):
    n, d = h.shape
    n_h = d // HEAD_DIM
    const2 = lambda i: (0, 0)
    return pl.pallas_call(
        functools.partial(_qproj_kernel, scale=scale),
        grid=(n // tile,),
        in_specs=[pl.BlockSpec((tile, d), lambda i: (i, 0)),
                  pl.BlockSpec((d, d), const2),
                  pl.BlockSpec((tile, F_PIECES * LANES), lambda i: (i, 0)),
                  pl.BlockSpec(pq.shape, lambda i: (0, 0, 0)),
                  pl.BlockSpec(ones_q.shape, const2)],
        out_specs=pl.BlockSpec((tile, n_h * 2 * LANES), lambda i: (i, 0)),
        out_shape=jax.ShapeDtypeStruct((n, n_h * 2 * LANES), BF16),
        compiler_params=_params(("arbitrary",)),
        name="q_proj",
    )(h, w_q.astype(BF16), fpieces, pq, ones_q)


def _attn_kernel(q_ref, k_ref, vt_ref, o_ref, sa_ref, sb_ref, *, tq, tk):
    i = pl.program_id(2)
    per_q = tq // tk
    qa = q_ref[0]
    qs = (qa[:, :2 * LANES], qa[:, 2 * LANES:])
    row = lax.broadcasted_iota(jnp.int32, (tk, tq), 0)
    col = lax.broadcasted_iota(jnp.int32, (tk, tq), 1)
    upper = lax.broadcasted_iota(jnp.int32, (LANES, 1), 0) < HEAD_DIM

    def logits(j, s_ref):
        kj = k_ref[0, pl.ds(pl.multiple_of(j * tk, tk), tk), :]
        for hd in range(2):
            s_ref[hd] = lax.dot_general(kj, qs[hd], (((1,), (1,)), ((), ())),
                                        preferred_element_type=F32)

    def softmax_pv(j, s_ref, carry, diag_offset):
        ms, ls, acc = carry
        vtj = vt_ref[0, 0, :, pl.ds(pl.multiple_of(j * tk, tk), tk)]
        new_ms, new_ls, alphas, outs = [], [], [], []
        for hd in range(2):
            sc = s_ref[hd]
            if diag_offset is not None:
                sc = jnp.where(row + diag_offset <= col, sc, NEG_INF)
            m_new = jnp.maximum(ms[hd], jnp.max(sc, axis=0, keepdims=True))
            pt = jnp.exp(sc - m_new).astype(BF16)
            alpha = jnp.exp(ms[hd] - m_new)
            r = jnp.dot(vtj, pt, preferred_element_type=F32)
            new_ms.append(m_new)
            new_ls.append(alpha * ls[hd] + r[LANES:LANES + 1, :])
            alphas.append(alpha)
            outs.append(r[:LANES, :])
        acc = jnp.where(upper, alphas[0], alphas[1]) * acc + jnp.where(upper, outs[0], outs[1])
        return tuple(new_ms), tuple(new_ls), acc

    def two_blocks(t, carry):
        j = t * 2
        logits(j + 1, sb_ref)
        carry = softmax_pv(j, sa_ref, carry, None)
        logits(j + 2, sa_ref)
        return softmax_pv(j + 1, sb_ref, carry, None)

    init = ((jnp.full((1, tq), NEG_INF, F32),) * 2, (jnp.zeros((1, tq), F32),) * 2,
            jnp.zeros((LANES, tq), F32))
    logits(0, sa_ref)
    first_diag = i * per_q
    carry = lax.fori_loop(0, first_diag // 2, two_blocks, init)
    bufs = (sa_ref, sb_ref)
    for u in range(per_q):
        if u + 1 < per_q:
            logits(first_diag + u + 1, bufs[(u + 1) % 2])
        carry = softmax_pv(first_diag + u, bufs[u % 2], carry, u * tk)
    _, ls, acc = carry
    out_t = acc / jnp.where(upper, ls[0], ls[1])
    o_ref[0] = out_t.T.astype(BF16)


def _attention(qaug, kaug, vaug_t, d, tq=512, tk=256):
    bsz, seq, _ = kaug.shape
    n_pairs = vaug_t.shape[1]
    assert (tq // tk) % 2 == 0
    return pl.pallas_call(
        functools.partial(_attn_kernel, tq=tq, tk=tk),
        grid=(bsz, n_pairs, seq // tq),
        in_specs=[pl.BlockSpec((1, tq, 4 * LANES), lambda b, p, i: (b, i, p)),
                  pl.BlockSpec((1, seq, 2 * LANES), lambda b, p, i: (b, 0, p)),
                  pl.BlockSpec((1, 1, LANES + ONES_ROWS, seq), lambda b, p, i: (b, p, 0, 0))],
        out_specs=pl.BlockSpec((1, tq, LANES), lambda b, p, i: (b, i, p)),
        out_shape=jax.ShapeDtypeStruct((bsz, seq, d), BF16),
        scratch_shapes=[pltpu.VMEM((2, tk, tq), F32), pltpu.VMEM((2, tk, tq), F32)],
        compiler_params=_params(("arbitrary", "arbitrary", "arbitrary")),
        name="fox_attention",
    )(qaug, kaug, vaug_t)


def _oproj_ln_kernel(o_ref, h_ref, wo_ref, g_ref, b_ref, out_ref, *, alpha):
    mix = jnp.dot(o_ref[...], wo_ref[...], preferred_element_type=F32)
    out_ref[...] = _layer_norm(alpha * h_ref[...] + mix, g_ref[...], b_ref[...])


def _oproj_ln(o, h, w_o, g, b, alpha, tile=512):
    n, d = h.shape
    vec = pl.BlockSpec((1, d), lambda i: (0, 0))
    tspec = pl.BlockSpec((tile, d), lambda i: (i, 0))
    return pl.pallas_call(
        functools.partial(_oproj_ln_kernel, alpha=alpha),
        grid=(n // tile,),
        in_specs=[tspec, tspec, pl.BlockSpec((d, d), lambda i: (0, 0)), vec, vec],
        out_specs=tspec,
        out_shape=jax.ShapeDtypeStruct((n, d), F32),
        compiler_params=_params(("arbitrary",)),
        name="oproj_ln",
    )(o, h, w_o.astype(BF16), g.reshape(1, d), b.reshape(1, d))


def kernel(x, pool_w, pool_scale, w_q, w_o, w_kvf, b_f, ln_g, ln_b, router_w, router_b,
           w_gate_up, w_down, ws_gate_up, ws_down):
    bsz, seq, d = x.shape
    depth = ln_g.shape[0]
    n_pool = pool_w.shape[0]
    alpha = float((2 * depth) ** 0.25)
    n = bsz * seq
    kaug = vaug_t = fpieces = None
    pk, ones_k, pq, ones_q = _gate_placement(d // HEAD_DIM)
    for l in range(depth):
        if l < n_pool:
            x = _pool_ln(x, pool_w[l], pool_scale[l], ln_g[l, 0], ln_b[l, 0], alpha)
        else:
            if l == n_pool:
                kaug, vaug_t, fpieces = _kvf(x, w_kvf, b_f, pk, ones_k)
            j = l - n_pool
            qaug = _qproj(x.reshape(n, d), w_q[j], fpieces.reshape(n, -1), pq, ones_q,
                          HEAD_DIM ** -0.5)
            o = _attention(qaug.reshape(bsz, seq, -1), kaug, vaug_t, d)
            x = _oproj_ln(o.reshape(n, d), x.reshape(n, d), w_o[j], ln_g[l, 0], ln_b[l, 0],
                          alpha).reshape(bsz, seq, d)
        x = _moe_ln(x.reshape(n, d), router_w[l], router_b[l], w_gate_up, w_down, l,
                    ws_gate_up[l], ws_down[l], ln_g[l, 1], ln_b[l, 1], alpha).reshape(bsz, seq, d)
    return x
```

```python
import functools

import jax
import jax.numpy as jnp
from jax import lax
from jax.experimental import pallas as pl
from jax.experimental.pallas import tpu as pltpu

POOL_WINDOWS = (2, 4, 8, 16)
POOL_HALO = 16
HEAD_DIM = 64
N_EXPERT_GROUPS = 8
TOPK_GROUPS = 4
TOP_K = 8
ROUTED_SCALE = 2.5
MOE_BLOCK = 256
DMA_UNROLL = 8
DMA_QUEUES = 2
DEST_TILE = 256
LN_EPS = 1e-5

LANES = 128
SUBLANES = 8
VMEM_LIMIT = 48 * 1024 * 1024

F32 = jnp.float32
BF16 = jnp.bfloat16
NEG_INF = float("-inf")


def _params(semantics):
    return pltpu.CompilerParams(dimension_semantics=semantics, vmem_limit_bytes=VMEM_LIMIT)


def _layer_norm(z, g, b):
    mu = jnp.mean(z, axis=-1, keepdims=True)
    zc = z - mu
    var = jnp.mean(zc * zc, axis=-1, keepdims=True)
    return zc * lax.rsqrt(var + LN_EPS) * g + b


def _silu(x):
    return x * jax.nn.sigmoid(x)


def _pool_ln_kernel(x_ref, w_ref, scale_ref, g_ref, b_ref, o_ref, ext_ref, *, alpha, tile):
    s = pl.program_id(1)
    d = x_ref.shape[2]
    c = d // len(POOL_WINDOWS)

    @pl.when(s == 0)
    def _():
        ext_ref[0:POOL_HALO, :] = jnp.zeros((POOL_HALO, d), F32)

    x = x_ref[0]
    ext_ref[POOL_HALO:POOL_HALO + tile, :] = x
    pos = (s * tile + 1 + lax.broadcasted_iota(jnp.int32, (tile, 1), 0)).astype(F32)
    ys = []
    for gi, w in enumerate(POOL_WINDOWS):
        cs = slice(gi * c, (gi + 1) * c)
        acc = ext_ref[:, cs]
        span = 1
        while span < w:
            acc = acc + pltpu.roll(acc, span, axis=0)
            span *= 2
        mean = acc[POOL_HALO:, :] / jnp.minimum(pos, float(w))
        mixed = mean - x[:, cs]
        ys.append(jnp.dot(mixed.astype(BF16), w_ref[gi], preferred_element_type=F32))
    y = jnp.concatenate(ys, axis=1) * scale_ref[...]
    o_ref[0] = _layer_norm(alpha * x + y, g_ref[...], b_ref[...])
    ext_ref[0:POOL_HALO, :] = x[tile - POOL_HALO:, :]


def _pool_ln(x, w, scale, g, b, alpha, tile=512):
    bsz, seq, d = x.shape
    ng, c, _ = w.shape
    vec = pl.BlockSpec((1, d), lambda i, j: (0, 0))
    return pl.pallas_call(
        functools.partial(_pool_ln_kernel, alpha=alpha, tile=tile),
        grid=(bsz, seq // tile),
        in_specs=[pl.BlockSpec((1, tile, d), lambda i, j: (i, j, 0)),
                  pl.BlockSpec((ng, c, c), lambda i, j: (0, 0, 0)),
                  vec, vec, vec],
        out_specs=pl.BlockSpec((1, tile, d), lambda i, j: (i, j, 0)),
        out_shape=jax.ShapeDtypeStruct(x.shape, F32),
        scratch_shapes=[pltpu.VMEM((POOL_HALO + tile, d), F32)],
        compiler_params=_params(("arbitrary", "arbitrary")),
        name="pool_ln",
    )(x, w.astype(BF16), scale.reshape(1, d), g.reshape(1, d), b.reshape(1, d))


def _first_argmax(v, iota, size):
    m = jnp.max(v, axis=0, keepdims=True)
    idx = jnp.min(jnp.where(v == m, iota, size), axis=0, keepdims=True)
    return m, idx


def _router_kernel(h_ref, wt_ref, b_ref, eidx_ref, gw_ref, rank_ref, cnt_ref, carry_ref, *, tile):
    i = pl.program_id(0)
    n_e = wt_ref.shape[0]
    per_group = n_e // N_EXPERT_GROUPS

    @pl.when(i == 0)
    def _():
        carry_ref[...] = jnp.zeros_like(carry_ref)

    h = h_ref[...]
    h_hi = h.astype(BF16)
    h_lo = (h - h_hi.astype(F32)).astype(BF16)
    wt = wt_ref[...]
    wt_hi = wt.astype(BF16)
    wt_lo = (wt - wt_hi.astype(F32)).astype(BF16)
    nt_dims = (((1,), (1,)), ((), ()))
    logits = (lax.dot_general(wt_hi, h_hi, nt_dims, preferred_element_type=F32)
              + (lax.dot_general(wt_lo, h_hi, nt_dims, preferred_element_type=F32)
                 + lax.dot_general(wt_hi, h_lo, nt_dims, preferred_element_type=F32)))
    s = jax.nn.sigmoid(logits)
    sb = s + b_ref[...]

    iota_g = lax.broadcasted_iota(jnp.int32, (per_group, tile), 0)
    gscores = []
    for g in range(N_EXPERT_GROUPS):
        blk = sb[g * per_group:(g + 1) * per_group, :]
        m1, i1 = _first_argmax(blk, iota_g, per_group)
        m2 = jnp.max(jnp.where(iota_g == i1, NEG_INF, blk), axis=0, keepdims=True)
        gscores.append(m1 + m2)
    gs = jnp.concatenate(gscores, axis=0)
    iota_8 = lax.broadcasted_iota(jnp.int32, (N_EXPERT_GROUPS, tile), 0)
    gsel = jnp.zeros((N_EXPERT_GROUPS, tile), jnp.bool_)
    for _ in range(TOPK_GROUPS):
        _, gi = _first_argmax(gs, iota_8, N_EXPERT_GROUPS)
        hit = iota_8 == gi
        gsel = jnp.logical_or(gsel, hit)
        gs = jnp.where(hit, NEG_INF, gs)
    cand = jnp.concatenate(
        [jnp.where(gsel[g:g + 1, :], sb[g * per_group:(g + 1) * per_group, :], NEG_INF)
         for g in range(N_EXPERT_GROUPS)], axis=0)

    iota_e = lax.broadcasted_iota(jnp.int32, (n_e, tile), 0)
    member = jnp.zeros((n_e, tile), F32)
    idxs, svals = [], []
    for _ in range(TOP_K):
        _, ei = _first_argmax(cand, iota_e, n_e)
        hit = iota_e == ei
        idxs.append(ei)
        svals.append(jnp.sum(jnp.where(hit, s, 0.0), axis=0, keepdims=True))
        member = jnp.where(hit, 1.0, member)
        cand = jnp.where(hit, NEG_INF, cand)
    sv = jnp.concatenate(svals, axis=0)
    gw_ref[...] = sv / jnp.sum(sv, axis=0, keepdims=True) * ROUTED_SCALE
    eidx_ref[...] = jnp.concatenate(idxs, axis=0)

    r = lax.broadcasted_iota(jnp.int32, (tile, tile), 0)
    c = lax.broadcasted_iota(jnp.int32, (tile, tile), 1)
    earlier = (r < c).astype(BF16)
    before = jnp.dot(member.astype(BF16), earlier, preferred_element_type=F32) + carry_ref[...]
    ranks = [jnp.sum(jnp.where(iota_e == ei, before, 0.0), axis=0, keepdims=True) for ei in idxs]
    rank_ref[...] = jnp.concatenate(ranks, axis=0).astype(jnp.int32)
    carry_ref[...] = carry_ref[...] + jnp.sum(member, axis=1, keepdims=True)
    cnt_ref[...] = carry_ref[...].astype(jnp.int32)


def _router(h, w_router, b_router, tile=512):
    n, d = h.shape
    n_e = w_router.shape[1]
    out_kt = pl.BlockSpec((TOP_K, tile), lambda i: (0, i))
    return pl.pallas_call(
        functools.partial(_router_kernel, tile=tile),
        grid=(n // tile,),
        in_specs=[pl.BlockSpec((tile, d), lambda i: (i, 0)),
                  pl.BlockSpec((n_e, d), lambda i: (0, 0)),
                  pl.BlockSpec((n_e, 1), lambda i: (0, 0))],
        out_specs=[out_kt, out_kt, out_kt, pl.BlockSpec((n_e, 1), lambda i: (0, 0))],
        out_shape=[jax.ShapeDtypeStruct((TOP_K, n), jnp.int32),
                   jax.ShapeDtypeStruct((TOP_K, n), F32),
                   jax.ShapeDtypeStruct((TOP_K, n), jnp.int32),
                   jax.ShapeDtypeStruct((n_e, 1), jnp.int32)],
        scratch_shapes=[pltpu.VMEM((n_e, 1), F32)],
        compiler_params=_params(("arbitrary",)),
        name="router",
    )(h, w_router.T, b_router.reshape(n_e, 1))


def _dest_kernel(eidx_ref, rank_ref, pstart_ref, dest_ref):
    n_e = pstart_ref.shape[0]
    tile = eidx_ref.shape[1]
    iota_e = lax.broadcasted_iota(jnp.int32, (n_e, tile), 0)
    starts = pstart_ref[...]
    rows = [jnp.sum(jnp.where(iota_e == eidx_ref[k:k + 1, :], starts, 0), axis=0, keepdims=True)
            for k in range(TOP_K)]
    dest_ref[0] = jnp.concatenate(rows, axis=0) + rank_ref[...]


def _dest_rows(eidx, rank, pstart):
    n = eidx.shape[1]
    n_e = pstart.shape[0]
    kt = pl.BlockSpec((TOP_K, DEST_TILE), lambda i: (0, i))
    return pl.pallas_call(
        _dest_kernel,
        grid=(n // DEST_TILE,),
        in_specs=[kt, kt, pl.BlockSpec((n_e, 1), lambda i: (0, 0))],
        out_specs=pl.BlockSpec((1, TOP_K, DEST_TILE), lambda i: (i, 0, 0)),
        out_shape=jax.ShapeDtypeStruct((n // DEST_TILE, TOP_K, DEST_TILE), jnp.int32),
        compiler_params=_params(("arbitrary",)),
        name="dest_rows",
    )(eidx, rank, pstart.reshape(n_e, 1))


def _dispatch_kernel(dest_ref, h_ref, xg_hbm, sem):
    tiles = dest_ref.shape[0]
    for sub in range(tiles):
        def group(g, carry, sub=sub):
            off = pl.multiple_of(g * DMA_UNROLL, DMA_UNROLL)
            for u in range(DMA_UNROLL):
                for k in range(TOP_K):
                    pltpu.make_async_copy(h_ref.at[pl.ds(sub * DEST_TILE + off + u, 1)],
                                          xg_hbm.at[pl.ds(dest_ref[sub, k, off + u], 1)],
                                          sem).start(priority=k % DMA_QUEUES)
            return carry
        lax.fori_loop(0, DEST_TILE // DMA_UNROLL, group, 0)
    for _ in range(tiles * TOP_K):
        pltpu.make_async_copy(xg_hbm.at[pl.ds(0, DEST_TILE)], xg_hbm.at[pl.ds(0, DEST_TILE)],
                              sem).wait()


def _dispatch(dest, h, n_rows, tiles=4):
    n, d = h.shape
    rows = tiles * DEST_TILE
    return pl.pallas_call(
        _dispatch_kernel,
        grid=(n // rows,),
        in_specs=[pl.BlockSpec((tiles, TOP_K, DEST_TILE), lambda i: (i, 0, 0),
                               memory_space=pltpu.SMEM),
                  pl.BlockSpec((rows, d), lambda i: (i, 0))],
        out_specs=pl.BlockSpec(memory_space=pl.ANY),
        out_shape=jax.ShapeDtypeStruct((n_rows, d), F32),
        scratch_shapes=[pltpu.SemaphoreType.DMA(())],
        compiler_params=_params(("arbitrary",)),
        name="dispatch_rows",
    )(dest, h)


def _grouped_kernel(ps_ref, cnt_ref, xg_hbm, wgu_ref, wd_ref, y_hbm, xfirst, xbuf, ybuf, wgu_bf,
                    wd_bf, fsem, isem, osem, done_ref, osize_ref):
    e = pl.program_id(0)
    last = pl.num_programs(0) - 1
    n_rows = cnt_ref[e]
    base = ps_ref[e]
    n_chunks = (n_rows + MOE_BLOCK - 1) // MOE_BLOCK
    f = wd_ref.shape[2]
    row = lax.broadcasted_iota(jnp.int32, (MOE_BLOCK, 1), 0)

    def dma_rows(count, c):
        valid = jnp.minimum(count - c * MOE_BLOCK, MOE_BLOCK)
        return pl.multiple_of((valid + SUBLANES - 1) // SUBLANES * SUBLANES, SUBLANES)

    def first_copy(ex):
        size = dma_rows(cnt_ref[ex], 0)
        start = pl.multiple_of(ps_ref[ex], MOE_BLOCK)
        slot = lax.rem(ex, 2)
        return pltpu.make_async_copy(xg_hbm.at[pl.ds(start, size)],
                                     xfirst.at[slot, pl.ds(0, size)], fsem.at[slot])

    def in_copy(c, slot):
        size = dma_rows(n_rows, c)
        start = pl.multiple_of(base + c * MOE_BLOCK, MOE_BLOCK)
        return pltpu.make_async_copy(xg_hbm.at[pl.ds(start, size)],
                                     xbuf.at[slot, pl.ds(0, size)], isem.at[slot])

    def out_copy(start, size, slot):
        return pltpu.make_async_copy(ybuf.at[slot, pl.ds(0, size)],
                                     y_hbm.at[pl.ds(start, size)], osem.at[slot])

    def process(x32, c):
        x = jnp.where(row < n_rows - c * MOE_BLOCK, x32, 0.0).astype(BF16)
        gu = jnp.dot(x, wgu_bf[...], preferred_element_type=F32)
        act = _silu(gu[:, :f]) * gu[:, f:]
        y = jnp.dot(act.astype(BF16), wd_bf[...], preferred_element_type=F32)
        done = done_ref[0]
        slot = lax.rem(done, 2)

        @pl.when(done >= 2)
        def _():
            out_copy(0, pl.multiple_of(osize_ref[slot], SUBLANES), slot).wait()

        ybuf[slot] = y
        size = dma_rows(n_rows, c)
        osize_ref[slot] = size
        out_copy(pl.multiple_of(base + c * MOE_BLOCK, MOE_BLOCK), size, slot).start()
        done_ref[0] = done + 1

    @pl.when(e == 0)
    def _():
        done_ref[0] = 0
        xfirst[...] = jnp.zeros_like(xfirst)
        xbuf[...] = jnp.zeros_like(xbuf)

        @pl.when(n_rows > 0)
        def _():
            first_copy(0).start()

    nxt = jnp.minimum(e + 1, last)

    @pl.when(jnp.logical_and(e < last, cnt_ref[nxt] > 0))
    def _():
        first_copy(nxt).start()

    @pl.when(n_chunks > 0)
    def _():
        wgu_bf[...] = wgu_ref[0, 0].astype(BF16)
        wd_bf[...] = wd_ref[0, 0].astype(BF16)

        @pl.when(n_chunks > 1)
        def _():
            in_copy(1, 1).start()

        first_copy(e).wait()
        process(xfirst[lax.rem(e, 2)], 0)

        def chunk(c, carry):
            slot = lax.rem(c, 2)

            @pl.when(c + 1 < n_chunks)
            def _():
                in_copy(c + 1, 1 - slot).start()

            in_copy(c, slot).wait()
            process(xbuf[slot], c)
            return carry

        lax.fori_loop(1, n_chunks, chunk, 0)

    @pl.when(e == last)
    def _():
        for back in (1, 2):
            @pl.when(done_ref[0] >= back)
            def _():
                slot = lax.rem(done_ref[0] - back, 2)
                out_copy(0, pl.multiple_of(osize_ref[slot], SUBLANES), slot).wait()


def _grouped_experts(xg, pstart, counts, w_gate_up, w_down, layer):
    p, d = xg.shape
    _, n_e, _, f2 = w_gate_up.shape
    f = f2 // 2
    grid_spec = pltpu.PrefetchScalarGridSpec(
        num_scalar_prefetch=2,
        grid=(n_e,),
        in_specs=[
            pl.BlockSpec(memory_space=pl.ANY),
            pl.BlockSpec((1, 1, d, f2), lambda e, ps, cnt: (layer, e, 0, 0)),
            pl.BlockSpec((1, 1, f, d), lambda e, ps, cnt: (layer, e, 0, 0)),
        ],
        out_specs=pl.BlockSpec(memory_space=pl.ANY),
        scratch_shapes=[pltpu.VMEM((2, MOE_BLOCK, d), F32), pltpu.VMEM((2, MOE_BLOCK, d), F32),
                        pltpu.VMEM((2, MOE_BLOCK, d), F32),
                        pltpu.VMEM((d, f2), BF16), pltpu.VMEM((f, d), BF16),
                        pltpu.SemaphoreType.DMA((2,)), pltpu.SemaphoreType.DMA((2,)),
                        pltpu.SemaphoreType.DMA((2,)),
                        pltpu.SMEM((1,), jnp.int32), pltpu.SMEM((2,), jnp.int32)],
    )
    return pl.pallas_call(
        _grouped_kernel,
        grid_spec=grid_spec,
        out_shape=jax.ShapeDtypeStruct((p, d), F32),
        compiler_params=_params(("arbitrary",)),
        name="grouped_experts",
    )(pstart, counts, xg, w_gate_up, w_down)


def _combine_ln_kernel(dpair_ref, dnext_ref, h_ref, gw_ref, y_hbm, wsgu_ref, wsd_ref, g_ref, b_ref,
                       o_ref, ybuf, sem, *, alpha):
    i = pl.program_id(0)
    n_steps = pl.num_programs(0)
    f = wsd_ref.shape[0]

    def start_gather(dref, sub, s):
        def group(g, carry):
            off = pl.multiple_of(g * DMA_UNROLL, DMA_UNROLL)
            for u in range(DMA_UNROLL):
                for k in range(TOP_K):
                    pltpu.make_async_copy(y_hbm.at[pl.ds(dref[sub, k, off + u], 1)],
                                          ybuf.at[s, k, pl.ds(off + u, 1)],
                                          sem.at[s]).start(priority=k % DMA_QUEUES)
            return carry
        lax.fori_loop(0, DEST_TILE // DMA_UNROLL, group, 0)

    def finish(s):
        rows = slice(s * DEST_TILE, (s + 1) * DEST_TILE)
        h = h_ref[rows, :]
        gu = jnp.dot(h.astype(BF16), wsgu_ref[...], preferred_element_type=F32)
        act = _silu(gu[:, :f]) * gu[:, f:]
        shared = jnp.dot(act.astype(BF16), wsd_ref[...], preferred_element_type=F32)
        for k in range(TOP_K):
            pltpu.make_async_copy(y_hbm.at[pl.ds(0, DEST_TILE)], ybuf.at[s, k], sem.at[s]).wait()
        gw = gw_ref[rows, :]
        routed = gw[:, 0:1] * ybuf[s, 0]
        for k in range(1, TOP_K):
            routed = routed + gw[:, k:k + 1] * ybuf[s, k]
        o_ref[rows, :] = _layer_norm(alpha * h + (routed + shared), g_ref[...], b_ref[...])

    @pl.when(i == 0)
    def _():
        start_gather(dpair_ref, 0, 0)

    start_gather(dpair_ref, 1, 1)
    finish(0)

    @pl.when(i + 1 < n_steps)
    def _():
        start_gather(dnext_ref, 0, 0)

    finish(1)


def _combine_ln(dest, h, gw, y_rows, ws_gate_up, ws_down, g, b, alpha):
    n, d = h.shape
    f2 = ws_gate_up.shape[1]
    nt = n // DEST_TILE
    rows = 2 * DEST_TILE
    vec = pl.BlockSpec((1, d), lambda i: (0, 0))
    return pl.pallas_call(
        functools.partial(_combine_ln_kernel, alpha=alpha),
        grid=(nt // 2,),
        in_specs=[pl.BlockSpec((2, TOP_K, DEST_TILE), lambda i: (i, 0, 0),
                               memory_space=pltpu.SMEM),
                  pl.BlockSpec((1, TOP_K, DEST_TILE),
                               lambda i: (jnp.minimum(2 * i + 2, nt - 1), 0, 0),
                               memory_space=pltpu.SMEM),
                  pl.BlockSpec((rows, d), lambda i: (i, 0)),
                  pl.BlockSpec((rows, TOP_K), lambda i: (i, 0)),
                  pl.BlockSpec(memory_space=pl.ANY),
                  pl.BlockSpec((d, f2), lambda i: (0, 0)),
                  pl.BlockSpec((f2 // 2, d), lambda i: (0, 0)), vec, vec],
        out_specs=pl.BlockSpec((rows, d), lambda i: (i, 0)),
        out_shape=jax.ShapeDtypeStruct((n, d), F32),
        scratch_shapes=[pltpu.VMEM((2, TOP_K, DEST_TILE, d), F32),
                        pltpu.SemaphoreType.DMA((2,))],
        compiler_params=_params(("arbitrary",)),
        name="combine_ln",
    )(dest, dest, h, gw, y_rows, ws_gate_up.astype(BF16), ws_down.astype(BF16),
      g.reshape(1, d), b.reshape(1, d))


def _moe_ln(h, w_router, b_router, w_gate_up, w_down, layer, ws_gate_up, ws_down, g, b, alpha):
    n, d = h.shape
    n_e = w_router.shape[1]
    eidx, gw, rank, counts = _router(h, w_router, b_router)

    nb = -(-(n * TOP_K + n_e * (MOE_BLOCK - 1)) // MOE_BLOCK)
    counts = counts[:, 0]
    padded = (counts + MOE_BLOCK - 1) // MOE_BLOCK * MOE_BLOCK
    pstart = (jnp.cumsum(padded) - padded).astype(jnp.int32)

    dest = _dest_rows(eidx, rank, pstart)
    xg = _dispatch(dest, h, nb * MOE_BLOCK)
    y_rows = _grouped_experts(xg, pstart, counts, w_gate_up, w_down, layer)
    return _combine_ln(dest, h, gw.T, y_rows, ws_gate_up, ws_down, g, b, alpha)


F_PIECES = 3
ONES_ROWS = 16


def _gate_placement(n_heads):
    import numpy as np
    n_pairs = n_heads // 2
    pk = np.zeros((F_PIECES, LANES, n_pairs * LANES), np.float32)
    ones_k = np.zeros((1, n_pairs * LANES), np.float32)
    pq = np.zeros((F_PIECES, LANES, n_pairs * LANES), np.float32)
    ones_q = np.zeros((1, n_pairs * LANES), np.float32)
    for h in range(n_heads):
        p, e = divmod(h, 2)
        for j in range(F_PIECES):
            pk[j, h, p * LANES + F_PIECES * e + j] = -1.0
            ones_q[0, p * LANES + F_PIECES * e + j] = 1.0
            pq[j, h, p * LANES + F_PIECES * (2 + e) + j] = 1.0
            ones_k[0, p * LANES + F_PIECES * (2 + e) + j] = 1.0
    return (jnp.asarray(pk, BF16), jnp.asarray(ones_k), jnp.asarray(pq, BF16), jnp.asarray(ones_q))


def _kvf_kernel(h_ref, wk_ref, wvt_ref, wf_ref, bf_ref, pk_ref, onesk_ref, k_ref, vt_ref, fp_ref,
                carry_ref, *, tile):
    s = pl.program_id(1)
    n_pairs = vt_ref.shape[1]

    @pl.when(s == 0)
    def _():
        carry_ref[...] = jnp.zeros_like(carry_ref)

    h = h_ref[0]
    hb = h.astype(BF16)
    h_lo = (h - hb.astype(F32)).astype(BF16)
    wf = wf_ref[...]
    wf_hi = wf.astype(BF16)
    wf_lo = (wf - wf_hi.astype(F32)).astype(BF16)
    z = (jnp.dot(hb, wf_hi, preferred_element_type=F32)
         + (jnp.dot(hb, wf_lo, preferred_element_type=F32)
            + jnp.dot(h_lo, wf_hi, preferred_element_type=F32))) + bf_ref[...]
    logf = jnp.minimum(z, 0.0) - jnp.log1p(jnp.exp(-jnp.abs(z)))
    row = lax.broadcasted_iota(jnp.int32, (tile, 1), 0)
    span = 1
    while span < tile:
        logf = logf + jnp.where(row >= span, pltpu.roll(logf, span, axis=0), 0.0)
        span *= 2
    cum = logf + carry_ref[...]
    carry_ref[...] = cum[tile - 1:tile, :]

    pieces = []
    rest = cum
    for _ in range(F_PIECES):
        piece = rest.astype(BF16)
        pieces.append(piece)
        rest = rest - piece.astype(F32)
    fp_ref[0] = jnp.concatenate(pieces, axis=1)
    kf = onesk_ref[...]
    for j in range(F_PIECES):
        kf = kf + jnp.dot(pieces[j], pk_ref[j], preferred_element_type=F32)

    k = jnp.dot(hb, wk_ref[...], preferred_element_type=F32)
    parts = []
    for p in range(n_pairs):
        parts += [k[:, p * LANES:(p + 1) * LANES], kf[:, p * LANES:(p + 1) * LANES]]
    k_ref[0] = jnp.concatenate(parts, axis=1).astype(BF16)

    vt = lax.dot_general(wvt_ref[...], hb, (((1,), (1,)), ((), ())),
                         preferred_element_type=F32).astype(BF16)
    for p in range(n_pairs):
        vt_ref[0, p, 0:LANES, :] = vt[p * LANES:(p + 1) * LANES, :]
        vt_ref[0, p, LANES:LANES + ONES_ROWS, :] = jnp.ones((ONES_ROWS, tile), BF16)


def _kvf(h3, w_kvf, b_f, pk, ones_k, tile=512):
    bsz, seq, d = h3.shape
    n_h = b_f.shape[0]
    n_pairs = n_h // 2
    wk = w_kvf[:, :d].astype(BF16)
    wvt = w_kvf[:, d:2 * d].T.astype(BF16)
    wf = jnp.pad(w_kvf[:, 2 * d:], ((0, 0), (0, LANES - n_h)))
    bf = jnp.pad(b_f, (0, LANES - n_h)).reshape(1, LANES)
    const2 = lambda i, j: (0, 0)
    return pl.pallas_call(
        functools.partial(_kvf_kernel, tile=tile),
        grid=(bsz, seq // tile),
        in_specs=[pl.BlockSpec((1, tile, d), lambda i, j: (i, j, 0)),
                  pl.BlockSpec((d, d), const2), pl.BlockSpec((d, d), const2),
                  pl.BlockSpec((d, LANES), const2), pl.BlockSpec((1, LANES), const2),
                  pl.BlockSpec(pk.shape, lambda i, j: (0, 0, 0)),
                  pl.BlockSpec(ones_k.shape, const2)],
        out_specs=[pl.BlockSpec((1, tile, 2 * d), lambda i, j: (i, j, 0)),
                   pl.BlockSpec((1, n_pairs, LANES + ONES_ROWS, tile), lambda i, j: (i, 0, 0, j)),
                   pl.BlockSpec((1, tile, F_PIECES * LANES), lambda i, j: (i, j, 0))],
        out_shape=[jax.ShapeDtypeStruct((bsz, seq, 2 * d), BF16),
                   jax.ShapeDtypeStruct((bsz, n_pairs, LANES + ONES_ROWS, seq), BF16),
                   jax.ShapeDtypeStruct((bsz, seq, F_PIECES * LANES), BF16)],
        scratch_shapes=[pltpu.VMEM((1, LANES), F32)],
        compiler_params=_params(("arbitrary", "arbitrary")),
        name="kvf_proj",
    )(h3, wk, wvt, wf, bf, pk, ones_k)


def _qproj_kernel(h_ref, wq_ref, fp_ref, pq_ref, onesq_ref, q_ref, *, scale):
    n_pairs = q_ref.shape[1] // (2 * LANES)
    q = jnp.dot(h_ref[...].astype(BF16), wq_ref[...], preferred_element_type=F32) * scale
    qf = onesq_ref[...]
    for j in range(F_PIECES):
        qf = qf + jnp.dot(fp_ref[:, j * LANES:(j + 1) * LANES], pq_ref[j],
                          preferred_element_type=F32)
    parts = []
    for p in range(n_pairs):
        parts += [q[:, p * LANES:(p + 1) * LANES], qf[:, p * LANES:(p + 1) * LANES]]
    q_ref[...] = jnp.concatenate(parts, axis=1).astype(BF16)


def _qproj(h, w_q, fpieces, pq, ones_q, scale, tile=512):
    n, d = h.shape
    n_h = d // HEAD_DIM
    const2 = lambda i: (0, 0)
    return pl.pallas_call(
        functools.partial(_qproj_kernel, scale=scale),
        grid=(n // tile,),
        in_specs=[pl.BlockSpec((tile, d), lambda i: (i, 0)),
                  pl.BlockSpec((d, d), const2),
                  pl.BlockSpec((tile, F_PIECES * LANES), lambda i: (i, 0)),
                  pl.BlockSpec(pq.shape, lambda i: (0, 0, 0)),
                  pl.BlockSpec(ones_q.shape, const2)],
        out_specs=pl.BlockSpec((tile, n_h * LANES), lambda i: (i, 0)),
        out_shape=jax.ShapeDtypeStruct((n, n_h * LANES), BF16),
        compiler_params=_params(("arbitrary",)),
        name="q_proj",
    )(h, w_q.astype(BF16), fpieces, pq, ones_q)


def _attn_kernel(q_ref, k_ref, vt_ref, o_ref, sa_ref, sb_ref, *, tq, tk):
    i = pl.program_id(2)
    per_q = tq // tk
    qa = q_ref[0]
    lane = lax.broadcasted_iota(jnp.int32, (1, 2 * LANES), 1)

    def lanes_in(start, size):
        return jnp.logical_and(lane >= start, lane < start + size)

    qs = []
    for hd in range(2):
        own = jnp.logical_or(
            lanes_in(hd * HEAD_DIM, HEAD_DIM),
            jnp.logical_or(lanes_in(LANES + hd * F_PIECES, F_PIECES),
                           lanes_in(LANES + (2 + hd) * F_PIECES, F_PIECES)))
        qs.append(jnp.where(own, qa, jnp.zeros_like(qa)))
    row = lax.broadcasted_iota(jnp.int32, (tk, tq), 0)
    col = lax.broadcasted_iota(jnp.int32, (tk, tq), 1)
    upper = lax.broadcasted_iota(jnp.int32, (LANES, 1), 0) < HEAD_DIM

    def logits(j, s_ref):
        kj = k_ref[0, pl.ds(pl.multiple_of(j * tk, tk), tk), :]
        for hd in range(2):
            s_ref[hd] = lax.dot_general(kj, qs[hd], (((1,), (1,)), ((), ())),
                                        preferred_element_type=F32)

    def softmax_pv(j, s_ref, carry, diag_offset):
        ms, ls, acc = carry
        vtj = vt_ref[0, 0, :, pl.ds(pl.multiple_of(j * tk, tk), tk)]
        new_ms, new_ls, alphas, outs = [], [], [], []
        for hd in range(2):
            sc = s_ref[hd]
            if diag_offset is not None:
                sc = jnp.where(row + diag_offset <= col, sc, NEG_INF)
            m_new = jnp.maximum(ms[hd], jnp.max(sc, axis=0, keepdims=True))
            pt = jnp.exp(sc - m_new).astype(BF16)
            alpha = jnp.exp(ms[hd] - m_new)
            r = jnp.dot(vtj, pt, preferred_element_type=F32)
            new_ms.append(m_new)
            new_ls.append(alpha * ls[hd] + r[LANES:LANES + 1, :])
            alphas.append(alpha)
            outs.append(r[:LANES, :])
        acc = jnp.where(upper, alphas[0], alphas[1]) * acc + jnp.where(upper, outs[0], outs[1])
        return tuple(new_ms), tuple(new_ls), acc

    def two_blocks(t, carry):
        j = t * 2
        logits(j + 1, sb_ref)
        carry = softmax_pv(j, sa_ref, carry, None)
        logits(j + 2, sa_ref)
        return softmax_pv(j + 1, sb_ref, carry, None)

    init = ((jnp.full((1, tq), NEG_INF, F32),) * 2, (jnp.zeros((1, tq), F32),) * 2,
            jnp.zeros((LANES, tq), F32))
    logits(0, sa_ref)
    first_diag = i * per_q
    carry = lax.fori_loop(0, first_diag // 2, two_blocks, init)
    bufs = (sa_ref, sb_ref)
    for u in range(per_q):
        if u + 1 < per_q:
            logits(first_diag + u + 1, bufs[(u + 1) % 2])
        carry = softmax_pv(first_diag + u, bufs[u % 2], carry, u * tk)
    _, ls, acc = carry
    out_t = acc / jnp.where(upper, ls[0], ls[1])
    o_ref[0] = out_t.T.astype(BF16)


def _attention(qaug, kaug, vaug_t, d, tq=512, tk=256):
    bsz, seq, _ = kaug.shape
    n_pairs = vaug_t.shape[1]
    assert (tq // tk) % 2 == 0
    return pl.pallas_call(
        functools.partial(_attn_kernel, tq=tq, tk=tk),
        grid=(bsz, n_pairs, seq // tq),
        in_specs=[pl.BlockSpec((1, tq, 2 * LANES), lambda b, p, i: (b, i, p)),
                  pl.BlockSpec((1, seq, 2 * LANES), lambda b, p, i: (b, 0, p)),
                  pl.BlockSpec((1, 1, LANES + ONES_ROWS, seq), lambda b, p, i: (b, p, 0, 0))],
        out_specs=pl.BlockSpec((1, tq, LANES), lambda b, p, i: (b, i, p)),
        out_shape=jax.ShapeDtypeStruct((bsz, seq, d), BF16),
        scratch_shapes=[pltpu.VMEM((2, tk, tq), F32), pltpu.VMEM((2, tk, tq), F32)],
        compiler_params=_params(("arbitrary", "arbitrary", "arbitrary")),
        name="fox_attention",
    )(qaug, kaug, vaug_t)


def _oproj_ln_kernel(o_ref, h_ref, wo_ref, g_ref, b_ref, out_ref, *, alpha):
    mix = jnp.dot(o_ref[...], wo_ref[...], preferred_element_type=F32)
    out_ref[...] = _layer_norm(alpha * h_ref[...] + mix, g_ref[...], b_ref[...])


def _oproj_ln(o, h, w_o, g, b, alpha, tile=512):
    n, d = h.shape
    vec = pl.BlockSpec((1, d), lambda i: (0, 0))
    tspec = pl.BlockSpec((tile, d), lambda i: (i, 0))
    return pl.pallas_call(
        functools.partial(_oproj_ln_kernel, alpha=alpha),
        grid=(n // tile,),
        in_specs=[tspec, tspec, pl.BlockSpec((d, d), lambda i: (0, 0)), vec, vec],
        out_specs=tspec,
        out_shape=jax.ShapeDtypeStruct((n, d), F32),
        compiler_params=_params(("arbitrary",)),
        name="oproj_ln",
    )(o, h, w_o.astype(BF16), g.reshape(1, d), b.reshape(1, d))


def kernel(x, pool_w, pool_scale, w_q, w_o, w_kvf, b_f, ln_g, ln_b, router_w, router_b,
           w_gate_up, w_down, ws_gate_up, ws_down):
    bsz, seq, d = x.shape
    depth = ln_g.shape[0]
    n_pool = pool_w.shape[0]
    alpha = float((2 * depth) ** 0.25)
    n = bsz * seq
    kaug = vaug_t = fpieces = None
    pk, ones_k, pq, ones_q = _gate_placement(d // HEAD_DIM)
    for l in range(depth):
        if l < n_pool:
            x = _pool_ln(x, pool_w[l], pool_scale[l], ln_g[l, 0], ln_b[l, 0], alpha)
        else:
            if l == n_pool:
                kaug, vaug_t, fpieces = _kvf(x, w_kvf, b_f, pk, ones_k)
            j = l - n_pool
            qaug = _qproj(x.reshape(n, d), w_q[j], fpieces.reshape(n, -1), pq, ones_q,
                          HEAD_DIM ** -0.5)
            o = _attention(qaug.reshape(bsz, seq, -1), kaug, vaug_t, d)
            x = _oproj_ln(o.reshape(n, d), x.reshape(n, d), w_o[j], ln_g[l, 0], ln_b[l, 0],
                          alpha).reshape(bsz, seq, d)
        x = _moe_ln(x.reshape(n, d), router_w[l], router_b[l], w_gate_up, w_down, l,
                    ws_gate_up[l], ws_down[l], ln_g[l, 1], ln_b[l, 1], alpha).reshape(bsz, seq, d)
    return x
```

```python
import functools

import jax
import jax.numpy as jnp
from jax import lax
from jax.experimental import pallas as pl
from jax.experimental.pallas import tpu as pltpu

POOL_WINDOWS = (2, 4, 8, 16)
POOL_HALO = 16
HEAD_DIM = 64
N_EXPERT_GROUPS = 8
TOPK_GROUPS = 4
TOP_K = 8
ROUTED_SCALE = 2.5
MOE_BLOCK = 256
DMA_UNROLL = 8
DMA_QUEUES = 2
DEST_TILE = 256
LN_EPS = 1e-5

LANES = 128
SUBLANES = 8
VMEM_LIMIT = 48 * 1024 * 1024

F32 = jnp.float32
BF16 = jnp.bfloat16
NEG_INF = float("-inf")


def _params(semantics):
    return pltpu.CompilerParams(dimension_semantics=semantics, vmem_limit_bytes=VMEM_LIMIT)


def _layer_norm(z, g, b):
    mu = jnp.mean(z, axis=-1, keepdims=True)
    zc = z - mu
    var = jnp.mean(zc * zc, axis=-1, keepdims=True)
    return zc * lax.rsqrt(var + LN_EPS) * g + b


def _silu(x):
    return x * jax.nn.sigmoid(x)


def _pool_ln_kernel(x_ref, w_ref, scale_ref, g_ref, b_ref, o_ref, ext_ref, *, alpha, tile):
    s = pl.program_id(1)
    d = x_ref.shape[2]
    c = d // len(POOL_WINDOWS)

    @pl.when(s == 0)
    def _():
        ext_ref[0:POOL_HALO, :] = jnp.zeros((POOL_HALO, d), F32)

    x = x_ref[0]
    ext_ref[POOL_HALO:POOL_HALO + tile, :] = x
    pos = (s * tile + 1 + lax.broadcasted_iota(jnp.int32, (tile, 1), 0)).astype(F32)
    ys = []
    for gi, w in enumerate(POOL_WINDOWS):
        cs = slice(gi * c, (gi + 1) * c)
        acc = ext_ref[:, cs]
        span = 1
        while span < w:
            acc = acc + pltpu.roll(acc, span, axis=0)
            span *= 2
        mean = acc[POOL_HALO:, :] / jnp.minimum(pos, float(w))
        mixed = mean - x[:, cs]
        ys.append(jnp.dot(mixed.astype(BF16), w_ref[gi], preferred_element_type=F32))
    y = jnp.concatenate(ys, axis=1) * scale_ref[...]
    o_ref[0] = _layer_norm(alpha * x + y, g_ref[...], b_ref[...])
    ext_ref[0:POOL_HALO, :] = x[tile - POOL_HALO:, :]


def _pool_ln(x, w, scale, g, b, alpha, tile=512):
    bsz, seq, d = x.shape
    ng, c, _ = w.shape
    vec = pl.BlockSpec((1, d), lambda i, j: (0, 0))
    return pl.pallas_call(
        functools.partial(_pool_ln_kernel, alpha=alpha, tile=tile),
        grid=(bsz, seq // tile),
        in_specs=[pl.BlockSpec((1, tile, d), lambda i, j: (i, j, 0)),
                  pl.BlockSpec((ng, c, c), lambda i, j: (0, 0, 0)),
                  vec, vec, vec],
        out_specs=pl.BlockSpec((1, tile, d), lambda i, j: (i, j, 0)),
        out_shape=jax.ShapeDtypeStruct(x.shape, F32),
        scratch_shapes=[pltpu.VMEM((POOL_HALO + tile, d), F32)],
        compiler_params=_params(("arbitrary", "arbitrary")),
        name="pool_ln",
    )(x, w.astype(BF16), scale.reshape(1, d), g.reshape(1, d), b.reshape(1, d))


def _first_argmax(v, iota, size):
    m = jnp.max(v, axis=0, keepdims=True)
    idx = jnp.min(jnp.where(v == m, iota, size), axis=0, keepdims=True)
    return m, idx


def _router_kernel(h_ref, wt_ref, b_ref, eidx_ref, gw_ref, rank_ref, cnt_ref, carry_ref, *, tile):
    i = pl.program_id(0)
    n_e = wt_ref.shape[0]
    per_group = n_e // N_EXPERT_GROUPS

    @pl.when(i == 0)
    def _():
        carry_ref[...] = jnp.zeros_like(carry_ref)

    h = h_ref[...]
    h_hi = h.astype(BF16)
    h_lo = (h - h_hi.astype(F32)).astype(BF16)
    wt = wt_ref[...]
    wt_hi = wt.astype(BF16)
    wt_lo = (wt - wt_hi.astype(F32)).astype(BF16)
    nt_dims = (((1,), (1,)), ((), ()))
    logits = (lax.dot_general(wt_hi, h_hi, nt_dims, preferred_element_type=F32)
              + (lax.dot_general(wt_lo, h_hi, nt_dims, preferred_element_type=F32)
                 + lax.dot_general(wt_hi, h_lo, nt_dims, preferred_element_type=F32)))
    s = jax.nn.sigmoid(logits)
    sb = s + b_ref[...]

    iota_g = lax.broadcasted_iota(jnp.int32, (per_group, tile), 0)
    gscores = []
    for g in range(N_EXPERT_GROUPS):
        blk = sb[g * per_group:(g + 1) * per_group, :]
        m1, i1 = _first_argmax(blk, iota_g, per_group)
        m2 = jnp.max(jnp.where(iota_g == i1, NEG_INF, blk), axis=0, keepdims=True)
        gscores.append(m1 + m2)
    gs = jnp.concatenate(gscores, axis=0)
    iota_8 = lax.broadcasted_iota(jnp.int32, (N_EXPERT_GROUPS, tile), 0)
    gsel = jnp.zeros((N_EXPERT_GROUPS, tile), jnp.bool_)
    for _ in range(TOPK_GROUPS):
        _, gi = _first_argmax(gs, iota_8, N_EXPERT_GROUPS)
        hit = iota_8 == gi
        gsel = jnp.logical_or(gsel, hit)
        gs = jnp.where(hit, NEG_INF, gs)
    cand = jnp.concatenate(
        [jnp.where(gsel[g:g + 1, :], sb[g * per_group:(g + 1) * per_group, :], NEG_INF)
         for g in range(N_EXPERT_GROUPS)], axis=0)

    iota_e = lax.broadcasted_iota(jnp.int32, (n_e, tile), 0)
    member = jnp.zeros((n_e, tile), F32)
    idxs, svals = [], []
    for _ in range(TOP_K):
        _, ei = _first_argmax(cand, iota_e, n_e)
        hit = iota_e == ei
        idxs.append(ei)
        svals.append(jnp.sum(jnp.where(hit, s, 0.0), axis=0, keepdims=True))
        member = jnp.where(hit, 1.0, member)
        cand = jnp.where(hit, NEG_INF, cand)
    sv = jnp.concatenate(svals, axis=0)
    gw_ref[...] = sv / jnp.sum(sv, axis=0, keepdims=True) * ROUTED_SCALE
    eidx_ref[...] = jnp.concatenate(idxs, axis=0)

    r = lax.broadcasted_iota(jnp.int32, (tile, tile), 0)
    c = lax.broadcasted_iota(jnp.int32, (tile, tile), 1)
    earlier = (r < c).astype(BF16)
    before = jnp.dot(member.astype(BF16), earlier, preferred_element_type=F32) + carry_ref[...]
    ranks = [jnp.sum(jnp.where(iota_e == ei, before, 0.0), axis=0, keepdims=True) for ei in idxs]
    rank_ref[...] = jnp.concatenate(ranks, axis=0).astype(jnp.int32)
    carry_ref[...] = carry_ref[...] + jnp.sum(member, axis=1, keepdims=True)
    cnt_ref[...] = carry_ref[...].astype(jnp.int32)


def _router(h, w_router, b_router, tile=512):
    n, d = h.shape
    n_e = w_router.shape[1]
    out_kt = pl.BlockSpec((TOP_K, tile), lambda i: (0, i))
    return pl.pallas_call(
        functools.partial(_router_kernel, tile=tile),
        grid=(n // tile,),
        in_specs=[pl.BlockSpec((tile, d), lambda i: (i, 0)),
                  pl.BlockSpec((n_e, d), lambda i: (0, 0)),
                  pl.BlockSpec((n_e, 1), lambda i: (0, 0))],
        out_specs=[out_kt, out_kt, out_kt, pl.BlockSpec((n_e, 1), lambda i: (0, 0))],
        out_shape=[jax.ShapeDtypeStruct((TOP_K, n), jnp.int32),
                   jax.ShapeDtypeStruct((TOP_K, n), F32),
                   jax.ShapeDtypeStruct((TOP_K, n), jnp.int32),
                   jax.ShapeDtypeStruct((n_e, 1), jnp.int32)],
        scratch_shapes=[pltpu.VMEM((n_e, 1), F32)],
        compiler_params=_params(("arbitrary",)),
        name="router",
    )(h, w_router.T, b_router.reshape(n_e, 1))


def _dest_kernel(eidx_ref, rank_ref, pstart_ref, dest_ref):
    n_e = pstart_ref.shape[0]
    tile = eidx_ref.shape[1]
    iota_e = lax.broadcasted_iota(jnp.int32, (n_e, tile), 0)
    starts = pstart_ref[...]
    rows = [jnp.sum(jnp.where(iota_e == eidx_ref[k:k + 1, :], starts, 0), axis=0, keepdims=True)
            for k in range(TOP_K)]
    dest_ref[0] = jnp.concatenate(rows, axis=0) + rank_ref[...]


def _dest_rows(eidx, rank, pstart):
    n = eidx.shape[1]
    n_e = pstart.shape[0]
    kt = pl.BlockSpec((TOP_K, DEST_TILE), lambda i: (0, i))
    return pl.pallas_call(
        _dest_kernel,
        grid=(n // DEST_TILE,),
        in_specs=[kt, kt, pl.BlockSpec((n_e, 1), lambda i: (0, 0))],
        out_specs=pl.BlockSpec((1, TOP_K, DEST_TILE), lambda i: (i, 0, 0)),
        out_shape=jax.ShapeDtypeStruct((n // DEST_TILE, TOP_K, DEST_TILE), jnp.int32),
        compiler_params=_params(("arbitrary",)),
        name="dest_rows",
    )(eidx, rank, pstart.reshape(n_e, 1))


def _dispatch_kernel(dest_ref, h_ref, xg_hbm, sem):
    tiles = dest_ref.shape[0]
    for sub in range(tiles):
        def group(g, carry, sub=sub):
            off = pl.multiple_of(g * DMA_UNROLL, DMA_UNROLL)
            for u in range(DMA_UNROLL):
                for k in range(TOP_K):
                    pltpu.make_async_copy(h_ref.at[pl.ds(sub * DEST_TILE + off + u, 1)],
                                          xg_hbm.at[pl.ds(dest_ref[sub, k, off + u], 1)],
                                          sem).start(priority=k % DMA_QUEUES)
            return carry
        lax.fori_loop(0, DEST_TILE // DMA_UNROLL, group, 0)
    for _ in range(tiles * TOP_K):
        pltpu.make_async_copy(xg_hbm.at[pl.ds(0, DEST_TILE)], xg_hbm.at[pl.ds(0, DEST_TILE)],
                              sem).wait()


def _dispatch(dest, h, n_rows, tiles=4):
    n, d = h.shape
    rows = tiles * DEST_TILE
    return pl.pallas_call(
        _dispatch_kernel,
        grid=(n // rows,),
        in_specs=[pl.BlockSpec((tiles, TOP_K, DEST_TILE), lambda i: (i, 0, 0),
                               memory_space=pltpu.SMEM),
                  pl.BlockSpec((rows, d), lambda i: (i, 0))],
        out_specs=pl.BlockSpec(memory_space=pl.ANY),
        out_shape=jax.ShapeDtypeStruct((n_rows, d), F32),
        scratch_shapes=[pltpu.SemaphoreType.DMA(())],
        compiler_params=_params(("arbitrary",)),
        name="dispatch_rows",
    )(dest, h)


def _grouped_kernel(ps_ref, cnt_ref, xg_hbm, wgu_ref, wd_ref, y_hbm, xfirst, xbuf, ybuf, wgu_bf,
                    wd_bf, fsem, isem, osem, done_ref, osize_ref):
    e = pl.program_id(0)
    last = pl.num_programs(0) - 1
    n_rows = cnt_ref[e]
    base = ps_ref[e]
    n_chunks = (n_rows + MOE_BLOCK - 1) // MOE_BLOCK
    f = wd_ref.shape[2]
    row = lax.broadcasted_iota(jnp.int32, (MOE_BLOCK, 1), 0)

    def dma_rows(count, c):
        valid = jnp.minimum(count - c * MOE_BLOCK, MOE_BLOCK)
        return pl.multiple_of((valid + SUBLANES - 1) // SUBLANES * SUBLANES, SUBLANES)

    def first_copy(ex):
        size = dma_rows(cnt_ref[ex], 0)
        start = pl.multiple_of(ps_ref[ex], MOE_BLOCK)
        slot = lax.rem(ex, 2)
        return pltpu.make_async_copy(xg_hbm.at[pl.ds(start, size)],
                                     xfirst.at[slot, pl.ds(0, size)], fsem.at[slot])

    def in_copy(c, slot):
        size = dma_rows(n_rows, c)
        start = pl.multiple_of(base + c * MOE_BLOCK, MOE_BLOCK)
        return pltpu.make_async_copy(xg_hbm.at[pl.ds(start, size)],
                                     xbuf.at[slot, pl.ds(0, size)], isem.at[slot])

    def out_copy(start, size, slot):
        return pltpu.make_async_copy(ybuf.at[slot, pl.ds(0, size)],
                                     y_hbm.at[pl.ds(start, size)], osem.at[slot])

    def process(x32, c):
        x = jnp.where(row < n_rows - c * MOE_BLOCK, x32, 0.0).astype(BF16)
        gu = jnp.dot(x, wgu_bf[...], preferred_element_type=F32)
        act = _silu(gu[:, :f]) * gu[:, f:]
        y = jnp.dot(act.astype(BF16), wd_bf[...], preferred_element_type=F32)
        done = done_ref[0]
        slot = lax.rem(done, 2)

        @pl.when(done >= 2)
        def _():
            out_copy(0, pl.multiple_of(osize_ref[slot], SUBLANES), slot).wait()

        ybuf[slot] = y
        size = dma_rows(n_rows, c)
        osize_ref[slot] = size
        out_copy(pl.multiple_of(base + c * MOE_BLOCK, MOE_BLOCK), size, slot).start()
        done_ref[0] = done + 1

    @pl.when(e == 0)
    def _():
        done_ref[0] = 0
        xfirst[...] = jnp.zeros_like(xfirst)
        xbuf[...] = jnp.zeros_like(xbuf)

        @pl.when(n_rows > 0)
        def _():
            first_copy(0).start()

    nxt = jnp.minimum(e + 1, last)

    @pl.when(jnp.logical_and(e < last, cnt_ref[nxt] > 0))
    def _():
        first_copy(nxt).start()

    @pl.when(n_chunks > 0)
    def _():
        wgu_bf[...] = wgu_ref[0, 0].astype(BF16)
        wd_bf[...] = wd_ref[0, 0].astype(BF16)

        @pl.when(n_chunks > 1)
        def _():
            in_copy(1, 1).start()

        first_copy(e).wait()
        process(xfirst[lax.rem(e, 2)], 0)

        def chunk(c, carry):
            slot = lax.rem(c, 2)

            @pl.when(c + 1 < n_chunks)
            def _():
                in_copy(c + 1, 1 - slot).start()

            in_copy(c, slot).wait()
            process(xbuf[slot], c)
            return carry

        lax.fori_loop(1, n_chunks, chunk, 0)

    @pl.when(e == last)
    def _():
        for back in (1, 2):
            @pl.when(done_ref[0] >= back)
            def _():
                slot = lax.rem(done_ref[0] - back, 2)
                out_copy(0, pl.multiple_of(osize_ref[slot], SUBLANES), slot).wait()


def _grouped_experts(xg, pstart, counts, w_gate_up, w_down, layer):
    p, d = xg.shape
    _, n_e, _, f2 = w_gate_up.shape
    f = f2 // 2
    grid_spec = pltpu.PrefetchScalarGridSpec(
        num_scalar_prefetch=2,
        grid=(n_e,),
        in_specs=[
            pl.BlockSpec(memory_space=pl.ANY),
            pl.BlockSpec((1, 1, d, f2), lambda e, ps, cnt: (layer, e, 0, 0)),
            pl.BlockSpec((1, 1, f, d), lambda e, ps, cnt: (layer, e, 0, 0)),
        ],
        out_specs=pl.BlockSpec(memory_space=pl.ANY),
        scratch_shapes=[pltpu.VMEM((2, MOE_BLOCK, d), F32), pltpu.VMEM((2, MOE_BLOCK, d), F32),
                        pltpu.VMEM((2, MOE_BLOCK, d), F32),
                        pltpu.VMEM((d, f2), BF16), pltpu.VMEM((f, d), BF16),
                        pltpu.SemaphoreType.DMA((2,)), pltpu.SemaphoreType.DMA((2,)),
                        pltpu.SemaphoreType.DMA((2,)),
                        pltpu.SMEM((1,), jnp.int32), pltpu.SMEM((2,), jnp.int32)],
    )
    return pl.pallas_call(
        _grouped_kernel,
        grid_spec=grid_spec,
        out_shape=jax.ShapeDtypeStruct((p, d), F32),
        compiler_params=_params(("arbitrary",)),
        name="grouped_experts",
    )(pstart, counts, xg, w_gate_up, w_down)


def _combine_ln_kernel(dpair_ref, dnext_ref, h_ref, gw_ref, y_hbm, wsgu_ref, wsd_ref, g_ref, b_ref,
                       o_ref, ybuf, sem, *, alpha):
    i = pl.program_id(0)
    n_steps = pl.num_programs(0)
    f = wsd_ref.shape[0]

    def start_gather(dref, sub, s):
        def group(g, carry):
            off = pl.multiple_of(g * DMA_UNROLL, DMA_UNROLL)
            for u in range(DMA_UNROLL):
                for k in range(TOP_K):
                    pltpu.make_async_copy(y_hbm.at[pl.ds(dref[sub, k, off + u], 1)],
                                          ybuf.at[s, k, pl.ds(off, DMA_UNROLL)].at[pl.ds(u, 1)],
                                          sem.at[s]).start(priority=k % DMA_QUEUES)
            return carry
        lax.fori_loop(0, DEST_TILE // DMA_UNROLL, group, 0)

    def finish(s):
        rows = slice(s * DEST_TILE, (s + 1) * DEST_TILE)
        h = h_ref[rows, :]
        gu = jnp.dot(h.astype(BF16), wsgu_ref[...], preferred_element_type=F32)
        act = _silu(gu[:, :f]) * gu[:, f:]
        shared = jnp.dot(act.astype(BF16), wsd_ref[...], preferred_element_type=F32)
        for k in range(TOP_K):
            pltpu.make_async_copy(y_hbm.at[pl.ds(0, DEST_TILE)], ybuf.at[s, k], sem.at[s]).wait()
        gw = gw_ref[rows, :]
        routed = gw[:, 0:1] * ybuf[s, 0]
        for k in range(1, TOP_K):
            routed = routed + gw[:, k:k + 1] * ybuf[s, k]
        o_ref[rows, :] = _layer_norm(alpha * h + (routed + shared), g_ref[...], b_ref[...])

    @pl.when(i == 0)
    def _():
        start_gather(dpair_ref, 0, 0)

    start_gather(dpair_ref, 1, 1)
    finish(0)

    @pl.when(i + 1 < n_steps)
    def _():
        start_gather(dnext_ref, 0, 0)

    finish(1)


def _combine_ln(dest, h, gw, y_rows, ws_gate_up, ws_down, g, b, alpha):
    n, d = h.shape
    f2 = ws_gate_up.shape[1]
    nt = n // DEST_TILE
    rows = 2 * DEST_TILE
    vec = pl.BlockSpec((1, d), lambda i: (0, 0))
    return pl.pallas_call(
        functools.partial(_combine_ln_kernel, alpha=alpha),
        grid=(nt // 2,),
        in_specs=[pl.BlockSpec((2, TOP_K, DEST_TILE), lambda i: (i, 0, 0),
                               memory_space=pltpu.SMEM),
                  pl.BlockSpec((1, TOP_K, DEST_TILE),
                               lambda i: (jnp.minimum(2 * i + 2, nt - 1), 0, 0),
                               memory_space=pltpu.SMEM),
                  pl.BlockSpec((rows, d), lambda i: (i, 0)),
                  pl.BlockSpec((rows, TOP_K), lambda i: (i, 0)),
                  pl.BlockSpec(memory_space=pl.ANY),
                  pl.BlockSpec((d, f2), lambda i: (0, 0)),
                  pl.BlockSpec((f2 // 2, d), lambda i: (0, 0)), vec, vec],
        out_specs=pl.BlockSpec((rows, d), lambda i: (i, 0)),
        out_shape=jax.ShapeDtypeStruct((n, d), F32),
        scratch_shapes=[pltpu.VMEM((2, TOP_K, DEST_TILE, d), F32),
                        pltpu.SemaphoreType.DMA((2,))],
        compiler_params=_params(("arbitrary",)),
        name="combine_ln",
    )(dest, dest, h, gw, y_rows, ws_gate_up.astype(BF16), ws_down.astype(BF16),
      g.reshape(1, d), b.reshape(1, d))


def _moe_ln(h, w_router, b_router, w_gate_up, w_down, layer, ws_gate_up, ws_down, g, b, alpha):
    n, d = h.shape
    n_e = w_router.shape[1]
    eidx, gw, rank, counts = _router(h, w_router, b_router)

    nb = -(-(n * TOP_K + n_e * (MOE_BLOCK - 1)) // MOE_BLOCK)
    counts = counts[:, 0]
    padded = (counts + MOE_BLOCK - 1) // MOE_BLOCK * MOE_BLOCK
    pstart = (jnp.cumsum(padded) - padded).astype(jnp.int32)

    dest = _dest_rows(eidx, rank, pstart)
    xg = _dispatch(dest, h, nb * MOE_BLOCK)
    y_rows = _grouped_experts(xg, pstart, counts, w_gate_up, w_down, layer)
    return _combine_ln(dest, h, gw.T, y_rows, ws_gate_up, ws_down, g, b, alpha)


F_PIECES = 3
ONES_ROWS = 16


def _gate_placement(n_heads):
    import numpy as np
    n_pairs = n_heads // 2
    pk = np.zeros((F_PIECES, LANES, n_pairs * LANES), np.float32)
    ones_k = np.zeros((1, n_pairs * LANES), np.float32)
    pq = np.zeros((F_PIECES, LANES, n_pairs * LANES), np.float32)
    ones_q = np.zeros((1, n_pairs * LANES), np.float32)
    for h in range(n_heads):
        p, e = divmod(h, 2)
        for j in range(F_PIECES):
            pk[j, h, p * LANES + F_PIECES * e + j] = -1.0
            ones_q[0, p * LANES + F_PIECES * e + j] = 1.0
            pq[j, h, p * LANES + F_PIECES * (2 + e) + j] = 1.0
            ones_k[0, p * LANES + F_PIECES * (2 + e) + j] = 1.0
    return (jnp.asarray(pk, BF16), jnp.asarray(ones_k), jnp.asarray(pq, BF16), jnp.asarray(ones_q))


def _kvf_kernel(h_ref, wk_ref, wvt_ref, wf_ref, bf_ref, pk_ref, onesk_ref, k_ref, vt_ref, fp_ref,
                carry_ref, *, tile):
    s = pl.program_id(1)
    n_heads = vt_ref.shape[1]
    n_pairs = n_heads // 2

    @pl.when(s == 0)
    def _():
        carry_ref[...] = jnp.zeros_like(carry_ref)

    h = h_ref[0]
    hb = h.astype(BF16)
    h_lo = (h - hb.astype(F32)).astype(BF16)
    wf = wf_ref[...]
    wf_hi = wf.astype(BF16)
    wf_lo = (wf - wf_hi.astype(F32)).astype(BF16)
    z = (jnp.dot(hb, wf_hi, preferred_element_type=F32)
         + (jnp.dot(hb, wf_lo, preferred_element_type=F32)
            + jnp.dot(h_lo, wf_hi, preferred_element_type=F32))) + bf_ref[...]
    logf = jnp.minimum(z, 0.0) - jnp.log1p(jnp.exp(-jnp.abs(z)))
    row = lax.broadcasted_iota(jnp.int32, (tile, 1), 0)
    span = 1
    while span < tile:
        logf = logf + jnp.where(row >= span, pltpu.roll(logf, span, axis=0), 0.0)
        span *= 2
    cum = logf + carry_ref[...]
    carry_ref[...] = cum[tile - 1:tile, :]

    pieces = []
    rest = cum
    for _ in range(F_PIECES):
        piece = rest.astype(BF16)
        pieces.append(piece)
        rest = rest - piece.astype(F32)
    fp_ref[0] = jnp.concatenate(pieces, axis=1)
    kf = onesk_ref[...]
    for j in range(F_PIECES):
        kf = kf + jnp.dot(pieces[j], pk_ref[j], preferred_element_type=F32)

    k = jnp.dot(hb, wk_ref[...], preferred_element_type=F32)
    parts = []
    for p in range(n_pairs):
        parts += [k[:, p * LANES:(p + 1) * LANES], kf[:, p * LANES:(p + 1) * LANES]]
    k_ref[0] = jnp.concatenate(parts, axis=1).astype(BF16)

    vt = lax.dot_general(wvt_ref[...], hb, (((1,), (1,)), ((), ())),
                         preferred_element_type=F32).astype(BF16)
    for hd in range(n_heads):
        vt_ref[0, hd, 0:HEAD_DIM, :] = vt[hd * HEAD_DIM:(hd + 1) * HEAD_DIM, :]
        vt_ref[0, hd, HEAD_DIM:HEAD_DIM + ONES_ROWS, :] = jnp.ones((ONES_ROWS, tile), BF16)


def _kvf(h3, w_kvf, b_f, pk, ones_k, tile=512):
    bsz, seq, d = h3.shape
    n_h = b_f.shape[0]
    wk = w_kvf[:, :d].astype(BF16)
    wvt = w_kvf[:, d:2 * d].T.astype(BF16)
    wf = jnp.pad(w_kvf[:, 2 * d:], ((0, 0), (0, LANES - n_h)))
    bf = jnp.pad(b_f, (0, LANES - n_h)).reshape(1, LANES)
    const2 = lambda i, j: (0, 0)
    return pl.pallas_call(
        functools.partial(_kvf_kernel, tile=tile),
        grid=(bsz, seq // tile),
        in_specs=[pl.BlockSpec((1, tile, d), lambda i, j: (i, j, 0)),
                  pl.BlockSpec((d, d), const2), pl.BlockSpec((d, d), const2),
                  pl.BlockSpec((d, LANES), const2), pl.BlockSpec((1, LANES), const2),
                  pl.BlockSpec(pk.shape, lambda i, j: (0, 0, 0)),
                  pl.BlockSpec(ones_k.shape, const2)],
        out_specs=[pl.BlockSpec((1, tile, 2 * d), lambda i, j: (i, j, 0)),
                   pl.BlockSpec((1, n_h, HEAD_DIM + ONES_ROWS, tile), lambda i, j: (i, 0, 0, j)),
                   pl.BlockSpec((1, tile, F_PIECES * LANES), lambda i, j: (i, j, 0))],
        out_shape=[jax.ShapeDtypeStruct((bsz, seq, 2 * d), BF16),
                   jax.ShapeDtypeStruct((bsz, n_h, HEAD_DIM + ONES_ROWS, seq), BF16),
                   jax.ShapeDtypeStruct((bsz, seq, F_PIECES * LANES), BF16)],
        scratch_shapes=[pltpu.VMEM((1, LANES), F32)],
        compiler_params=_params(("arbitrary", "arbitrary")),
        name="kvf_proj",
    )(h3, wk, wvt, wf, bf, pk, ones_k)


def _qproj_kernel(h_ref, wq_ref, fp_ref, pq_ref, onesq_ref, q_ref, *, scale):
    n_pairs = q_ref.shape[1] // (2 * LANES)
    q = jnp.dot(h_ref[...].astype(BF16), wq_ref[...], preferred_element_type=F32) * scale
    qf = onesq_ref[...]
    for j in range(F_PIECES):
        qf = qf + jnp.dot(fp_ref[:, j * LANES:(j + 1) * LANES], pq_ref[j],
                          preferred_element_type=F32)
    parts = []
    for p in range(n_pairs):
        parts += [q[:, p * LANES:(p + 1) * LANES], qf[:, p * LANES:(p + 1) * LANES]]
    q_ref[...] = jnp.concatenate(parts, axis=1).astype(BF16)


def _qproj(h, w_q, fpieces, pq, ones_q, scale, tile=512):
    n, d = h.shape
    n_h = d // HEAD_DIM
    const2 = lambda i: (0, 0)
    return pl.pallas_call(
        functools.partial(_qproj_kernel, scale=scale),
        grid=(n // tile,),
        in_specs=[pl.BlockSpec((tile, d), lambda i: (i, 0)),
                  pl.BlockSpec((d, d), const2),
                  pl.BlockSpec((tile, F_PIECES * LANES), lambda i: (i, 0)),
                  pl.BlockSpec(pq.shape, lambda i: (0, 0, 0)),
                  pl.BlockSpec(ones_q.shape, const2)],
        out_specs=pl.BlockSpec((tile, n_h * LANES), lambda i: (i, 0)),
        out_shape=jax.ShapeDtypeStruct((n, n_h * LANES), BF16),
        compiler_params=_params(("arbitrary",)),
        name="q_proj",
    )(h, w_q.astype(BF16), fpieces, pq, ones_q)


def _attn_kernel(q_ref, k_ref, vt_ref, o_ref, sa_ref, sb_ref, *, tq, tk):
    i = pl.program_id(2)
    per_q = tq // tk
    qa = q_ref[0]
    lane = lax.broadcasted_iota(jnp.int32, (1, 2 * LANES), 1)

    def lanes_in(start, size):
        return jnp.logical_and(lane >= start, lane < start + size)

    qs = []
    for hd in range(2):
        own = jnp.logical_or(
            lanes_in(hd * HEAD_DIM, HEAD_DIM),
            jnp.logical_or(lanes_in(LANES + hd * F_PIECES, F_PIECES),
                           lanes_in(LANES + (2 + hd) * F_PIECES, F_PIECES)))
        qs.append(jnp.where(own, qa, jnp.zeros_like(qa)))
    row = lax.broadcasted_iota(jnp.int32, (tk, tq), 0)
    col = lax.broadcasted_iota(jnp.int32, (tk, tq), 1)

    def logits(j, s_ref):
        kj = k_ref[0, pl.ds(pl.multiple_of(j * tk, tk), tk), :]
        for hd in range(2):
            s_ref[hd] = lax.dot_general(kj, qs[hd], (((1,), (1,)), ((), ())),
                                        preferred_element_type=F32)

    def softmax_pv(j, s_ref, carry, diag_offset):
        ms, ls, accs = carry
        keys = pl.ds(pl.multiple_of(j * tk, tk), tk)
        new_ms, new_ls, new_accs = [], [], []
        for hd in range(2):
            sc = s_ref[hd]
            if diag_offset is not None:
                sc = jnp.where(row + diag_offset <= col, sc, NEG_INF)
            m_new = jnp.maximum(ms[hd], jnp.max(sc, axis=0, keepdims=True))
            pt = jnp.exp(sc - m_new).astype(BF16)
            alpha = jnp.exp(ms[hd] - m_new)
            r = jnp.dot(vt_ref[0, hd, :, keys], pt,
                        preferred_element_type=F32)
            new_ms.append(m_new)
            new_ls.append(alpha * ls[hd] + r[HEAD_DIM:HEAD_DIM + 1, :])
            new_accs.append(alpha * accs[hd] + r[:HEAD_DIM, :])
        return tuple(new_ms), tuple(new_ls), tuple(new_accs)

    def two_blocks(t, carry):
        j = t * 2
        logits(j + 1, sb_ref)
        carry = softmax_pv(j, sa_ref, carry, None)
        logits(j + 2, sa_ref)
        return softmax_pv(j + 1, sb_ref, carry, None)

    init = ((jnp.full((1, tq), NEG_INF, F32),) * 2, (jnp.zeros((1, tq), F32),) * 2,
            (jnp.zeros((HEAD_DIM, tq), F32),) * 2)
    logits(0, sa_ref)
    first_diag = i * per_q
    carry = lax.fori_loop(0, first_diag // 2, two_blocks, init)
    bufs = (sa_ref, sb_ref)
    for u in range(per_q):
        if u + 1 < per_q:
            logits(first_diag + u + 1, bufs[(u + 1) % 2])
        carry = softmax_pv(first_diag + u, bufs[u % 2], carry, u * tk)
    _, ls, accs = carry
    out_t = jnp.concatenate([accs[0] / ls[0], accs[1] / ls[1]], axis=0)
    o_ref[0] = out_t.T.astype(BF16)


def _attention(qaug, kaug, vaug_t, d, tq=512, tk=256):
    bsz, seq, _ = kaug.shape
    n_pairs = vaug_t.shape[1] // 2
    assert (tq // tk) % 2 == 0
    return pl.pallas_call(
        functools.partial(_attn_kernel, tq=tq, tk=tk),
        grid=(bsz, n_pairs, seq // tq),
        in_specs=[pl.BlockSpec((1, tq, 2 * LANES), lambda b, p, i: (b, i, p)),
                  pl.BlockSpec((1, seq, 2 * LANES), lambda b, p, i: (b, 0, p)),
                  pl.BlockSpec((1, 2, HEAD_DIM + ONES_ROWS, seq), lambda b, p, i: (b, p, 0, 0))],
        out_specs=pl.BlockSpec((1, tq, LANES), lambda b, p, i: (b, i, p)),
        out_shape=jax.ShapeDtypeStruct((bsz, seq, d), BF16),
        scratch_shapes=[pltpu.VMEM((2, tk, tq), F32), pltpu.VMEM((2, tk, tq), F32)],
        compiler_params=_params(("arbitrary", "arbitrary", "arbitrary")),
        name="fox_attention",
    )(qaug, kaug, vaug_t)


def _oproj_ln_kernel(o_ref, h_ref, wo_ref, g_ref, b_ref, out_ref, *, alpha):
    mix = jnp.dot(o_ref[...], wo_ref[...], preferred_element_type=F32)
    out_ref[...] = _layer_norm(alpha * h_ref[...] + mix, g_ref[...], b_ref[...])


def _oproj_ln(o, h, w_o, g, b, alpha, tile=512):
    n, d = h.shape
    vec = pl.BlockSpec((1, d), lambda i: (0, 0))
    tspec = pl.BlockSpec((tile, d), lambda i: (i, 0))
    return pl.pallas_call(
        functools.partial(_oproj_ln_kernel, alpha=alpha),
        grid=(n // tile,),
        in_specs=[tspec, tspec, pl.BlockSpec((d, d), lambda i: (0, 0)), vec, vec],
        out_specs=tspec,
        out_shape=jax.ShapeDtypeStruct((n, d), F32),
        compiler_params=_params(("arbitrary",)),
        name="oproj_ln",
    )(o, h, w_o.astype(BF16), g.reshape(1, d), b.reshape(1, d))


def kernel(x, pool_w, pool_scale, w_q, w_o, w_kvf, b_f, ln_g, ln_b, router_w, router_b,
           w_gate_up, w_down, ws_gate_up, ws_down):
    bsz, seq, d = x.shape
    depth = ln_g.shape[0]
    n_pool = pool_w.shape[0]
    alpha = float((2 * depth) ** 0.25)
    n = bsz * seq
    kaug = vaug_t = fpieces = None
    pk, ones_k, pq, ones_q = _gate_placement(d // HEAD_DIM)
    for l in range(depth):
        if l < n_pool:
            x = _pool_ln(x, pool_w[l], pool_scale[l], ln_g[l, 0], ln_b[l, 0], alpha)
        else:
            if l == n_pool:
                kaug, vaug_t, fpieces = _kvf(x, w_kvf, b_f, pk, ones_k)
            j = l - n_pool
            qaug = _qproj(x.reshape(n, d), w_q[j], fpieces.reshape(n, -1), pq, ones_q,
                          HEAD_DIM ** -0.5)
            o = _attention(qaug.reshape(bsz, seq, -1), kaug, vaug_t, d)
            x = _oproj_ln(o.reshape(n, d), x.reshape(n, d), w_o[j], ln_g[l, 0], ln_b[l, 0],
                          alpha).reshape(bsz, seq, d)
        x = _moe_ln(x.reshape(n, d), router_w[l], router_b[l], w_gate_up, w_down, l,
                    ws_gate_up[l], ws_down[l], ln_g[l, 1], ln_b[l, 1], alpha).reshape(bsz, seq, d)
    return x
```

```python
import functools

import jax
import jax.numpy as jnp
from jax import lax
from jax.experimental import pallas as pl
from jax.experimental.pallas import tpu as pltpu

POOL_WINDOWS = (2, 4, 8, 16)
POOL_HALO = 16
HEAD_DIM = 64
N_EXPERT_GROUPS = 8
TOPK_GROUPS = 4
TOP_K = 8
ROUTED_SCALE = 2.5
MOE_BLOCK = 256
DMA_UNROLL = 8
DMA_QUEUES = 2
DEST_TILE = 256
LN_EPS = 1e-5

LANES = 128
SUBLANES = 8
VMEM_LIMIT = 48 * 1024 * 1024

F32 = jnp.float32
BF16 = jnp.bfloat16
NEG_INF = float("-inf")


def _params(semantics):
    return pltpu.CompilerParams(dimension_semantics=semantics, vmem_limit_bytes=VMEM_LIMIT)


def _layer_norm(z, g, b):
    mu = jnp.mean(z, axis=-1, keepdims=True)
    zc = z - mu
    var = jnp.mean(zc * zc, axis=-1, keepdims=True)
    return zc * lax.rsqrt(var + LN_EPS) * g + b


def _silu(x):
    return x * jax.nn.sigmoid(x)


def _pool_ln_kernel(x_ref, w_ref, scale_ref, g_ref, b_ref, o_ref, ext_ref, *, alpha, tile):
    s = pl.program_id(1)
    d = x_ref.shape[2]
    c = d // len(POOL_WINDOWS)

    @pl.when(s == 0)
    def _():
        ext_ref[0:POOL_HALO, :] = jnp.zeros((POOL_HALO, d), F32)

    x = x_ref[0]
    ext_ref[POOL_HALO:POOL_HALO + tile, :] = x
    pos = (s * tile + 1 + lax.broadcasted_iota(jnp.int32, (tile, 1), 0)).astype(F32)
    ys = []
    for gi, w in enumerate(POOL_WINDOWS):
        cs = slice(gi * c, (gi + 1) * c)
        acc = ext_ref[:, cs]
        span = 1
        while span < w:
            acc = acc + pltpu.roll(acc, span, axis=0)
            span *= 2
        mean = acc[POOL_HALO:, :] / jnp.minimum(pos, float(w))
        mixed = mean - x[:, cs]
        ys.append(jnp.dot(mixed.astype(BF16), w_ref[gi], preferred_element_type=F32))
    y = jnp.concatenate(ys, axis=1) * scale_ref[...]
    o_ref[0] = _layer_norm(alpha * x + y, g_ref[...], b_ref[...])
    ext_ref[0:POOL_HALO, :] = x[tile - POOL_HALO:, :]


def _pool_ln(x, w, scale, g, b, alpha, tile=512):
    bsz, seq, d = x.shape
    ng, c, _ = w.shape
    vec = pl.BlockSpec((1, d), lambda i, j: (0, 0))
    return pl.pallas_call(
        functools.partial(_pool_ln_kernel, alpha=alpha, tile=tile),
        grid=(bsz, seq // tile),
        in_specs=[pl.BlockSpec((1, tile, d), lambda i, j: (i, j, 0)),
                  pl.BlockSpec((ng, c, c), lambda i, j: (0, 0, 0)),
                  vec, vec, vec],
        out_specs=pl.BlockSpec((1, tile, d), lambda i, j: (i, j, 0)),
        out_shape=jax.ShapeDtypeStruct(x.shape, F32),
        scratch_shapes=[pltpu.VMEM((POOL_HALO + tile, d), F32)],
        compiler_params=_params(("arbitrary", "arbitrary")),
        name="pool_ln",
    )(x, w.astype(BF16), scale.reshape(1, d), g.reshape(1, d), b.reshape(1, d))


def _first_argmax(v, iota, size):
    m = jnp.max(v, axis=0, keepdims=True)
    idx = jnp.min(jnp.where(v == m, iota, size), axis=0, keepdims=True)
    return m, idx


def _router_kernel(h_ref, wt_ref, b_ref, eidx_ref, gw_ref, rank_ref, cnt_ref, carry_ref, *, tile):
    i = pl.program_id(0)
    n_e = wt_ref.shape[0]
    per_group = n_e // N_EXPERT_GROUPS

    @pl.when(i == 0)
    def _():
        carry_ref[...] = jnp.zeros_like(carry_ref)

    h = h_ref[...]
    h_hi = h.astype(BF16)
    h_lo = (h - h_hi.astype(F32)).astype(BF16)
    wt = wt_ref[...]
    wt_hi = wt.astype(BF16)
    wt_lo = (wt - wt_hi.astype(F32)).astype(BF16)
    nt_dims = (((1,), (1,)), ((), ()))
    logits = (lax.dot_general(wt_hi, h_hi, nt_dims, preferred_element_type=F32)
              + (lax.dot_general(wt_lo, h_hi, nt_dims, preferred_element_type=F32)
                 + lax.dot_general(wt_hi, h_lo, nt_dims, preferred_element_type=F32)))
    s = jax.nn.sigmoid(logits)
    sb = s + b_ref[...]

    iota_g = lax.broadcasted_iota(jnp.int32, (per_group, tile), 0)
    gscores = []
    for g in range(N_EXPERT_GROUPS):
        blk = sb[g * per_group:(g + 1) * per_group, :]
        m1, i1 = _first_argmax(blk, iota_g, per_group)
        m2 = jnp.max(jnp.where(iota_g == i1, NEG_INF, blk), axis=0, keepdims=True)
        gscores.append(m1 + m2)
    gs = jnp.concatenate(gscores, axis=0)
    iota_8 = lax.broadcasted_iota(jnp.int32, (N_EXPERT_GROUPS, tile), 0)
    gsel = jnp.zeros((N_EXPERT_GROUPS, tile), jnp.bool_)
    for _ in range(TOPK_GROUPS):
        _, gi = _first_argmax(gs, iota_8, N_EXPERT_GROUPS)
        hit = iota_8 == gi
        gsel = jnp.logical_or(gsel, hit)
        gs = jnp.where(hit, NEG_INF, gs)
    cand = jnp.concatenate(
        [jnp.where(gsel[g:g + 1, :], sb[g * per_group:(g + 1) * per_group, :], NEG_INF)
         for g in range(N_EXPERT_GROUPS)], axis=0)

    iota_e = lax.broadcasted_iota(jnp.int32, (n_e, tile), 0)
    member = jnp.zeros((n_e, tile), F32)
    idxs, svals = [], []
    for _ in range(TOP_K):
        _, ei = _first_argmax(cand, iota_e, n_e)
        hit = iota_e == ei
        idxs.append(ei)
        svals.append(jnp.sum(jnp.where(hit, s, 0.0), axis=0, keepdims=True))
        member = jnp.where(hit, 1.0, member)
        cand = jnp.where(hit, NEG_INF, cand)
    sv = jnp.concatenate(svals, axis=0)
    gw_ref[...] = sv / jnp.sum(sv, axis=0, keepdims=True) * ROUTED_SCALE
    eidx_ref[...] = jnp.concatenate(idxs, axis=0)

    r = lax.broadcasted_iota(jnp.int32, (tile, tile), 0)
    c = lax.broadcasted_iota(jnp.int32, (tile, tile), 1)
    earlier = (r < c).astype(BF16)
    before = jnp.dot(member.astype(BF16), earlier, preferred_element_type=F32) + carry_ref[...]
    ranks = [jnp.sum(jnp.where(iota_e == ei, before, 0.0), axis=0, keepdims=True) for ei in idxs]
    rank_ref[...] = jnp.concatenate(ranks, axis=0).astype(jnp.int32)
    carry_ref[...] = carry_ref[...] + jnp.sum(member, axis=1, keepdims=True)
    cnt_ref[...] = carry_ref[...].astype(jnp.int32)


def _router(h, w_router, b_router, tile=512):
    n, d = h.shape
    n_e = w_router.shape[1]
    out_kt = pl.BlockSpec((TOP_K, tile), lambda i: (0, i))
    return pl.pallas_call(
        functools.partial(_router_kernel, tile=tile),
        grid=(n // tile,),
        in_specs=[pl.BlockSpec((tile, d), lambda i: (i, 0)),
                  pl.BlockSpec((n_e, d), lambda i: (0, 0)),
                  pl.BlockSpec((n_e, 1), lambda i: (0, 0))],
        out_specs=[out_kt, out_kt, out_kt, pl.BlockSpec((n_e, 1), lambda i: (0, 0))],
        out_shape=[jax.ShapeDtypeStruct((TOP_K, n), jnp.int32),
                   jax.ShapeDtypeStruct((TOP_K, n), F32),
                   jax.ShapeDtypeStruct((TOP_K, n), jnp.int32),
                   jax.ShapeDtypeStruct((n_e, 1), jnp.int32)],
        scratch_shapes=[pltpu.VMEM((n_e, 1), F32)],
        compiler_params=_params(("arbitrary",)),
        name="router",
    )(h, w_router.T, b_router.reshape(n_e, 1))


def _dest_kernel(eidx_ref, rank_ref, pstart_ref, dest_ref):
    n_e = pstart_ref.shape[0]
    tile = eidx_ref.shape[1]
    iota_e = lax.broadcasted_iota(jnp.int32, (n_e, tile), 0)
    starts = pstart_ref[...]
    rows = [jnp.sum(jnp.where(iota_e == eidx_ref[k:k + 1, :], starts, 0), axis=0, keepdims=True)
            for k in range(TOP_K)]
    dest_ref[0] = jnp.concatenate(rows, axis=0) + rank_ref[...]


def _dest_rows(eidx, rank, pstart):
    n = eidx.shape[1]
    n_e = pstart.shape[0]
    kt = pl.BlockSpec((TOP_K, DEST_TILE), lambda i: (0, i))
    return pl.pallas_call(
        _dest_kernel,
        grid=(n // DEST_TILE,),
        in_specs=[kt, kt, pl.BlockSpec((n_e, 1), lambda i: (0, 0))],
        out_specs=pl.BlockSpec((1, TOP_K, DEST_TILE), lambda i: (i, 0, 0)),
        out_shape=jax.ShapeDtypeStruct((n // DEST_TILE, TOP_K, DEST_TILE), jnp.int32),
        compiler_params=_params(("arbitrary",)),
        name="dest_rows",
    )(eidx, rank, pstart.reshape(n_e, 1))


def _dispatch_kernel(dest_ref, h_ref, xg_hbm, sem):
    tiles = dest_ref.shape[0]
    for sub in range(tiles):
        def group(g, carry, sub=sub):
            off = pl.multiple_of(g * DMA_UNROLL, DMA_UNROLL)
            for u in range(DMA_UNROLL):
                for k in range(TOP_K):
                    pltpu.make_async_copy(h_ref.at[pl.ds(sub * DEST_TILE + off + u, 1)],
                                          xg_hbm.at[pl.ds(dest_ref[sub, g, u * TOP_K + k], 1)],
                                          sem).start(priority=k % DMA_QUEUES)
            return carry
        lax.fori_loop(0, DEST_TILE // DMA_UNROLL, group, 0)
    for _ in range(tiles * TOP_K):
        pltpu.make_async_copy(xg_hbm.at[pl.ds(0, DEST_TILE)], xg_hbm.at[pl.ds(0, DEST_TILE)],
                              sem).wait()


def _dispatch(dest, h, n_rows, tiles=4):
    n, d = h.shape
    rows = tiles * DEST_TILE
    return pl.pallas_call(
        _dispatch_kernel,
        grid=(n // rows,),
        in_specs=[pl.BlockSpec((tiles,) + dest.shape[1:], lambda i: (i, 0, 0),
                               memory_space=pltpu.SMEM),
                  pl.BlockSpec((rows, d), lambda i: (i, 0))],
        out_specs=pl.BlockSpec(memory_space=pl.ANY),
        out_shape=jax.ShapeDtypeStruct((n_rows, d), F32),
        scratch_shapes=[pltpu.SemaphoreType.DMA(())],
        compiler_params=_params(("arbitrary",)),
        name="dispatch_rows",
    )(dest, h)


def _grouped_kernel(ps_ref, cnt_ref, xg_hbm, wgu_ref, wd_ref, y_hbm, xfirst, xbuf, ybuf, wgu_bf,
                    wd_bf, fsem, isem, osem, done_ref, osize_ref):
    e = pl.program_id(0)
    last = pl.num_programs(0) - 1
    n_rows = cnt_ref[e]
    base = ps_ref[e]
    n_chunks = (n_rows + MOE_BLOCK - 1) // MOE_BLOCK
    f = wd_ref.shape[2]
    row = lax.broadcasted_iota(jnp.int32, (MOE_BLOCK, 1), 0)

    def dma_rows(count, c):
        valid = jnp.minimum(count - c * MOE_BLOCK, MOE_BLOCK)
        return pl.multiple_of((valid + SUBLANES - 1) // SUBLANES * SUBLANES, SUBLANES)

    def first_copy(ex):
        size = dma_rows(cnt_ref[ex], 0)
        start = pl.multiple_of(ps_ref[ex], MOE_BLOCK)
        slot = lax.rem(ex, 2)
        return pltpu.make_async_copy(xg_hbm.at[pl.ds(start, size)],
                                     xfirst.at[slot, pl.ds(0, size)], fsem.at[slot])

    def in_copy(c, slot):
        size = dma_rows(n_rows, c)
        start = pl.multiple_of(base + c * MOE_BLOCK, MOE_BLOCK)
        return pltpu.make_async_copy(xg_hbm.at[pl.ds(start, size)],
                                     xbuf.at[slot, pl.ds(0, size)], isem.at[slot])

    def out_copy(start, size, slot):
        return pltpu.make_async_copy(ybuf.at[slot, pl.ds(0, size)],
                                     y_hbm.at[pl.ds(start, size)], osem.at[slot])

    def process(x32, c):
        x = jnp.where(row < n_rows - c * MOE_BLOCK, x32, 0.0).astype(BF16)
        gu = jnp.dot(x, wgu_bf[...], preferred_element_type=F32)
        act = _silu(gu[:, :f]) * gu[:, f:]
        y = jnp.dot(act.astype(BF16), wd_bf[...], preferred_element_type=F32)
        done = done_ref[0]
        slot = lax.rem(done, 2)

        @pl.when(done >= 2)
        def _():
            out_copy(0, pl.multiple_of(osize_ref[slot], SUBLANES), slot).wait()

        ybuf[slot] = y
        size = dma_rows(n_rows, c)
        osize_ref[slot] = size
        out_copy(pl.multiple_of(base + c * MOE_BLOCK, MOE_BLOCK), size, slot).start()
        done_ref[0] = done + 1

    @pl.when(e == 0)
    def _():
        done_ref[0] = 0
        xfirst[...] = jnp.zeros_like(xfirst)
        xbuf[...] = jnp.zeros_like(xbuf)

        @pl.when(n_rows > 0)
        def _():
            first_copy(0).start()

    nxt = jnp.minimum(e + 1, last)

    @pl.when(jnp.logical_and(e < last, cnt_ref[nxt] > 0))
    def _():
        first_copy(nxt).start()

    @pl.when(n_chunks > 0)
    def _():
        wgu_bf[...] = wgu_ref[0, 0].astype(BF16)
        wd_bf[...] = wd_ref[0, 0].astype(BF16)

        @pl.when(n_chunks > 1)
        def _():
            in_copy(1, 1).start()

        first_copy(e).wait()
        process(xfirst[lax.rem(e, 2)], 0)

        def chunk(c, carry):
            slot = lax.rem(c, 2)

            @pl.when(c + 1 < n_chunks)
            def _():
                in_copy(c + 1, 1 - slot).start()

            in_copy(c, slot).wait()
            process(xbuf[slot], c)
            return carry

        lax.fori_loop(1, n_chunks, chunk, 0)

    @pl.when(e == last)
    def _():
        for back in (1, 2):
            @pl.when(done_ref[0] >= back)
            def _():
                slot = lax.rem(done_ref[0] - back, 2)
                out_copy(0, pl.multiple_of(osize_ref[slot], SUBLANES), slot).wait()


def _grouped_experts(xg, pstart, counts, w_gate_up, w_down, layer):
    p, d = xg.shape
    _, n_e, _, f2 = w_gate_up.shape
    f = f2 // 2
    grid_spec = pltpu.PrefetchScalarGridSpec(
        num_scalar_prefetch=2,
        grid=(n_e,),
        in_specs=[
            pl.BlockSpec(memory_space=pl.ANY),
            pl.BlockSpec((1, 1, d, f2), lambda e, ps, cnt: (layer, e, 0, 0)),
            pl.BlockSpec((1, 1, f, d), lambda e, ps, cnt: (layer, e, 0, 0)),
        ],
        out_specs=pl.BlockSpec(memory_space=pl.ANY),
        scratch_shapes=[pltpu.VMEM((2, MOE_BLOCK, d), F32), pltpu.VMEM((2, MOE_BLOCK, d), F32),
                        pltpu.VMEM((2, MOE_BLOCK, d), F32),
                        pltpu.VMEM((d, f2), BF16), pltpu.VMEM((f, d), BF16),
                        pltpu.SemaphoreType.DMA((2,)), pltpu.SemaphoreType.DMA((2,)),
                        pltpu.SemaphoreType.DMA((2,)),
                        pltpu.SMEM((1,), jnp.int32), pltpu.SMEM((2,), jnp.int32)],
    )
    return pl.pallas_call(
        _grouped_kernel,
        grid_spec=grid_spec,
        out_shape=jax.ShapeDtypeStruct((p, d), F32),
        compiler_params=_params(("arbitrary",)),
        name="grouped_experts",
    )(pstart, counts, xg, w_gate_up, w_down)


def _combine_ln_kernel(dpair_ref, dnext_ref, h_ref, gw_ref, y_hbm, wsgu_ref, wsd_ref, g_ref, b_ref,
                       o_ref, ybuf, sem, *, alpha):
    i = pl.program_id(0)
    n_steps = pl.num_programs(0)
    f = wsd_ref.shape[0]

    def start_gather(dref, sub, s):
        def group(g, carry):
            off = pl.multiple_of(g * DMA_UNROLL, DMA_UNROLL)
            for u in range(DMA_UNROLL):
                for k in range(TOP_K):
                    pltpu.make_async_copy(y_hbm.at[pl.ds(dref[sub, g, u * TOP_K + k], 1)],
                                          ybuf.at[s, k, pl.ds(off, DMA_UNROLL)].at[pl.ds(u, 1)],
                                          sem.at[s]).start(priority=k % DMA_QUEUES)
            return carry
        lax.fori_loop(0, DEST_TILE // DMA_UNROLL, group, 0)

    def finish(s):
        rows = slice(s * DEST_TILE, (s + 1) * DEST_TILE)
        h = h_ref[rows, :]
        gu = jnp.dot(h.astype(BF16), wsgu_ref[...], preferred_element_type=F32)
        act = _silu(gu[:, :f]) * gu[:, f:]
        shared = jnp.dot(act.astype(BF16), wsd_ref[...], preferred_element_type=F32)
        for k in range(TOP_K):
            pltpu.make_async_copy(y_hbm.at[pl.ds(0, DEST_TILE)], ybuf.at[s, k], sem.at[s]).wait()
        gw = gw_ref[rows, :]
        routed = gw[:, 0:1] * ybuf[s, 0]
        for k in range(1, TOP_K):
            routed = routed + gw[:, k:k + 1] * ybuf[s, k]
        o_ref[rows, :] = _layer_norm(alpha * h + (routed + shared), g_ref[...], b_ref[...])

    @pl.when(i == 0)
    def _():
        start_gather(dpair_ref, 0, 0)

    start_gather(dpair_ref, 1, 1)
    finish(0)

    @pl.when(i + 1 < n_steps)
    def _():
        start_gather(dnext_ref, 0, 0)

    finish(1)


def _combine_ln(dest, h, gw, y_rows, ws_gate_up, ws_down, g, b, alpha):
    n, d = h.shape
    f2 = ws_gate_up.shape[1]
    nt = n // DEST_TILE
    rows = 2 * DEST_TILE
    vec = pl.BlockSpec((1, d), lambda i: (0, 0))
    return pl.pallas_call(
        functools.partial(_combine_ln_kernel, alpha=alpha),
        grid=(nt // 2,),
        in_specs=[pl.BlockSpec((2,) + dest.shape[1:], lambda i: (i, 0, 0),
                               memory_space=pltpu.SMEM),
                  pl.BlockSpec((1,) + dest.shape[1:],
                               lambda i: (jnp.minimum(2 * i + 2, nt - 1), 0, 0),
                               memory_space=pltpu.SMEM),
                  pl.BlockSpec((rows, d), lambda i: (i, 0)),
                  pl.BlockSpec((rows, TOP_K), lambda i: (i, 0)),
                  pl.BlockSpec(memory_space=pl.ANY),
                  pl.BlockSpec((d, f2), lambda i: (0, 0)),
                  pl.BlockSpec((f2 // 2, d), lambda i: (0, 0)), vec, vec],
        out_specs=pl.BlockSpec((rows, d), lambda i: (i, 0)),
        out_shape=jax.ShapeDtypeStruct((n, d), F32),
        scratch_shapes=[pltpu.VMEM((2, TOP_K, DEST_TILE, d), F32),
                        pltpu.SemaphoreType.DMA((2,))],
        compiler_params=_params(("arbitrary",)),
        name="combine_ln",
    )(dest, dest, h, gw, y_rows, ws_gate_up.astype(BF16), ws_down.astype(BF16),
      g.reshape(1, d), b.reshape(1, d))


def _moe_ln(h, w_router, b_router, w_gate_up, w_down, layer, ws_gate_up, ws_down, g, b, alpha):
    n, d = h.shape
    n_e = w_router.shape[1]
    eidx, gw, rank, counts = _router(h, w_router, b_router)

    nb = -(-(n * TOP_K + n_e * (MOE_BLOCK - 1)) // MOE_BLOCK)
    counts = counts[:, 0]
    padded = (counts + MOE_BLOCK - 1) // MOE_BLOCK * MOE_BLOCK
    pstart = (jnp.cumsum(padded) - padded).astype(jnp.int32)

    dest = _dest_rows(eidx, rank, pstart)
    dest = dest.reshape(-1, TOP_K, DEST_TILE // DMA_UNROLL, DMA_UNROLL)
    dest = dest.transpose(0, 2, 3, 1).reshape(-1, DEST_TILE // DMA_UNROLL, DMA_UNROLL * TOP_K)
    xg = _dispatch(dest, h, nb * MOE_BLOCK)
    y_rows = _grouped_experts(xg, pstart, counts, w_gate_up, w_down, layer)
    return _combine_ln(dest, h, gw.T, y_rows, ws_gate_up, ws_down, g, b, alpha)


F_PIECES = 3
ONES_ROWS = 16


def _gate_placement(n_heads):
    import numpy as np
    n_pairs = n_heads // 2
    pk = np.zeros((F_PIECES, LANES, n_pairs * LANES), np.float32)
    ones_k = np.zeros((1, n_pairs * LANES), np.float32)
    pq = np.zeros((F_PIECES, LANES, n_pairs * LANES), np.float32)
    ones_q = np.zeros((1, n_pairs * LANES), np.float32)
    for h in range(n_heads):
        p, e = divmod(h, 2)
        for j in range(F_PIECES):
            pk[j, h, p * LANES + F_PIECES * e + j] = -1.0
            ones_q[0, p * LANES + F_PIECES * e + j] = 1.0
            pq[j, h, p * LANES + F_PIECES * (2 + e) + j] = 1.0
            ones_k[0, p * LANES + F_PIECES * (2 + e) + j] = 1.0
    return (jnp.asarray(pk, BF16), jnp.asarray(ones_k), jnp.asarray(pq, BF16), jnp.asarray(ones_q))


def _kvf_kernel(h_ref, wk_ref, wvt_ref, wf_ref, bf_ref, pk_ref, onesk_ref, k_ref, vt_ref, fp_ref,
                carry_ref, *, tile):
    s = pl.program_id(1)
    n_heads = vt_ref.shape[1]
    n_pairs = n_heads // 2

    @pl.when(s == 0)
    def _():
        carry_ref[...] = jnp.zeros_like(carry_ref)

    h = h_ref[0]
    hb = h.astype(BF16)
    h_lo = (h - hb.astype(F32)).astype(BF16)
    wf = wf_ref[...]
    wf_hi = wf.astype(BF16)
    wf_lo = (wf - wf_hi.astype(F32)).astype(BF16)
    z = (jnp.dot(hb, wf_hi, preferred_element_type=F32)
         + (jnp.dot(hb, wf_lo, preferred_element_type=F32)
            + jnp.dot(h_lo, wf_hi, preferred_element_type=F32))) + bf_ref[...]
    logf = jnp.minimum(z, 0.0) - jnp.log1p(jnp.exp(-jnp.abs(z)))
    row = lax.broadcasted_iota(jnp.int32, (tile, 1), 0)
    span = 1
    while span < tile:
        logf = logf + jnp.where(row >= span, pltpu.roll(logf, span, axis=0), 0.0)
        span *= 2
    cum = logf + carry_ref[...]
    carry_ref[...] = cum[tile - 1:tile, :]

    pieces = []
    rest = cum
    for _ in range(F_PIECES):
        piece = rest.astype(BF16)
        pieces.append(piece)
        rest = rest - piece.astype(F32)
    fp_ref[0] = jnp.concatenate(pieces, axis=1)
    kf = onesk_ref[...]
    for j in range(F_PIECES):
        kf = kf + jnp.dot(pieces[j], pk_ref[j], preferred_element_type=F32)

    k = jnp.dot(hb, wk_ref[...], preferred_element_type=F32)
    parts = []
    for p in range(n_pairs):
        parts += [k[:, p * LANES:(p + 1) * LANES], kf[:, p * LANES:(p + 1) * LANES]]
    k_ref[0] = jnp.concatenate(parts, axis=1).astype(BF16)

    vt = lax.dot_general(wvt_ref[...], hb, (((1,), (1,)), ((), ())),
                         preferred_element_type=F32).astype(BF16)
    for hd in range(n_heads):
        vt_ref[0, hd, 0:HEAD_DIM, :] = vt[hd * HEAD_DIM:(hd + 1) * HEAD_DIM, :]
        vt_ref[0, hd, HEAD_DIM:HEAD_DIM + ONES_ROWS, :] = jnp.ones((ONES_ROWS, tile), BF16)


def _kvf(h3, w_kvf, b_f, pk, ones_k, tile=512):
    bsz, seq, d = h3.shape
    n_h = b_f.shape[0]
    wk = w_kvf[:, :d].astype(BF16)
    wvt = w_kvf[:, d:2 * d].T.astype(BF16)
    wf = jnp.pad(w_kvf[:, 2 * d:], ((0, 0), (0, LANES - n_h)))
    bf = jnp.pad(b_f, (0, LANES - n_h)).reshape(1, LANES)
    const2 = lambda i, j: (0, 0)
    return pl.pallas_call(
        functools.partial(_kvf_kernel, tile=tile),
        grid=(bsz, seq // tile),
        in_specs=[pl.BlockSpec((1, tile, d), lambda i, j: (i, j, 0)),
                  pl.BlockSpec((d, d), const2), pl.BlockSpec((d, d), const2),
                  pl.BlockSpec((d, LANES), const2), pl.BlockSpec((1, LANES), const2),
                  pl.BlockSpec(pk.shape, lambda i, j: (0, 0, 0)),
                  pl.BlockSpec(ones_k.shape, const2)],
        out_specs=[pl.BlockSpec((1, tile, 2 * d), lambda i, j: (i, j, 0)),
                   pl.BlockSpec((1, n_h, HEAD_DIM + ONES_ROWS, tile), lambda i, j: (i, 0, 0, j)),
                   pl.BlockSpec((1, tile, F_PIECES * LANES), lambda i, j: (i, j, 0))],
        out_shape=[jax.ShapeDtypeStruct((bsz, seq, 2 * d), BF16),
                   jax.ShapeDtypeStruct((bsz, n_h, HEAD_DIM + ONES_ROWS, seq), BF16),
                   jax.ShapeDtypeStruct((bsz, seq, F_PIECES * LANES), BF16)],
        scratch_shapes=[pltpu.VMEM((1, LANES), F32)],
        compiler_params=_params(("arbitrary", "arbitrary")),
        name="kvf_proj",
    )(h3, wk, wvt, wf, bf, pk, ones_k)


def _qproj_kernel(h_ref, wq_ref, fp_ref, pq_ref, onesq_ref, q_ref, *, scale):
    n_pairs = q_ref.shape[1] // (2 * LANES)
    q = jnp.dot(h_ref[...].astype(BF16), wq_ref[...], preferred_element_type=F32) * scale
    qf = onesq_ref[...]
    for j in range(F_PIECES):
        qf = qf + jnp.dot(fp_ref[:, j * LANES:(j + 1) * LANES], pq_ref[j],
                          preferred_element_type=F32)
    parts = []
    for p in range(n_pairs):
        parts += [q[:, p * LANES:(p + 1) * LANES], qf[:, p * LANES:(p + 1) * LANES]]
    q_ref[...] = jnp.concatenate(parts, axis=1).astype(BF16)


def _qproj(h, w_q, fpieces, pq, ones_q, scale, tile=512):
    n, d = h.shape
    n_h = d // HEAD_DIM
    const2 = lambda i: (0, 0)
    return pl.pallas_call(
        functools.partial(_qproj_kernel, scale=scale),
        grid=(n // tile,),
        in_specs=[pl.BlockSpec((tile, d), lambda i: (i, 0)),
                  pl.BlockSpec((d, d), const2),
                  pl.BlockSpec((tile, F_PIECES * LANES), lambda i: (i, 0)),
                  pl.BlockSpec(pq.shape, lambda i: (0, 0, 0)),
                  pl.BlockSpec(ones_q.shape, const2)],
        out_specs=pl.BlockSpec((tile, n_h * LANES), lambda i: (i, 0)),
        out_shape=jax.ShapeDtypeStruct((n, n_h * LANES), BF16),
        compiler_params=_params(("arbitrary",)),
        name="q_proj",
    )(h, w_q.astype(BF16), fpieces, pq, ones_q)


def _attn_kernel(q_ref, k_ref, vt_ref, o_ref, sa_ref, sb_ref, *, tq, tk):
    i = pl.program_id(2)
    per_q = tq // tk
    qa = q_ref[0]
    lane = lax.broadcasted_iota(jnp.int32, (1, 2 * LANES), 1)

    def lanes_in(start, size):
        return jnp.logical_and(lane >= start, lane < start + size)

    qs = []
    for hd in range(2):
        own = jnp.logical_or(
            lanes_in(hd * HEAD_DIM, HEAD_DIM),
            jnp.logical_or(lanes_in(LANES + hd * F_PIECES, F_PIECES),
                           lanes_in(LANES + (2 + hd) * F_PIECES, F_PIECES)))
        qs.append(jnp.where(own, qa, jnp.zeros_like(qa)))
    row = lax.broadcasted_iota(jnp.int32, (tk, tq), 0)
    col = lax.broadcasted_iota(jnp.int32, (tk, tq), 1)

    def logits(j, s_ref):
        kj = k_ref[0, pl.ds(pl.multiple_of(j * tk, tk), tk), :]
        for hd in range(2):
            s_ref[hd] = lax.dot_general(kj, qs[hd], (((1,), (1,)), ((), ())),
                                        preferred_element_type=F32)

    def softmax_pv(j, s_ref, carry, diag_offset):
        ms, ls, accs = carry
        keys = pl.ds(pl.multiple_of(j * tk, tk), tk)
        new_ms, new_ls, new_accs = [], [], []
        for hd in range(2):
            sc = s_ref[hd]
            if diag_offset is not None:
                sc = jnp.where(row + diag_offset <= col, sc, NEG_INF)
            m_new = jnp.maximum(ms[hd], jnp.max(sc, axis=0, keepdims=True))
            pt = jnp.exp(sc - m_new).astype(BF16)
            alpha = jnp.exp(ms[hd] - m_new)
            r = jnp.dot(vt_ref[0, hd, :, keys], pt,
                        preferred_element_type=F32)
            new_ms.append(m_new)
            new_ls.append(alpha * ls[hd] + r[HEAD_DIM:HEAD_DIM + 1, :])
            new_accs.append(alpha * accs[hd] + r[:HEAD_DIM, :])
        return tuple(new_ms), tuple(new_ls), tuple(new_accs)

    def two_blocks(t, carry):
        j = t * 2
        logits(j + 1, sb_ref)
        carry = softmax_pv(j, sa_ref, carry, None)
        logits(j + 2, sa_ref)
        return softmax_pv(j + 1, sb_ref, carry, None)

    init = ((jnp.full((1, tq), NEG_INF, F32),) * 2, (jnp.zeros((1, tq), F32),) * 2,
            (jnp.zeros((HEAD_DIM, tq), F32),) * 2)
    logits(0, sa_ref)
    first_diag = i * per_q
    carry = lax.fori_loop(0, first_diag // 2, two_blocks, init)
    bufs = (sa_ref, sb_ref)
    for u in range(per_q):
        if u + 1 < per_q:
            logits(first_diag + u + 1, bufs[(u + 1) % 2])
        carry = softmax_pv(first_diag + u, bufs[u % 2], carry, u * tk)
    _, ls, accs = carry
    out_t = jnp.concatenate([accs[0] / ls[0], accs[1] / ls[1]], axis=0)
    o_ref[0] = out_t.T.astype(BF16)


def _attention(qaug, kaug, vaug_t, d, tq=512, tk=256):
    bsz, seq, _ = kaug.shape
    n_pairs = vaug_t.shape[1] // 2
    assert (tq // tk) % 2 == 0
    return pl.pallas_call(
        functools.partial(_attn_kernel, tq=tq, tk=tk),
        grid=(bsz, n_pairs, seq // tq),
        in_specs=[pl.BlockSpec((1, tq, 2 * LANES), lambda b, p, i: (b, i, p)),
                  pl.BlockSpec((1, seq, 2 * LANES), lambda b, p, i: (b, 0, p)),
                  pl.BlockSpec((1, 2, HEAD_DIM + ONES_ROWS, seq), lambda b, p, i: (b, p, 0, 0))],
        out_specs=pl.BlockSpec((1, tq, LANES), lambda b, p, i: (b, i, p)),
        out_shape=jax.ShapeDtypeStruct((bsz, seq, d), BF16),
        scratch_shapes=[pltpu.VMEM((2, tk, tq), F32), pltpu.VMEM((2, tk, tq), F32)],
        compiler_params=_params(("arbitrary", "arbitrary", "arbitrary")),
        name="fox_attention",
    )(qaug, kaug, vaug_t)


def _oproj_ln_kernel(o_ref, h_ref, wo_ref, g_ref, b_ref, out_ref, *, alpha):
    mix = jnp.dot(o_ref[...], wo_ref[...], preferred_element_type=F32)
    out_ref[...] = _layer_norm(alpha * h_ref[...] + mix, g_ref[...], b_ref[...])


def _oproj_ln(o, h, w_o, g, b, alpha, tile=512):
    n, d = h.shape
    vec = pl.BlockSpec((1, d), lambda i: (0, 0))
    tspec = pl.BlockSpec((tile, d), lambda i: (i, 0))
    return pl.pallas_call(
        functools.partial(_oproj_ln_kernel, alpha=alpha),
        grid=(n // tile,),
        in_specs=[tspec, tspec, pl.BlockSpec((d, d), lambda i: (0, 0)), vec, vec],
        out_specs=tspec,
        out_shape=jax.ShapeDtypeStruct((n, d), F32),
        compiler_params=_params(("arbitrary",)),
        name="oproj_ln",
    )(o, h, w_o.astype(BF16), g.reshape(1, d), b.reshape(1, d))


def kernel(x, pool_w, pool_scale, w_q, w_o, w_kvf, b_f, ln_g, ln_b, router_w, router_b,
           w_gate_up, w_down, ws_gate_up, ws_down):
    bsz, seq, d = x.shape
    depth = ln_g.shape[0]
    n_pool = pool_w.shape[0]
    alpha = float((2 * depth) ** 0.25)
    n = bsz * seq
    kaug = vaug_t = fpieces = None
    pk, ones_k, pq, ones_q = _gate_placement(d // HEAD_DIM)
    for l in range(depth):
        if l < n_pool:
            x = _pool_ln(x, pool_w[l], pool_scale[l], ln_g[l, 0], ln_b[l, 0], alpha)
        else:
            if l == n_pool:
                kaug, vaug_t, fpieces = _kvf(x, w_kvf, b_f, pk, ones_k)
            j = l - n_pool
            qaug = _qproj(x.reshape(n, d), w_q[j], fpieces.reshape(n, -1), pq, ones_q,
                          HEAD_DIM ** -0.5)
            o = _attention(qaug.reshape(bsz, seq, -1), kaug, vaug_t, d)
            x = _oproj_ln(o.reshape(n, d), x.reshape(n, d), w_o[j], ln_g[l, 0], ln_b[l, 0],
                          alpha).reshape(bsz, seq, d)
        x = _moe_ln(x.reshape(n, d), router_w[l], router_b[l], w_gate_up, w_down, l,
                    ws_gate_up[l], ws_down[l], ln_g[l, 1], ln_b[l, 1], alpha).reshape(bsz, seq, d)
    return x
```

```python
import functools

import jax
import jax.numpy as jnp
from jax import lax
from jax.experimental import pallas as pl
from jax.experimental.pallas import tpu as pltpu

POOL_WINDOWS = (2, 4, 8, 16)
POOL_HALO = 16
HEAD_DIM = 64
N_EXPERT_GROUPS = 8
TOPK_GROUPS = 4
TOP_K = 8
ROUTED_SCALE = 2.5
MOE_BLOCK = 512
DMA_UNROLL = 8
DMA_QUEUES = 2
DEST_TILE = 256
LN_EPS = 1e-5

LANES = 128
SUBLANES = 8
VMEM_LIMIT = 48 * 1024 * 1024

F32 = jnp.float32
BF16 = jnp.bfloat16
NEG_INF = float("-inf")


def _params(semantics):
    return pltpu.CompilerParams(dimension_semantics=semantics, vmem_limit_bytes=VMEM_LIMIT)


def _layer_norm(z, g, b):
    mu = jnp.mean(z, axis=-1, keepdims=True)
    zc = z - mu
    var = jnp.mean(zc * zc, axis=-1, keepdims=True)
    return zc * lax.rsqrt(var + LN_EPS) * g + b


def _silu(x):
    return x * jax.nn.sigmoid(x)


def _pool_ln_kernel(x_ref, w_ref, scale_ref, g_ref, b_ref, o_ref, ext_ref, *, alpha, tile):
    s = pl.program_id(1)
    d = x_ref.shape[2]
    c = d // len(POOL_WINDOWS)

    @pl.when(s == 0)
    def _():
        ext_ref[0:POOL_HALO, :] = jnp.zeros((POOL_HALO, d), F32)

    x = x_ref[0]
    ext_ref[POOL_HALO:POOL_HALO + tile, :] = x
    pos = (s * tile + 1 + lax.broadcasted_iota(jnp.int32, (tile, 1), 0)).astype(F32)
    ys = []
    for gi, w in enumerate(POOL_WINDOWS):
        cs = slice(gi * c, (gi + 1) * c)
        acc = ext_ref[:, cs]
        span = 1
        while span < w:
            acc = acc + pltpu.roll(acc, span, axis=0)
            span *= 2
        mean = acc[POOL_HALO:, :] / jnp.minimum(pos, float(w))
        mixed = mean - x[:, cs]
        ys.append(jnp.dot(mixed.astype(BF16), w_ref[gi], preferred_element_type=F32))
    y = jnp.concatenate(ys, axis=1) * scale_ref[...]
    o_ref[0] = _layer_norm(alpha * x + y, g_ref[...], b_ref[...])
    ext_ref[0:POOL_HALO, :] = x[tile - POOL_HALO:, :]


def _pool_ln(x, w, scale, g, b, alpha, tile=512):
    bsz, seq, d = x.shape
    ng, c, _ = w.shape
    vec = pl.BlockSpec((1, d), lambda i, j: (0, 0))
    return pl.pallas_call(
        functools.partial(_pool_ln_kernel, alpha=alpha, tile=tile),
        grid=(bsz, seq // tile),
        in_specs=[pl.BlockSpec((1, tile, d), lambda i, j: (i, j, 0)),
                  pl.BlockSpec((ng, c, c), lambda i, j: (0, 0, 0)),
                  vec, vec, vec],
        out_specs=pl.BlockSpec((1, tile, d), lambda i, j: (i, j, 0)),
        out_shape=jax.ShapeDtypeStruct(x.shape, F32),
        scratch_shapes=[pltpu.VMEM((POOL_HALO + tile, d), F32)],
        compiler_params=_params(("arbitrary", "arbitrary")),
        name="pool_ln",
    )(x, w.astype(BF16), scale.reshape(1, d), g.reshape(1, d), b.reshape(1, d))


def _first_argmax(v, iota, size):
    m = jnp.max(v, axis=0, keepdims=True)
    idx = jnp.min(jnp.where(v == m, iota, size), axis=0, keepdims=True)
    return m, idx


def _router_kernel(h_ref, wt_ref, b_ref, eidx_ref, gw_ref, rank_ref, cnt_ref, carry_ref, *, tile):
    i = pl.program_id(0)
    n_e = wt_ref.shape[0]
    per_group = n_e // N_EXPERT_GROUPS

    @pl.when(i == 0)
    def _():
        carry_ref[...] = jnp.zeros_like(carry_ref)

    h = h_ref[...]
    h_hi = h.astype(BF16)
    h_lo = (h - h_hi.astype(F32)).astype(BF16)
    wt = wt_ref[...]
    wt_hi = wt.astype(BF16)
    wt_lo = (wt - wt_hi.astype(F32)).astype(BF16)
    nt_dims = (((1,), (1,)), ((), ()))
    logits = (lax.dot_general(wt_hi, h_hi, nt_dims, preferred_element_type=F32)
              + (lax.dot_general(wt_lo, h_hi, nt_dims, preferred_element_type=F32)
                 + lax.dot_general(wt_hi, h_lo, nt_dims, preferred_element_type=F32)))
    s = jax.nn.sigmoid(logits)
    sb = s + b_ref[...]

    iota_g = lax.broadcasted_iota(jnp.int32, (per_group, tile), 0)
    gscores = []
    for g in range(N_EXPERT_GROUPS):
        blk = sb[g * per_group:(g + 1) * per_group, :]
        m1, i1 = _first_argmax(blk, iota_g, per_group)
        m2 = jnp.max(jnp.where(iota_g == i1, NEG_INF, blk), axis=0, keepdims=True)
        gscores.append(m1 + m2)
    gs = jnp.concatenate(gscores, axis=0)
    iota_8 = lax.broadcasted_iota(jnp.int32, (N_EXPERT_GROUPS, tile), 0)
    gsel = jnp.zeros((N_EXPERT_GROUPS, tile), jnp.bool_)
    for _ in range(TOPK_GROUPS):
        _, gi = _first_argmax(gs, iota_8, N_EXPERT_GROUPS)
        hit = iota_8 == gi
        gsel = jnp.logical_or(gsel, hit)
        gs = jnp.where(hit, NEG_INF, gs)
    cand = jnp.concatenate(
        [jnp.where(gsel[g:g + 1, :], sb[g * per_group:(g + 1) * per_group, :], NEG_INF)
         for g in range(N_EXPERT_GROUPS)], axis=0)

    iota_e = lax.broadcasted_iota(jnp.int32, (n_e, tile), 0)
    member = jnp.zeros((n_e, tile), F32)
    idxs, svals = [], []
    for _ in range(TOP_K):
        _, ei = _first_argmax(cand, iota_e, n_e)
        hit = iota_e == ei
        idxs.append(ei)
        svals.append(jnp.sum(jnp.where(hit, s, 0.0), axis=0, keepdims=True))
        member = jnp.where(hit, 1.0, member)
        cand = jnp.where(hit, NEG_INF, cand)
    sv = jnp.concatenate(svals, axis=0)
    gw_ref[...] = sv / jnp.sum(sv, axis=0, keepdims=True) * ROUTED_SCALE
    eidx_ref[...] = jnp.concatenate(idxs, axis=0)

    r = lax.broadcasted_iota(jnp.int32, (tile, tile), 0)
    c = lax.broadcasted_iota(jnp.int32, (tile, tile), 1)
    earlier = (r < c).astype(BF16)
    before = jnp.dot(member.astype(BF16), earlier, preferred_element_type=F32) + carry_ref[...]
    ranks = [jnp.sum(jnp.where(iota_e == ei, before, 0.0), axis=0, keepdims=True) for ei in idxs]
    rank_ref[...] = jnp.concatenate(ranks, axis=0).astype(jnp.int32)
    carry_ref[...] = carry_ref[...] + jnp.sum(member, axis=1, keepdims=True)
    cnt_ref[...] = carry_ref[...].astype(jnp.int32)


def _router(h, w_router, b_router, tile=512):
    n, d = h.shape
    n_e = w_router.shape[1]
    out_kt = pl.BlockSpec((TOP_K, tile), lambda i: (0, i))
    return pl.pallas_call(
        functools.partial(_router_kernel, tile=tile),
        grid=(n // tile,),
        in_specs=[pl.BlockSpec((tile, d), lambda i: (i, 0)),
                  pl.BlockSpec((n_e, d), lambda i: (0, 0)),
                  pl.BlockSpec((n_e, 1), lambda i: (0, 0))],
        out_specs=[out_kt, out_kt, out_kt, pl.BlockSpec((n_e, 1), lambda i: (0, 0))],
        out_shape=[jax.ShapeDtypeStruct((TOP_K, n), jnp.int32),
                   jax.ShapeDtypeStruct((TOP_K, n), F32),
                   jax.ShapeDtypeStruct((TOP_K, n), jnp.int32),
                   jax.ShapeDtypeStruct((n_e, 1), jnp.int32)],
        scratch_shapes=[pltpu.VMEM((n_e, 1), F32)],
        compiler_params=_params(("arbitrary",)),
        name="router",
    )(h, w_router.T, b_router.reshape(n_e, 1))


def _dest_kernel(eidx_ref, rank_ref, pstart_ref, dest_ref):
    n_e = pstart_ref.shape[0]
    tile = eidx_ref.shape[1]
    iota_e = lax.broadcasted_iota(jnp.int32, (n_e, tile), 0)
    starts = pstart_ref[...]
    rows = [jnp.sum(jnp.where(iota_e == eidx_ref[k:k + 1, :], starts, 0), axis=0, keepdims=True)
            for k in range(TOP_K)]
    dest_ref[0] = jnp.concatenate(rows, axis=0) + rank_ref[...]


def _dest_rows(eidx, rank, pstart):
    n = eidx.shape[1]
    n_e = pstart.shape[0]
    kt = pl.BlockSpec((TOP_K, DEST_TILE), lambda i: (0, i))
    return pl.pallas_call(
        _dest_kernel,
        grid=(n // DEST_TILE,),
        in_specs=[kt, kt, pl.BlockSpec((n_e, 1), lambda i: (0, 0))],
        out_specs=pl.BlockSpec((1, TOP_K, DEST_TILE), lambda i: (i, 0, 0)),
        out_shape=jax.ShapeDtypeStruct((n // DEST_TILE, TOP_K, DEST_TILE), jnp.int32),
        compiler_params=_params(("arbitrary",)),
        name="dest_rows",
    )(eidx, rank, pstart.reshape(n_e, 1))


def _dispatch_kernel(dest_ref, h_ref, xg_hbm, sem):
    tiles = dest_ref.shape[0]
    for sub in range(tiles):
        def group(g, carry, sub=sub):
            off = pl.multiple_of(g * DMA_UNROLL, DMA_UNROLL)
            for u in range(DMA_UNROLL):
                for k in range(TOP_K):
                    pltpu.make_async_copy(h_ref.at[pl.ds(sub * DEST_TILE + off + u, 1)],
                                          xg_hbm.at[pl.ds(dest_ref[sub, g, u * TOP_K + k], 1)],
                                          sem).start(priority=k % DMA_QUEUES)
            return carry
        lax.fori_loop(0, DEST_TILE // DMA_UNROLL, group, 0)
    for _ in range(tiles * TOP_K):
        pltpu.make_async_copy(xg_hbm.at[pl.ds(0, DEST_TILE)], xg_hbm.at[pl.ds(0, DEST_TILE)],
                              sem).wait()


def _dispatch(dest, h, n_rows, tiles=4):
    n, d = h.shape
    rows = tiles * DEST_TILE
    return pl.pallas_call(
        _dispatch_kernel,
        grid=(n // rows,),
        in_specs=[pl.BlockSpec((tiles,) + dest.shape[1:], lambda i: (i, 0, 0),
                               memory_space=pltpu.SMEM),
                  pl.BlockSpec((rows, d), lambda i: (i, 0))],
        out_specs=pl.BlockSpec(memory_space=pl.ANY),
        out_shape=jax.ShapeDtypeStruct((n_rows, d), F32),
        scratch_shapes=[pltpu.SemaphoreType.DMA(())],
        compiler_params=_params(("arbitrary",)),
        name="dispatch_rows",
    )(dest, h)


def _grouped_kernel(ps_ref, cnt_ref, xg_hbm, wgu_ref, wd_ref, y_hbm, xfirst, xbuf, ybuf, wgu_bf,
                    wd_bf, fsem, isem, osem, done_ref, osize_ref):
    e = pl.program_id(0)
    last = pl.num_programs(0) - 1
    n_rows = cnt_ref[e]
    base = ps_ref[e]
    n_chunks = (n_rows + MOE_BLOCK - 1) // MOE_BLOCK
    f = wd_ref.shape[2]
    row = lax.broadcasted_iota(jnp.int32, (MOE_BLOCK, 1), 0)

    def dma_rows(count, c):
        valid = jnp.minimum(count - c * MOE_BLOCK, MOE_BLOCK)
        return pl.multiple_of((valid + SUBLANES - 1) // SUBLANES * SUBLANES, SUBLANES)

    def first_copy(ex):
        size = dma_rows(cnt_ref[ex], 0)
        start = pl.multiple_of(ps_ref[ex], MOE_BLOCK)
        slot = lax.rem(ex, 2)
        return pltpu.make_async_copy(xg_hbm.at[pl.ds(start, size)],
                                     xfirst.at[slot, pl.ds(0, size)], fsem.at[slot])

    def in_copy(c, slot):
        size = dma_rows(n_rows, c)
        start = pl.multiple_of(base + c * MOE_BLOCK, MOE_BLOCK)
        return pltpu.make_async_copy(xg_hbm.at[pl.ds(start, size)],
                                     xbuf.at[slot, pl.ds(0, size)], isem.at[slot])

    def out_copy(start, size, slot):
        return pltpu.make_async_copy(ybuf.at[slot, pl.ds(0, size)],
                                     y_hbm.at[pl.ds(start, size)], osem.at[slot])

    def process(x32, c):
        x = jnp.where(row < n_rows - c * MOE_BLOCK, x32, 0.0).astype(BF16)
        gu = jnp.dot(x, wgu_bf[...], preferred_element_type=F32)
        act = _silu(gu[:, :f]) * gu[:, f:]
        y = jnp.dot(act.astype(BF16), wd_bf[...], preferred_element_type=F32)
        done = done_ref[0]
        slot = lax.rem(done, 2)

        @pl.when(done >= 2)
        def _():
            out_copy(0, pl.multiple_of(osize_ref[slot], SUBLANES), slot).wait()

        ybuf[slot] = y
        size = dma_rows(n_rows, c)
        osize_ref[slot] = size
        out_copy(pl.multiple_of(base + c * MOE_BLOCK, MOE_BLOCK), size, slot).start()
        done_ref[0] = done + 1

    @pl.when(e == 0)
    def _():
        done_ref[0] = 0
        xfirst[...] = jnp.zeros_like(xfirst)
        xbuf[...] = jnp.zeros_like(xbuf)

        @pl.when(n_rows > 0)
        def _():
            first_copy(0).start()

    nxt = jnp.minimum(e + 1, last)

    @pl.when(jnp.logical_and(e < last, cnt_ref[nxt] > 0))
    def _():
        first_copy(nxt).start()

    @pl.when(n_chunks > 0)
    def _():
        wgu_bf[...] = wgu_ref[0, 0].astype(BF16)
        wd_bf[...] = wd_ref[0, 0].astype(BF16)

        @pl.when(n_chunks > 1)
        def _():
            in_copy(1, 1).start()

        first_copy(e).wait()
        process(xfirst[lax.rem(e, 2)], 0)

        def chunk(c, carry):
            slot = lax.rem(c, 2)

            @pl.when(c + 1 < n_chunks)
            def _():
                in_copy(c + 1, 1 - slot).start()

            in_copy(c, slot).wait()
            process(xbuf[slot], c)
            return carry

        lax.fori_loop(1, n_chunks, chunk, 0)

    @pl.when(e == last)
    def _():
        for back in (1, 2):
            @pl.when(done_ref[0] >= back)
            def _():
                slot = lax.rem(done_ref[0] - back, 2)
                out_copy(0, pl.multiple_of(osize_ref[slot], SUBLANES), slot).wait()


def _grouped_experts(xg, pstart, counts, w_gate_up, w_down, layer):
    p, d = xg.shape
    _, n_e, _, f2 = w_gate_up.shape
    f = f2 // 2
    grid_spec = pltpu.PrefetchScalarGridSpec(
        num_scalar_prefetch=2,
        grid=(n_e,),
        in_specs=[
            pl.BlockSpec(memory_space=pl.ANY),
            pl.BlockSpec((1, 1, d, f2), lambda e, ps, cnt: (layer, e, 0, 0)),
            pl.BlockSpec((1, 1, f, d), lambda e, ps, cnt: (layer, e, 0, 0)),
        ],
        out_specs=pl.BlockSpec(memory_space=pl.ANY),
        scratch_shapes=[pltpu.VMEM((2, MOE_BLOCK, d), F32), pltpu.VMEM((2, MOE_BLOCK, d), F32),
                        pltpu.VMEM((2, MOE_BLOCK, d), F32),
                        pltpu.VMEM((d, f2), BF16), pltpu.VMEM((f, d), BF16),
                        pltpu.SemaphoreType.DMA((2,)), pltpu.SemaphoreType.DMA((2,)),
                        pltpu.SemaphoreType.DMA((2,)),
                        pltpu.SMEM((1,), jnp.int32), pltpu.SMEM((2,), jnp.int32)],
    )
    return pl.pallas_call(
        _grouped_kernel,
        grid_spec=grid_spec,
        out_shape=jax.ShapeDtypeStruct((p, d), F32),
        compiler_params=_params(("arbitrary",)),
        name="grouped_experts",
    )(pstart, counts, xg, w_gate_up, w_down)


def _combine_ln_kernel(dpair_ref, dnext_ref, h_ref, gw_ref, y_hbm, wsgu_ref, wsd_ref, g_ref, b_ref,
                       o_ref, ybuf, sem, *, alpha):
    i = pl.program_id(0)
    n_steps = pl.num_programs(0)
    f = wsd_ref.shape[0]

    def start_gather(dref, sub, s):
        def group(g, carry):
            off = pl.multiple_of(g * DMA_UNROLL, DMA_UNROLL)
            for u in range(DMA_UNROLL):
                for k in range(TOP_K):
                    pltpu.make_async_copy(y_hbm.at[pl.ds(dref[sub, g, u * TOP_K + k], 1)],
                                          ybuf.at[s, k, pl.ds(off, DMA_UNROLL)].at[pl.ds(u, 1)],
                                          sem.at[s]).start(priority=k % DMA_QUEUES)
            return carry
        lax.fori_loop(0, DEST_TILE // DMA_UNROLL, group, 0)

    def finish(s):
        rows = slice(s * DEST_TILE, (s + 1) * DEST_TILE)
        h = h_ref[rows, :]
        gu = jnp.dot(h.astype(BF16), wsgu_ref[...], preferred_element_type=F32)
        act = _silu(gu[:, :f]) * gu[:, f:]
        shared = jnp.dot(act.astype(BF16), wsd_ref[...], preferred_element_type=F32)
        for k in range(TOP_K):
            pltpu.make_async_copy(y_hbm.at[pl.ds(0, DEST_TILE)], ybuf.at[s, k], sem.at[s]).wait()
        gw = gw_ref[rows, :]
        routed = gw[:, 0:1] * ybuf[s, 0]
        for k in range(1, TOP_K):
            routed = routed + gw[:, k:k + 1] * ybuf[s, k]
        o_ref[rows, :] = _layer_norm(alpha * h + (routed + shared), g_ref[...], b_ref[...])

    @pl.when(i == 0)
    def _():
        start_gather(dpair_ref, 0, 0)

    start_gather(dpair_ref, 1, 1)
    finish(0)

    @pl.when(i + 1 < n_steps)
    def _():
        start_gather(dnext_ref, 0, 0)

    finish(1)


def _combine_ln(dest, h, gw, y_rows, ws_gate_up, ws_down, g, b, alpha):
    n, d = h.shape
    f2 = ws_gate_up.shape[1]
    nt = n // DEST_TILE
    rows = 2 * DEST_TILE
    vec = pl.BlockSpec((1, d), lambda i: (0, 0))
    return pl.pallas_call(
        functools.partial(_combine_ln_kernel, alpha=alpha),
        grid=(nt // 2,),
        in_specs=[pl.BlockSpec((2,) + dest.shape[1:], lambda i: (i, 0, 0),
                               memory_space=pltpu.SMEM),
                  pl.BlockSpec((1,) + dest.shape[1:],
                               lambda i: (jnp.minimum(2 * i + 2, nt - 1), 0, 0),
                               memory_space=pltpu.SMEM),
                  pl.BlockSpec((rows, d), lambda i: (i, 0)),
                  pl.BlockSpec((rows, TOP_K), lambda i: (i, 0)),
                  pl.BlockSpec(memory_space=pl.ANY),
                  pl.BlockSpec((d, f2), lambda i: (0, 0)),
                  pl.BlockSpec((f2 // 2, d), lambda i: (0, 0)), vec, vec],
        out_specs=pl.BlockSpec((rows, d), lambda i: (i, 0)),
        out_shape=jax.ShapeDtypeStruct((n, d), F32),
        scratch_shapes=[pltpu.VMEM((2, TOP_K, DEST_TILE, d), F32),
                        pltpu.SemaphoreType.DMA((2,))],
        compiler_params=_params(("arbitrary",)),
        name="combine_ln",
    )(dest, dest, h, gw, y_rows, ws_gate_up.astype(BF16), ws_down.astype(BF16),
      g.reshape(1, d), b.reshape(1, d))


def _moe_ln(h, w_router, b_router, w_gate_up, w_down, layer, ws_gate_up, ws_down, g, b, alpha):
    n, d = h.shape
    n_e = w_router.shape[1]
    eidx, gw, rank, counts = _router(h, w_router, b_router)

    nb = -(-(n * TOP_K + n_e * (MOE_BLOCK - 1)) // MOE_BLOCK)
    counts = counts[:, 0]
    padded = (counts + MOE_BLOCK - 1) // MOE_BLOCK * MOE_BLOCK
    pstart = (jnp.cumsum(padded) - padded).astype(jnp.int32)

    dest = _dest_rows(eidx, rank, pstart)
    dest = dest.reshape(-1, TOP_K, DEST_TILE // DMA_UNROLL, DMA_UNROLL)
    dest = dest.transpose(0, 2, 3, 1).reshape(-1, DEST_TILE // DMA_UNROLL, DMA_UNROLL * TOP_K)
    xg = _dispatch(dest, h, nb * MOE_BLOCK)
    y_rows = _grouped_experts(xg, pstart, counts, w_gate_up, w_down, layer)
    return _combine_ln(dest, h, gw.T, y_rows, ws_gate_up, ws_down, g, b, alpha)


F_PIECES = 3
ONES_ROWS = 16


def _gate_placement(n_heads):
    import numpy as np
    n_pairs = n_heads // 2
    pk = np.zeros((F_PIECES, LANES, n_pairs * LANES), np.float32)
    ones_k = np.zeros((1, n_pairs * LANES), np.float32)
    pq = np.zeros((F_PIECES, LANES, n_pairs * LANES), np.float32)
    ones_q = np.zeros((1, n_pairs * LANES), np.float32)
    for h in range(n_heads):
        p, e = divmod(h, 2)
        for j in range(F_PIECES):
            pk[j, h, p * LANES + F_PIECES * e + j] = -1.0
            ones_q[0, p * LANES + F_PIECES * e + j] = 1.0
            pq[j, h, p * LANES + F_PIECES * (2 + e) + j] = 1.0
            ones_k[0, p * LANES + F_PIECES * (2 + e) + j] = 1.0
    return (jnp.asarray(pk, BF16), jnp.asarray(ones_k), jnp.asarray(pq, BF16), jnp.asarray(ones_q))


def _kvf_kernel(h_ref, wk_ref, wvt_ref, wf_ref, bf_ref, pk_ref, onesk_ref, k_ref, vt_ref, fp_ref,
                carry_ref, *, tile):
    s = pl.program_id(1)
    n_heads = vt_ref.shape[1]
    n_pairs = n_heads // 2

    @pl.when(s == 0)
    def _():
        carry_ref[...] = jnp.zeros_like(carry_ref)

    h = h_ref[0]
    hb = h.astype(BF16)
    h_lo = (h - hb.astype(F32)).astype(BF16)
    wf = wf_ref[...]
    wf_hi = wf.astype(BF16)
    wf_lo = (wf - wf_hi.astype(F32)).astype(BF16)
    z = (jnp.dot(hb, wf_hi, preferred_element_type=F32)
         + (jnp.dot(hb, wf_lo, preferred_element_type=F32)
            + jnp.dot(h_lo, wf_hi, preferred_element_type=F32))) + bf_ref[...]
    logf = jnp.minimum(z, 0.0) - jnp.log1p(jnp.exp(-jnp.abs(z)))
    row = lax.broadcasted_iota(jnp.int32, (tile, 1), 0)
    span = 1
    while span < tile:
        logf = logf + jnp.where(row >= span, pltpu.roll(logf, span, axis=0), 0.0)
        span *= 2
    cum = logf + carry_ref[...]
    carry_ref[...] = cum[tile - 1:tile, :]

    pieces = []
    rest = cum
    for _ in range(F_PIECES):
        piece = rest.astype(BF16)
        pieces.append(piece)
        rest = rest - piece.astype(F32)
    fp_ref[0] = jnp.concatenate(pieces, axis=1)
    kf = onesk_ref[...]
    for j in range(F_PIECES):
        kf = kf + jnp.dot(pieces[j], pk_ref[j], preferred_element_type=F32)

    k = jnp.dot(hb, wk_ref[...], preferred_element_type=F32)
    parts = []
    for p in range(n_pairs):
        parts += [k[:, p * LANES:(p + 1) * LANES], kf[:, p * LANES:(p + 1) * LANES]]
    k_ref[0] = jnp.concatenate(parts, axis=1).astype(BF16)

    vt = lax.dot_general(wvt_ref[...], hb, (((1,), (1,)), ((), ())),
                         preferred_element_type=F32).astype(BF16)
    for hd in range(n_heads):
        vt_ref[0, hd, 0:HEAD_DIM, :] = vt[hd * HEAD_DIM:(hd + 1) * HEAD_DIM, :]
        vt_ref[0, hd, HEAD_DIM:HEAD_DIM + ONES_ROWS, :] = jnp.ones((ONES_ROWS, tile), BF16)


def _kvf(h3, w_kvf, b_f, pk, ones_k, tile=512):
    bsz, seq, d = h3.shape
    n_h = b_f.shape[0]
    wk = w_kvf[:, :d].astype(BF16)
    wvt = w_kvf[:, d:2 * d].T.astype(BF16)
    wf = jnp.pad(w_kvf[:, 2 * d:], ((0, 0), (0, LANES - n_h)))
    bf = jnp.pad(b_f, (0, LANES - n_h)).reshape(1, LANES)
    const2 = lambda i, j: (0, 0)
    return pl.pallas_call(
        functools.partial(_kvf_kernel, tile=tile),
        grid=(bsz, seq // tile),
        in_specs=[pl.BlockSpec((1, tile, d), lambda i, j: (i, j, 0)),
                  pl.BlockSpec((d, d), const2), pl.BlockSpec((d, d), const2),
                  pl.BlockSpec((d, LANES), const2), pl.BlockSpec((1, LANES), const2),
                  pl.BlockSpec(pk.shape, lambda i, j: (0, 0, 0)),
                  pl.BlockSpec(ones_k.shape, const2)],
        out_specs=[pl.BlockSpec((1, tile, 2 * d), lambda i, j: (i, j, 0)),
                   pl.BlockSpec((1, n_h, HEAD_DIM + ONES_ROWS, tile), lambda i, j: (i, 0, 0, j)),
                   pl.BlockSpec((1, tile, F_PIECES * LANES), lambda i, j: (i, j, 0))],
        out_shape=[jax.ShapeDtypeStruct((bsz, seq, 2 * d), BF16),
                   jax.ShapeDtypeStruct((bsz, n_h, HEAD_DIM + ONES_ROWS, seq), BF16),
                   jax.ShapeDtypeStruct((bsz, seq, F_PIECES * LANES), BF16)],
        scratch_shapes=[pltpu.VMEM((1, LANES), F32)],
        compiler_params=_params(("arbitrary", "arbitrary")),
        name="kvf_proj",
    )(h3, wk, wvt, wf, bf, pk, ones_k)


def _qproj_kernel(h_ref, wq_ref, fp_ref, pq_ref, onesq_ref, q_ref, *, scale):
    n_pairs = q_ref.shape[1] // (2 * LANES)
    q = jnp.dot(h_ref[...].astype(BF16), wq_ref[...], preferred_element_type=F32) * scale
    qf = onesq_ref[...]
    for j in range(F_PIECES):
        qf = qf + jnp.dot(fp_ref[:, j * LANES:(j + 1) * LANES], pq_ref[j],
                          preferred_element_type=F32)
    parts = []
    for p in range(n_pairs):
        parts += [q[:, p * LANES:(p + 1) * LANES], qf[:, p * LANES:(p + 1) * LANES]]
    q_ref[...] = jnp.concatenate(parts, axis=1).astype(BF16)


def _qproj(h, w_q, fpieces, pq, ones_q, scale, tile=512):
    n, d = h.shape
    n_h = d // HEAD_DIM
    const2 = lambda i: (0, 0)
    return pl.pallas_call(
        functools.partial(_qproj_kernel, scale=scale),
        grid=(n // tile,),
        in_specs=[pl.BlockSpec((tile, d), lambda i: (i, 0)),
                  pl.BlockSpec((d, d), const2),
                  pl.BlockSpec((tile, F_PIECES * LANES), lambda i: (i, 0)),
                  pl.BlockSpec(pq.shape, lambda i: (0, 0, 0)),
                  pl.BlockSpec(ones_q.shape, const2)],
        out_specs=pl.BlockSpec((tile, n_h * LANES), lambda i: (i, 0)),
        out_shape=jax.ShapeDtypeStruct((n, n_h * LANES), BF16),
        compiler_params=_params(("arbitrary",)),
        name="q_proj",
    )(h, w_q.astype(BF16), fpieces, pq, ones_q)


def _attn_kernel(q_ref, k_ref, vt_ref, o_ref, sa_ref, sb_ref, *, tq, tk):
    i = pl.program_id(2)
    per_q = tq // tk
    qa = q_ref[0]
    lane = lax.broadcasted_iota(jnp.int32, (1, 2 * LANES), 1)

    def lanes_in(start, size):
        return jnp.logical_and(lane >= start, lane < start + size)

    qs = []
    for hd in range(2):
        own = jnp.logical_or(
            lanes_in(hd * HEAD_DIM, HEAD_DIM),
            jnp.logical_or(lanes_in(LANES + hd * F_PIECES, F_PIECES),
                           lanes_in(LANES + (2 + hd) * F_PIECES, F_PIECES)))
        qs.append(jnp.where(own, qa, jnp.zeros_like(qa)))
    row = lax.broadcasted_iota(jnp.int32, (tk, tq), 0)
    col = lax.broadcasted_iota(jnp.int32, (tk, tq), 1)

    def logits(j, s_ref):
        kj = k_ref[0, pl.ds(pl.multiple_of(j * tk, tk), tk), :]
        for hd in range(2):
            s_ref[hd] = lax.dot_general(kj, qs[hd], (((1,), (1,)), ((), ())),
                                        preferred_element_type=F32)

    def softmax_pv(j, s_ref, carry, diag_offset):
        ms, ls, accs = carry
        keys = pl.ds(pl.multiple_of(j * tk, tk), tk)
        new_ms, new_ls, new_accs = [], [], []
        for hd in range(2):
            sc = s_ref[hd]
            if diag_offset is not None:
                sc = jnp.where(row + diag_offset <= col, sc, NEG_INF)
            m_new = jnp.maximum(ms[hd], jnp.max(sc, axis=0, keepdims=True))
            pt = jnp.exp(sc - m_new).astype(BF16)
            alpha = jnp.exp(ms[hd] - m_new)
            r = jnp.dot(vt_ref[0, hd, :, keys], pt,
                        preferred_element_type=F32)
            new_ms.append(m_new)
            new_ls.append(alpha * ls[hd] + r[HEAD_DIM:HEAD_DIM + 1, :])
            new_accs.append(alpha * accs[hd] + r[:HEAD_DIM, :])
        return tuple(new_ms), tuple(new_ls), tuple(new_accs)

    def two_blocks(t, carry):
        j = t * 2
        logits(j + 1, sb_ref)
        carry = softmax_pv(j, sa_ref, carry, None)
        logits(j + 2, sa_ref)
        return softmax_pv(j + 1, sb_ref, carry, None)

    init = ((jnp.full((1, tq), NEG_INF, F32),) * 2, (jnp.zeros((1, tq), F32),) * 2,
            (jnp.zeros((HEAD_DIM, tq), F32),) * 2)
    logits(0, sa_ref)
    first_diag = i * per_q
    carry = lax.fori_loop(0, first_diag // 2, two_blocks, init)
    bufs = (sa_ref, sb_ref)
    for u in range(per_q):
        if u + 1 < per_q:
            logits(first_diag + u + 1, bufs[(u + 1) % 2])
        carry = softmax_pv(first_diag + u, bufs[u % 2], carry, u * tk)
    _, ls, accs = carry
    out_t = jnp.concatenate([accs[0] / ls[0], accs[1] / ls[1]], axis=0)
    o_ref[0] = out_t.T.astype(BF16)


def _attention(qaug, kaug, vaug_t, d, tq=512, tk=256):
    bsz, seq, _ = kaug.shape
    n_pairs = vaug_t.shape[1] // 2
    assert (tq // tk) % 2 == 0
    return pl.pallas_call(
        functools.partial(_attn_kernel, tq=tq, tk=tk),
        grid=(bsz, n_pairs, seq // tq),
        in_specs=[pl.BlockSpec((1, tq, 2 * LANES), lambda b, p, i: (b, i, p)),
                  pl.BlockSpec((1, seq, 2 * LANES), lambda b, p, i: (b, 0, p)),
                  pl.BlockSpec((1, 2, HEAD_DIM + ONES_ROWS, seq), lambda b, p, i: (b, p, 0, 0))],
        out_specs=pl.BlockSpec((1, tq, LANES), lambda b, p, i: (b, i, p)),
        out_shape=jax.ShapeDtypeStruct((bsz, seq, d), BF16),
        scratch_shapes=[pltpu.VMEM((2, tk, tq), F32), pltpu.VMEM((2, tk, tq), F32)],
        compiler_params=_params(("arbitrary", "arbitrary", "arbitrary")),
        name="fox_attention",
    )(qaug, kaug, vaug_t)


def _oproj_ln_kernel(o_ref, h_ref, wo_ref, g_ref, b_ref, out_ref, *, alpha):
    mix = jnp.dot(o_ref[...], wo_ref[...], preferred_element_type=F32)
    out_ref[...] = _layer_norm(alpha * h_ref[...] + mix, g_ref[...], b_ref[...])


def _oproj_ln(o, h, w_o, g, b, alpha, tile=512):
    n, d = h.shape
    vec = pl.BlockSpec((1, d), lambda i: (0, 0))
    tspec = pl.BlockSpec((tile, d), lambda i: (i, 0))
    return pl.pallas_call(
        functools.partial(_oproj_ln_kernel, alpha=alpha),
        grid=(n // tile,),
        in_specs=[tspec, tspec, pl.BlockSpec((d, d), lambda i: (0, 0)), vec, vec],
        out_specs=tspec,
        out_shape=jax.ShapeDtypeStruct((n, d), F32),
        compiler_params=_params(("arbitrary",)),
        name="oproj_ln",
    )(o, h, w_o.astype(BF16), g.reshape(1, d), b.reshape(1, d))


def kernel(x, pool_w, pool_scale, w_q, w_o, w_kvf, b_f, ln_g, ln_b, router_w, router_b,
           w_gate_up, w_down, ws_gate_up, ws_down):
    bsz, seq, d = x.shape
    depth = ln_g.shape[0]
    n_pool = pool_w.shape[0]
    alpha = float((2 * depth) ** 0.25)
    n = bsz * seq
    kaug = vaug_t = fpieces = None
    pk, ones_k, pq, ones_q = _gate_placement(d // HEAD_DIM)
    for l in range(depth):
        if l < n_pool:
            x = _pool_ln(x, pool_w[l], pool_scale[l], ln_g[l, 0], ln_b[l, 0], alpha)
        else:
            if l == n_pool:
                kaug, vaug_t, fpieces = _kvf(x, w_kvf, b_f, pk, ones_k)
            j = l - n_pool
            qaug = _qproj(x.reshape(n, d), w_q[j], fpieces.reshape(n, -1), pq, ones_q,
                          HEAD_DIM ** -0.5)
            o = _attention(qaug.reshape(bsz, seq, -1), kaug, vaug_t, d)
            x = _oproj_ln(o.reshape(n, d), x.reshape(n, d), w_o[j], ln_g[l, 0], ln_b[l, 0],
                          alpha).reshape(bsz, seq, d)
        x = _moe_ln(x.reshape(n, d), router_w[l], router_b[l], w_gate_up, w_down, l,
                    ws_gate_up[l], ws_down[l], ln_g[l, 1], ln_b[l, 1], alpha).reshape(bsz, seq, d)
    return x
```

```python
import functools

import jax
import jax.numpy as jnp
from jax import lax
from jax.experimental import pallas as pl
from jax.experimental.pallas import tpu as pltpu

POOL_WINDOWS = (2, 4, 8, 16)
POOL_HALO = 16
HEAD_DIM = 64
N_EXPERT_GROUPS = 8
TOPK_GROUPS = 4
TOP_K = 8
ROUTED_SCALE = 2.5
MOE_BLOCK = 1024
SUB_ROWS = 256
DMA_UNROLL = 8
DMA_QUEUES = 2
DEST_TILE = 256
LN_EPS = 1e-5

LANES = 128
SUBLANES = 8
VMEM_LIMIT = 48 * 1024 * 1024

F32 = jnp.float32
BF16 = jnp.bfloat16
NEG_INF = float("-inf")


def _params(semantics):
    return pltpu.CompilerParams(dimension_semantics=semantics, vmem_limit_bytes=VMEM_LIMIT)


def _layer_norm(z, g, b):
    mu = jnp.mean(z, axis=-1, keepdims=True)
    zc = z - mu
    var = jnp.mean(zc * zc, axis=-1, keepdims=True)
    return zc * lax.rsqrt(var + LN_EPS) * g + b


def _silu(x):
    return x * jax.nn.sigmoid(x)


def _pool_ln_kernel(x_ref, w_ref, scale_ref, g_ref, b_ref, o_ref, ext_ref, *, alpha, tile):
    s = pl.program_id(1)
    d = x_ref.shape[2]
    c = d // len(POOL_WINDOWS)

    @pl.when(s == 0)
    def _():
        ext_ref[0:POOL_HALO, :] = jnp.zeros((POOL_HALO, d), F32)

    x = x_ref[0]
    ext_ref[POOL_HALO:POOL_HALO + tile, :] = x
    pos = (s * tile + 1 + lax.broadcasted_iota(jnp.int32, (tile, 1), 0)).astype(F32)
    ys = []
    for gi, w in enumerate(POOL_WINDOWS):
        cs = slice(gi * c, (gi + 1) * c)
        acc = ext_ref[:, cs]
        span = 1
        while span < w:
            acc = acc + pltpu.roll(acc, span, axis=0)
            span *= 2
        mean = acc[POOL_HALO:, :] / jnp.minimum(pos, float(w))
        mixed = mean - x[:, cs]
        ys.append(jnp.dot(mixed.astype(BF16), w_ref[gi], preferred_element_type=F32))
    y = jnp.concatenate(ys, axis=1) * scale_ref[...]
    o_ref[0] = _layer_norm(alpha * x + y, g_ref[...], b_ref[...])
    ext_ref[0:POOL_HALO, :] = x[tile - POOL_HALO:, :]


def _pool_ln(x, w, scale, g, b, alpha, tile=512):
    bsz, seq, d = x.shape
    ng, c, _ = w.shape
    vec = pl.BlockSpec((1, d), lambda i, j: (0, 0))
    return pl.pallas_call(
        functools.partial(_pool_ln_kernel, alpha=alpha, tile=tile),
        grid=(bsz, seq // tile),
        in_specs=[pl.BlockSpec((1, tile, d), lambda i, j: (i, j, 0)),
                  pl.BlockSpec((ng, c, c), lambda i, j: (0, 0, 0)),
                  vec, vec, vec],
        out_specs=pl.BlockSpec((1, tile, d), lambda i, j: (i, j, 0)),
        out_shape=jax.ShapeDtypeStruct(x.shape, F32),
        scratch_shapes=[pltpu.VMEM((POOL_HALO + tile, d), F32)],
        compiler_params=_params(("arbitrary", "arbitrary")),
        name="pool_ln",
    )(x, w.astype(BF16), scale.reshape(1, d), g.reshape(1, d), b.reshape(1, d))


def _first_argmax(v, iota, size):
    m = jnp.max(v, axis=0, keepdims=True)
    idx = jnp.min(jnp.where(v == m, iota, size), axis=0, keepdims=True)
    return m, idx


def _router_kernel(h_ref, wt_ref, b_ref, eidx_ref, gw_ref, rank_ref, cnt_ref, carry_ref, *, tile):
    i = pl.program_id(0)
    n_e = wt_ref.shape[0]
    per_group = n_e // N_EXPERT_GROUPS

    @pl.when(i == 0)
    def _():
        carry_ref[...] = jnp.zeros_like(carry_ref)

    h = h_ref[...]
    h_hi = h.astype(BF16)
    h_lo = (h - h_hi.astype(F32)).astype(BF16)
    wt = wt_ref[...]
    wt_hi = wt.astype(BF16)
    wt_lo = (wt - wt_hi.astype(F32)).astype(BF16)
    nt_dims = (((1,), (1,)), ((), ()))
    logits = (lax.dot_general(wt_hi, h_hi, nt_dims, preferred_element_type=F32)
              + (lax.dot_general(wt_lo, h_hi, nt_dims, preferred_element_type=F32)
                 + lax.dot_general(wt_hi, h_lo, nt_dims, preferred_element_type=F32)))
    s = jax.nn.sigmoid(logits)
    sb = s + b_ref[...]

    iota_g = lax.broadcasted_iota(jnp.int32, (per_group, tile), 0)
    gscores = []
    for g in range(N_EXPERT_GROUPS):
        blk = sb[g * per_group:(g + 1) * per_group, :]
        m1, i1 = _first_argmax(blk, iota_g, per_group)
        m2 = jnp.max(jnp.where(iota_g == i1, NEG_INF, blk), axis=0, keepdims=True)
        gscores.append(m1 + m2)
    gs = jnp.concatenate(gscores, axis=0)
    iota_8 = lax.broadcasted_iota(jnp.int32, (N_EXPERT_GROUPS, tile), 0)
    gsel = jnp.zeros((N_EXPERT_GROUPS, tile), jnp.bool_)
    for _ in range(TOPK_GROUPS):
        _, gi = _first_argmax(gs, iota_8, N_EXPERT_GROUPS)
        hit = iota_8 == gi
        gsel = jnp.logical_or(gsel, hit)
        gs = jnp.where(hit, NEG_INF, gs)
    cand = jnp.concatenate(
        [jnp.where(gsel[g:g + 1, :], sb[g * per_group:(g + 1) * per_group, :], NEG_INF)
         for g in range(N_EXPERT_GROUPS)], axis=0)

    iota_e = lax.broadcasted_iota(jnp.int32, (n_e, tile), 0)
    member = jnp.zeros((n_e, tile), F32)
    idxs, svals = [], []
    for _ in range(TOP_K):
        _, ei = _first_argmax(cand, iota_e, n_e)
        hit = iota_e == ei
        idxs.append(ei)
        svals.append(jnp.sum(jnp.where(hit, s, 0.0), axis=0, keepdims=True))
        member = jnp.where(hit, 1.0, member)
        cand = jnp.where(hit, NEG_INF, cand)
    sv = jnp.concatenate(svals, axis=0)
    gw_ref[...] = sv / jnp.sum(sv, axis=0, keepdims=True) * ROUTED_SCALE
    eidx_ref[...] = jnp.concatenate(idxs, axis=0)

    r = lax.broadcasted_iota(jnp.int32, (tile, tile), 0)
    c = lax.broadcasted_iota(jnp.int32, (tile, tile), 1)
    earlier = (r < c).astype(BF16)
    before = jnp.dot(member.astype(BF16), earlier, preferred_element_type=F32) + carry_ref[...]
    ranks = [jnp.sum(jnp.where(iota_e == ei, before, 0.0), axis=0, keepdims=True) for ei in idxs]
    rank_ref[...] = jnp.concatenate(ranks, axis=0).astype(jnp.int32)
    carry_ref[...] = carry_ref[...] + jnp.sum(member, axis=1, keepdims=True)
    cnt_ref[...] = carry_ref[...].astype(jnp.int32)


def _router(h, w_router, b_router, tile=512):
    n, d = h.shape
    n_e = w_router.shape[1]
    out_kt = pl.BlockSpec((TOP_K, tile), lambda i: (0, i))
    return pl.pallas_call(
        functools.partial(_router_kernel, tile=tile),
        grid=(n // tile,),
        in_specs=[pl.BlockSpec((tile, d), lambda i: (i, 0)),
                  pl.BlockSpec((n_e, d), lambda i: (0, 0)),
                  pl.BlockSpec((n_e, 1), lambda i: (0, 0))],
        out_specs=[out_kt, out_kt, out_kt, pl.BlockSpec((n_e, 1), lambda i: (0, 0))],
        out_shape=[jax.ShapeDtypeStruct((TOP_K, n), jnp.int32),
                   jax.ShapeDtypeStruct((TOP_K, n), F32),
                   jax.ShapeDtypeStruct((TOP_K, n), jnp.int32),
                   jax.ShapeDtypeStruct((n_e, 1), jnp.int32)],
        scratch_shapes=[pltpu.VMEM((n_e, 1), F32)],
        compiler_params=_params(("arbitrary",)),
        name="router",
    )(h, w_router.T, b_router.reshape(n_e, 1))


def _dest_kernel(eidx_ref, rank_ref, pstart_ref, dest_ref):
    n_e = pstart_ref.shape[0]
    tile = eidx_ref.shape[1]
    iota_e = lax.broadcasted_iota(jnp.int32, (n_e, tile), 0)
    starts = pstart_ref[...]
    rows = [jnp.sum(jnp.where(iota_e == eidx_ref[k:k + 1, :], starts, 0), axis=0, keepdims=True)
            for k in range(TOP_K)]
    dest_ref[0] = jnp.concatenate(rows, axis=0) + rank_ref[...]


def _dest_rows(eidx, rank, pstart):
    n = eidx.shape[1]
    n_e = pstart.shape[0]
    kt = pl.BlockSpec((TOP_K, DEST_TILE), lambda i: (0, i))
    return pl.pallas_call(
        _dest_kernel,
        grid=(n // DEST_TILE,),
        in_specs=[kt, kt, pl.BlockSpec((n_e, 1), lambda i: (0, 0))],
        out_specs=pl.BlockSpec((1, TOP_K, DEST_TILE), lambda i: (i, 0, 0)),
        out_shape=jax.ShapeDtypeStruct((n // DEST_TILE, TOP_K, DEST_TILE), jnp.int32),
        compiler_params=_params(("arbitrary",)),
        name="dest_rows",
    )(eidx, rank, pstart.reshape(n_e, 1))


def _dispatch_kernel(dest_ref, h_ref, xg_hbm, sem):
    tiles = dest_ref.shape[0]
    for sub in range(tiles):
        def group(g, carry, sub=sub):
            off = pl.multiple_of(g * DMA_UNROLL, DMA_UNROLL)
            for u in range(DMA_UNROLL):
                for k in range(TOP_K):
                    pltpu.make_async_copy(h_ref.at[pl.ds(sub * DEST_TILE + off + u, 1)],
                                          xg_hbm.at[pl.ds(dest_ref[sub, g, u * TOP_K + k], 1)],
                                          sem).start(priority=k % DMA_QUEUES)
            return carry
        lax.fori_loop(0, DEST_TILE // DMA_UNROLL, group, 0)
    for _ in range(tiles * TOP_K):
        pltpu.make_async_copy(xg_hbm.at[pl.ds(0, DEST_TILE)], xg_hbm.at[pl.ds(0, DEST_TILE)],
                              sem).wait()


def _dispatch(dest, h, n_rows, tiles=4):
    n, d = h.shape
    rows = tiles * DEST_TILE
    return pl.pallas_call(
        _dispatch_kernel,
        grid=(n // rows,),
        in_specs=[pl.BlockSpec((tiles,) + dest.shape[1:], lambda i: (i, 0, 0),
                               memory_space=pltpu.SMEM),
                  pl.BlockSpec((rows, d), lambda i: (i, 0))],
        out_specs=pl.BlockSpec(memory_space=pl.ANY),
        out_shape=jax.ShapeDtypeStruct((n_rows, d), F32),
        scratch_shapes=[pltpu.SemaphoreType.DMA(())],
        compiler_params=_params(("arbitrary",)),
        name="dispatch_rows",
    )(dest, h)


def _grouped_kernel(ps_ref, cnt_ref, xg_hbm, wgu_ref, wd_ref, y_hbm, xfirst, xbuf, ybuf, wgu_bf,
                    wd_bf, fsem, isem, osem, done_ref, osize_ref):
    e = pl.program_id(0)
    last = pl.num_programs(0) - 1
    n_rows = cnt_ref[e]
    base = ps_ref[e]
    n_chunks = (n_rows + MOE_BLOCK - 1) // MOE_BLOCK
    f = wd_ref.shape[2]
    row = lax.broadcasted_iota(jnp.int32, (SUB_ROWS, 1), 0)

    def dma_rows(count, c):
        valid = jnp.minimum(count - c * MOE_BLOCK, MOE_BLOCK)
        return pl.multiple_of((valid + SUBLANES - 1) // SUBLANES * SUBLANES, SUBLANES)

    def first_copy(ex):
        size = dma_rows(cnt_ref[ex], 0)
        start = pl.multiple_of(ps_ref[ex], SUBLANES)
        slot = lax.rem(ex, 2)
        return pltpu.make_async_copy(xg_hbm.at[pl.ds(start, size)],
                                     xfirst.at[slot, pl.ds(0, size)], fsem.at[slot])

    def in_copy(c, slot):
        size = dma_rows(n_rows, c)
        start = pl.multiple_of(base + c * MOE_BLOCK, SUBLANES)
        return pltpu.make_async_copy(xg_hbm.at[pl.ds(start, size)],
                                     xbuf.at[slot, pl.ds(0, size)], isem.at[slot])

    def out_copy(start, size, slot):
        return pltpu.make_async_copy(ybuf.at[slot, pl.ds(0, size)],
                                     y_hbm.at[pl.ds(start, size)], osem.at[slot])

    def process(x_ref, x_slot, c):
        valid = jnp.minimum(n_rows - c * MOE_BLOCK, MOE_BLOCK)
        done = done_ref[0]
        slot = lax.rem(done, 2)

        @pl.when(done >= 2)
        def _():
            out_copy(0, pl.multiple_of(osize_ref[slot], SUBLANES), slot).wait()

        def sub_block(j, carry):
            r0 = pl.multiple_of(j * SUB_ROWS, SUB_ROWS)
            x = jnp.where(row + r0 < valid, x_ref[x_slot, pl.ds(r0, SUB_ROWS), :], 0.0).astype(BF16)
            gu = jnp.dot(x, wgu_bf[...], preferred_element_type=F32)
            act = _silu(gu[:, :f]) * gu[:, f:]
            ybuf[slot, pl.ds(r0, SUB_ROWS), :] = jnp.dot(act.astype(BF16), wd_bf[...],
                                                         preferred_element_type=F32)
            return carry

        lax.fori_loop(0, (valid + SUB_ROWS - 1) // SUB_ROWS, sub_block, 0)
        size = dma_rows(n_rows, c)
        osize_ref[slot] = size
        out_copy(pl.multiple_of(base + c * MOE_BLOCK, SUBLANES), size, slot).start()
        done_ref[0] = done + 1

    @pl.when(e == 0)
    def _():
        done_ref[0] = 0
        xfirst[...] = jnp.zeros_like(xfirst)
        xbuf[...] = jnp.zeros_like(xbuf)

        @pl.when(n_rows > 0)
        def _():
            first_copy(0).start()

    nxt = jnp.minimum(e + 1, last)

    @pl.when(jnp.logical_and(e < last, cnt_ref[nxt] > 0))
    def _():
        first_copy(nxt).start()

    @pl.when(n_chunks > 0)
    def _():
        wgu_bf[...] = wgu_ref[0, 0].astype(BF16)
        wd_bf[...] = wd_ref[0, 0].astype(BF16)

        @pl.when(n_chunks > 1)
        def _():
            in_copy(1, 1).start()

        first_copy(e).wait()
        process(xfirst, lax.rem(e, 2), 0)

        def chunk(c, carry):
            slot = lax.rem(c, 2)

            @pl.when(c + 1 < n_chunks)
            def _():
                in_copy(c + 1, 1 - slot).start()

            in_copy(c, slot).wait()
            process(xbuf, slot, c)
            return carry

        lax.fori_loop(1, n_chunks, chunk, 0)

    @pl.when(e == last)
    def _():
        for back in (1, 2):
            @pl.when(done_ref[0] >= back)
            def _():
                slot = lax.rem(done_ref[0] - back, 2)
                out_copy(0, pl.multiple_of(osize_ref[slot], SUBLANES), slot).wait()


def _grouped_experts(xg, pstart, counts, w_gate_up, w_down, layer):
    p, d = xg.shape
    _, n_e, _, f2 = w_gate_up.shape
    f = f2 // 2
    grid_spec = pltpu.PrefetchScalarGridSpec(
        num_scalar_prefetch=2,
        grid=(n_e,),
        in_specs=[
            pl.BlockSpec(memory_space=pl.ANY),
            pl.BlockSpec((1, 1, d, f2), lambda e, ps, cnt: (layer, e, 0, 0)),
            pl.BlockSpec((1, 1, f, d), lambda e, ps, cnt: (layer, e, 0, 0)),
        ],
        out_specs=pl.BlockSpec(memory_space=pl.ANY),
        scratch_shapes=[pltpu.VMEM((2, MOE_BLOCK, d), F32), pltpu.VMEM((2, MOE_BLOCK, d), F32),
                        pltpu.VMEM((2, MOE_BLOCK, d), F32),
                        pltpu.VMEM((d, f2), BF16), pltpu.VMEM((f, d), BF16),
                        pltpu.SemaphoreType.DMA((2,)), pltpu.SemaphoreType.DMA((2,)),
                        pltpu.SemaphoreType.DMA((2,)),
                        pltpu.SMEM((1,), jnp.int32), pltpu.SMEM((2,), jnp.int32)],
    )
    return pl.pallas_call(
        _grouped_kernel,
        grid_spec=grid_spec,
        out_shape=jax.ShapeDtypeStruct((p, d), F32),
        compiler_params=_params(("arbitrary",)),
        name="grouped_experts",
    )(pstart, counts, xg, w_gate_up, w_down)


def _combine_ln_kernel(dpair_ref, dnext_ref, h_ref, gw_ref, y_hbm, wsgu_ref, wsd_ref, g_ref, b_ref,
                       o_ref, ybuf, sem, *, alpha):
    i = pl.program_id(0)
    n_steps = pl.num_programs(0)
    f = wsd_ref.shape[0]

    def start_gather(dref, sub, s):
        def group(g, carry):
            off = pl.multiple_of(g * DMA_UNROLL, DMA_UNROLL)
            for u in range(DMA_UNROLL):
                for k in range(TOP_K):
                    pltpu.make_async_copy(y_hbm.at[pl.ds(dref[sub, g, u * TOP_K + k], 1)],
                                          ybuf.at[s, k, pl.ds(off, DMA_UNROLL)].at[pl.ds(u, 1)],
                                          sem.at[s]).start(priority=k % DMA_QUEUES)
            return carry
        lax.fori_loop(0, DEST_TILE // DMA_UNROLL, group, 0)

    def finish(s):
        rows = slice(s * DEST_TILE, (s + 1) * DEST_TILE)
        h = h_ref[rows, :]
        gu = jnp.dot(h.astype(BF16), wsgu_ref[...], preferred_element_type=F32)
        act = _silu(gu[:, :f]) * gu[:, f:]
        shared = jnp.dot(act.astype(BF16), wsd_ref[...], preferred_element_type=F32)
        for k in range(TOP_K):
            pltpu.make_async_copy(y_hbm.at[pl.ds(0, DEST_TILE)], ybuf.at[s, k], sem.at[s]).wait()
        gw = gw_ref[rows, :]
        routed = gw[:, 0:1] * ybuf[s, 0]
        for k in range(1, TOP_K):
            routed = routed + gw[:, k:k + 1] * ybuf[s, k]
        o_ref[rows, :] = _layer_norm(alpha * h + (routed + shared), g_ref[...], b_ref[...])

    @pl.when(i == 0)
    def _():
        start_gather(dpair_ref, 0, 0)

    start_gather(dpair_ref, 1, 1)
    finish(0)

    @pl.when(i + 1 < n_steps)
    def _():
        start_gather(dnext_ref, 0, 0)

    finish(1)


def _combine_ln(dest, h, gw, y_rows, ws_gate_up, ws_down, g, b, alpha):
    n, d = h.shape
    f2 = ws_gate_up.shape[1]
    nt = n // DEST_TILE
    rows = 2 * DEST_TILE
    vec = pl.BlockSpec((1, d), lambda i: (0, 0))
    return pl.pallas_call(
        functools.partial(_combine_ln_kernel, alpha=alpha),
        grid=(nt // 2,),
        in_specs=[pl.BlockSpec((2,) + dest.shape[1:], lambda i: (i, 0, 0),
                               memory_space=pltpu.SMEM),
                  pl.BlockSpec((1,) + dest.shape[1:],
                               lambda i: (jnp.minimum(2 * i + 2, nt - 1), 0, 0),
                               memory_space=pltpu.SMEM),
                  pl.BlockSpec((rows, d), lambda i: (i, 0)),
                  pl.BlockSpec((rows, TOP_K), lambda i: (i, 0)),
                  pl.BlockSpec(memory_space=pl.ANY),
                  pl.BlockSpec((d, f2), lambda i: (0, 0)),
                  pl.BlockSpec((f2 // 2, d), lambda i: (0, 0)), vec, vec],
        out_specs=pl.BlockSpec((rows, d), lambda i: (i, 0)),
        out_shape=jax.ShapeDtypeStruct((n, d), F32),
        scratch_shapes=[pltpu.VMEM((2, TOP_K, DEST_TILE, d), F32),
                        pltpu.SemaphoreType.DMA((2,))],
        compiler_params=_params(("arbitrary",)),
        name="combine_ln",
    )(dest, dest, h, gw, y_rows, ws_gate_up.astype(BF16), ws_down.astype(BF16),
      g.reshape(1, d), b.reshape(1, d))


def _moe_ln(h, w_router, b_router, w_gate_up, w_down, layer, ws_gate_up, ws_down, g, b, alpha):
    n, d = h.shape
    n_e = w_router.shape[1]
    eidx, gw, rank, counts = _router(h, w_router, b_router)

    n_sorted = (n * TOP_K + n_e * (SUBLANES - 1)) // SUBLANES * SUBLANES
    counts = counts[:, 0]
    padded = (counts + SUBLANES - 1) // SUBLANES * SUBLANES
    pstart = (jnp.cumsum(padded) - padded).astype(jnp.int32)

    dest = _dest_rows(eidx, rank, pstart)
    dest = dest.reshape(-1, TOP_K, DEST_TILE // DMA_UNROLL, DMA_UNROLL)
    dest = dest.transpose(0, 2, 3, 1).reshape(-1, DEST_TILE // DMA_UNROLL, DMA_UNROLL * TOP_K)
    xg = _dispatch(dest, h, n_sorted)
    y_rows = _grouped_experts(xg, pstart, counts, w_gate_up, w_down, layer)
    return _combine_ln(dest, h, gw.T, y_rows, ws_gate_up, ws_down, g, b, alpha)


F_PIECES = 3
ONES_ROWS = 16


def _gate_placement(n_heads):
    import numpy as np
    n_pairs = n_heads // 2
    pk = np.zeros((F_PIECES, LANES, n_pairs * LANES), np.float32)
    ones_k = np.zeros((1, n_pairs * LANES), np.float32)
    pq = np.zeros((F_PIECES, LANES, n_pairs * LANES), np.float32)
    ones_q = np.zeros((1, n_pairs * LANES), np.float32)
    for h in range(n_heads):
        p, e = divmod(h, 2)
        for j in range(F_PIECES):
            pk[j, h, p * LANES + F_PIECES * e + j] = -1.0
            ones_q[0, p * LANES + F_PIECES * e + j] = 1.0
            pq[j, h, p * LANES + F_PIECES * (2 + e) + j] = 1.0
            ones_k[0, p * LANES + F_PIECES * (2 + e) + j] = 1.0
    return (jnp.asarray(pk, BF16), jnp.asarray(ones_k), jnp.asarray(pq, BF16), jnp.asarray(ones_q))


def _kvf_kernel(h_ref, wk_ref, wvt_ref, wf_ref, bf_ref, pk_ref, onesk_ref, k_ref, vt_ref, fp_ref,
                carry_ref, *, tile):
    s = pl.program_id(1)
    n_heads = vt_ref.shape[1]
    n_pairs = n_heads // 2

    @pl.when(s == 0)
    def _():
        carry_ref[...] = jnp.zeros_like(carry_ref)

    h = h_ref[0]
    hb = h.astype(BF16)
    h_lo = (h - hb.astype(F32)).astype(BF16)
    wf = wf_ref[...]
    wf_hi = wf.astype(BF16)
    wf_lo = (wf - wf_hi.astype(F32)).astype(BF16)
    z = (jnp.dot(hb, wf_hi, preferred_element_type=F32)
         + (jnp.dot(hb, wf_lo, preferred_element_type=F32)
            + jnp.dot(h_lo, wf_hi, preferred_element_type=F32))) + bf_ref[...]
    logf = jnp.minimum(z, 0.0) - jnp.log1p(jnp.exp(-jnp.abs(z)))
    row = lax.broadcasted_iota(jnp.int32, (tile, 1), 0)
    span = 1
    while span < tile:
        logf = logf + jnp.where(row >= span, pltpu.roll(logf, span, axis=0), 0.0)
        span *= 2
    cum = logf + carry_ref[...]
    carry_ref[...] = cum[tile - 1:tile, :]

    pieces = []
    rest = cum
    for _ in range(F_PIECES):
        piece = rest.astype(BF16)
        pieces.append(piece)
        rest = rest - piece.astype(F32)
    fp_ref[0] = jnp.concatenate(pieces, axis=1)
    kf = onesk_ref[...]
    for j in range(F_PIECES):
        kf = kf + jnp.dot(pieces[j], pk_ref[j], preferred_element_type=F32)

    k = jnp.dot(hb, wk_ref[...], preferred_element_type=F32)
    parts = []
    for p in range(n_pairs):
        parts += [k[:, p * LANES:(p + 1) * LANES], kf[:, p * LANES:(p + 1) * LANES]]
    k_ref[0] = jnp.concatenate(parts, axis=1).astype(BF16)

    vt = lax.dot_general(wvt_ref[...], hb, (((1,), (1,)), ((), ())),
                         preferred_element_type=F32).astype(BF16)
    for hd in range(n_heads):
        vt_ref[0, hd, 0:HEAD_DIM, :] = vt[hd * HEAD_DIM:(hd + 1) * HEAD_DIM, :]
        vt_ref[0, hd, HEAD_DIM:HEAD_DIM + ONES_ROWS, :] = jnp.ones((ONES_ROWS, tile), BF16)


def _kvf(h3, w_kvf, b_f, pk, ones_k, tile=512):
    bsz, seq, d = h3.shape
    n_h = b_f.shape[0]
    wk = w_kvf[:, :d].astype(BF16)
    wvt = w_kvf[:, d:2 * d].T.astype(BF16)
    wf = jnp.pad(w_kvf[:, 2 * d:], ((0, 0), (0, LANES - n_h)))
    bf = jnp.pad(b_f, (0, LANES - n_h)).reshape(1, LANES)
    const2 = lambda i, j: (0, 0)
    return pl.pallas_call(
        functools.partial(_kvf_kernel, tile=tile),
        grid=(bsz, seq // tile),
        in_specs=[pl.BlockSpec((1, tile, d), lambda i, j: (i, j, 0)),
                  pl.BlockSpec((d, d), const2), pl.BlockSpec((d, d), const2),
                  pl.BlockSpec((d, LANES), const2), pl.BlockSpec((1, LANES), const2),
                  pl.BlockSpec(pk.shape, lambda i, j: (0, 0, 0)),
                  pl.BlockSpec(ones_k.shape, const2)],
        out_specs=[pl.BlockSpec((1, tile, 2 * d), lambda i, j: (i, j, 0)),
                   pl.BlockSpec((1, n_h, HEAD_DIM + ONES_ROWS, tile), lambda i, j: (i, 0, 0, j)),
                   pl.BlockSpec((1, tile, F_PIECES * LANES), lambda i, j: (i, j, 0))],
        out_shape=[jax.ShapeDtypeStruct((bsz, seq, 2 * d), BF16),
                   jax.ShapeDtypeStruct((bsz, n_h, HEAD_DIM + ONES_ROWS, seq), BF16),
                   jax.ShapeDtypeStruct((bsz, seq, F_PIECES * LANES), BF16)],
        scratch_shapes=[pltpu.VMEM((1, LANES), F32)],
        compiler_params=_params(("arbitrary", "arbitrary")),
        name="kvf_proj",
    )(h3, wk, wvt, wf, bf, pk, ones_k)


def _qproj_kernel(h_ref, wq_ref, fp_ref, pq_ref, onesq_ref, q_ref, *, scale):
    n_pairs = q_ref.shape[1] // (2 * LANES)
    q = jnp.dot(h_ref[...].astype(BF16), wq_ref[...], preferred_element_type=F32) * scale
    qf = onesq_ref[...]
    for j in range(F_PIECES):
        qf = qf + jnp.dot(fp_ref[:, j * LANES:(j + 1) * LANES], pq_ref[j],
                          preferred_element_type=F32)
    parts = []
    for p in range(n_pairs):
        parts += [q[:, p * LANES:(p + 1) * LANES], qf[:, p * LANES:(p + 1) * LANES]]
    q_ref[...] = jnp.concatenate(parts, axis=1).astype(BF16)


def _qproj(h, w_q, fpieces, pq, ones_q, scale, tile=512):
    n, d = h.shape
    n_h = d // HEAD_DIM
    const2 = lambda i: (0, 0)
    return pl.pallas_call(
        functools.partial(_qproj_kernel, scale=scale),
        grid=(n // tile,),
        in_specs=[pl.BlockSpec((tile, d), lambda i: (i, 0)),
                  pl.BlockSpec((d, d), const2),
                  pl.BlockSpec((tile, F_PIECES * LANES), lambda i: (i, 0)),
                  pl.BlockSpec(pq.shape, lambda i: (0, 0, 0)),
                  pl.BlockSpec(ones_q.shape, const2)],
        out_specs=pl.BlockSpec((tile, n_h * LANES), lambda i: (i, 0)),
        out_shape=jax.ShapeDtypeStruct((n, n_h * LANES), BF16),
        compiler_params=_params(("arbitrary",)),
        name="q_proj",
    )(h, w_q.astype(BF16), fpieces, pq, ones_q)


def _attn_kernel(q_ref, k_ref, vt_ref, o_ref, sa_ref, sb_ref, *, tq, tk):
    i = pl.program_id(2)
    per_q = tq // tk
    qa = q_ref[0]
    lane = lax.broadcasted_iota(jnp.int32, (1, 2 * LANES), 1)

    def lanes_in(start, size):
        return jnp.logical_and(lane >= start, lane < start + size)

    qs = []
    for hd in range(2):
        own = jnp.logical_or(
            lanes_in(hd * HEAD_DIM, HEAD_DIM),
            jnp.logical_or(lanes_in(LANES + hd * F_PIECES, F_PIECES),
                           lanes_in(LANES + (2 + hd) * F_PIECES, F_PIECES)))
        qs.append(jnp.where(own, qa, jnp.zeros_like(qa)))
    row = lax.broadcasted_iota(jnp.int32, (tk, tq), 0)
    col = lax.broadcasted_iota(jnp.int32, (tk, tq), 1)

    def logits(j, s_ref):
        kj = k_ref[0, pl.ds(pl.multiple_of(j * tk, tk), tk), :]
        for hd in range(2):
            s_ref[hd] = lax.dot_general(kj, qs[hd], (((1,), (1,)), ((), ())),
                                        preferred_element_type=F32)

    def softmax_pv(j, s_ref, carry, diag_offset):
        ms, ls, accs = carry
        keys = pl.ds(pl.multiple_of(j * tk, tk), tk)
        new_ms, new_ls, new_accs = [], [], []
        for hd in range(2):
            sc = s_ref[hd]
            if diag_offset is not None:
                sc = jnp.where(row + diag_offset <= col, sc, NEG_INF)
            m_new = jnp.maximum(ms[hd], jnp.max(sc, axis=0, keepdims=True))
            pt = jnp.exp(sc - m_new).astype(BF16)
            alpha = jnp.exp(ms[hd] - m_new)
            r = jnp.dot(vt_ref[0, hd, :, keys], pt,
                        preferred_element_type=F32)
            new_ms.append(m_new)
            new_ls.append(alpha * ls[hd] + r[HEAD_DIM:HEAD_DIM + 1, :])
            new_accs.append(alpha * accs[hd] + r[:HEAD_DIM, :])
        return tuple(new_ms), tuple(new_ls), tuple(new_accs)

    def two_blocks(t, carry):
        j = t * 2
        logits(j + 1, sb_ref)
        carry = softmax_pv(j, sa_ref, carry, None)
        logits(j + 2, sa_ref)
        return softmax_pv(j + 1, sb_ref, carry, None)

    init = ((jnp.full((1, tq), NEG_INF, F32),) * 2, (jnp.zeros((1, tq), F32),) * 2,
            (jnp.zeros((HEAD_DIM, tq), F32),) * 2)
    logits(0, sa_ref)
    first_diag = i * per_q
    carry = lax.fori_loop(0, first_diag // 2, two_blocks, init)
    bufs = (sa_ref, sb_ref)
    for u in range(per_q):
        if u + 1 < per_q:
            logits(first_diag + u + 1, bufs[(u + 1) % 2])
        carry = softmax_pv(first_diag + u, bufs[u % 2], carry, u * tk)
    _, ls, accs = carry
    out_t = jnp.concatenate([accs[0] / ls[0], accs[1] / ls[1]], axis=0)
    o_ref[0] = out_t.T.astype(BF16)


def _attention(qaug, kaug, vaug_t, d, tq=512, tk=256):
    bsz, seq, _ = kaug.shape
    n_pairs = vaug_t.shape[1] // 2
    assert (tq // tk) % 2 == 0
    return pl.pallas_call(
        functools.partial(_attn_kernel, tq=tq, tk=tk),
        grid=(bsz, n_pairs, seq // tq),
        in_specs=[pl.BlockSpec((1, tq, 2 * LANES), lambda b, p, i: (b, i, p)),
                  pl.BlockSpec((1, seq, 2 * LANES), lambda b, p, i: (b, 0, p)),
                  pl.BlockSpec((1, 2, HEAD_DIM + ONES_ROWS, seq), lambda b, p, i: (b, p, 0, 0))],
        out_specs=pl.BlockSpec((1, tq, LANES), lambda b, p, i: (b, i, p)),
        out_shape=jax.ShapeDtypeStruct((bsz, seq, d), BF16),
        scratch_shapes=[pltpu.VMEM((2, tk, tq), F32), pltpu.VMEM((2, tk, tq), F32)],
        compiler_params=_params(("arbitrary", "arbitrary", "arbitrary")),
        name="fox_attention",
    )(qaug, kaug, vaug_t)


def _oproj_ln_kernel(o_ref, h_ref, wo_ref, g_ref, b_ref, out_ref, *, alpha):
    mix = jnp.dot(o_ref[...], wo_ref[...], preferred_element_type=F32)
    out_ref[...] = _layer_norm(alpha * h_ref[...] + mix, g_ref[...], b_ref[...])


def _oproj_ln(o, h, w_o, g, b, alpha, tile=512):
    n, d = h.shape
    vec = pl.BlockSpec((1, d), lambda i: (0, 0))
    tspec = pl.BlockSpec((tile, d), lambda i: (i, 0))
    return pl.pallas_call(
        functools.partial(_oproj_ln_kernel, alpha=alpha),
        grid=(n // tile,),
        in_specs=[tspec, tspec, pl.BlockSpec((d, d), lambda i: (0, 0)), vec, vec],
        out_specs=tspec,
        out_shape=jax.ShapeDtypeStruct((n, d), F32),
        compiler_params=_params(("arbitrary",)),
        name="oproj_ln",
    )(o, h, w_o.astype(BF16), g.reshape(1, d), b.reshape(1, d))


def kernel(x, pool_w, pool_scale, w_q, w_o, w_kvf, b_f, ln_g, ln_b, router_w, router_b,
           w_gate_up, w_down, ws_gate_up, ws_down):
    bsz, seq, d = x.shape
    depth = ln_g.shape[0]
    n_pool = pool_w.shape[0]
    alpha = float((2 * depth) ** 0.25)
    n = bsz * seq
    kaug = vaug_t = fpieces = None
    pk, ones_k, pq, ones_q = _gate_placement(d // HEAD_DIM)
    for l in range(depth):
        if l < n_pool:
            x = _pool_ln(x, pool_w[l], pool_scale[l], ln_g[l, 0], ln_b[l, 0], alpha)
        else:
            if l == n_pool:
                kaug, vaug_t, fpieces = _kvf(x, w_kvf, b_f, pk, ones_k)
            j = l - n_pool
            qaug = _qproj(x.reshape(n, d), w_q[j], fpieces.reshape(n, -1), pq, ones_q,
                          HEAD_DIM ** -0.5)
            o = _attention(qaug.reshape(bsz, seq, -1), kaug, vaug_t, d)
            x = _oproj_ln(o.reshape(n, d), x.reshape(n, d), w_o[j], ln_g[l, 0], ln_b[l, 0],
                          alpha).reshape(bsz, seq, d)
        x = _moe_ln(x.reshape(n, d), router_w[l], router_b[l], w_gate_up, w_down, l,
                    ws_gate_up[l], ws_down[l], ln_g[l, 1], ln_b[l, 1], alpha).reshape(bsz, seq, d)
    return x
```

```python
import functools

import jax
import jax.numpy as jnp
from jax import lax
from jax.experimental import pallas as pl
from jax.experimental.pallas import tpu as pltpu

POOL_WINDOWS = (2, 4, 8, 16)
POOL_HALO = 16
HEAD_DIM = 64
N_EXPERT_GROUPS = 8
TOPK_GROUPS = 4
TOP_K = 8
ROUTED_SCALE = 2.5
MOE_BLOCK = 1024
SUB_ROWS = 256
DMA_UNROLL = 8
DMA_QUEUES = 2
DEST_TILE = 256
LN_EPS = 1e-5

LANES = 128
SUBLANES = 8
VMEM_LIMIT = 48 * 1024 * 1024

F32 = jnp.float32
BF16 = jnp.bfloat16
NEG_INF = float("-inf")


def _params(semantics):
    return pltpu.CompilerParams(dimension_semantics=semantics, vmem_limit_bytes=VMEM_LIMIT)


def _layer_norm(z, g, b):
    mu = jnp.mean(z, axis=-1, keepdims=True)
    zc = z - mu
    var = jnp.mean(zc * zc, axis=-1, keepdims=True)
    return zc * lax.rsqrt(var + LN_EPS) * g + b


def _silu(x):
    return x * jax.nn.sigmoid(x)


def _pool_ln_kernel(x_ref, w_ref, scale_ref, g_ref, b_ref, o_ref, ext_ref, *, alpha, tile):
    s = pl.program_id(1)
    d = x_ref.shape[2]
    c = d // len(POOL_WINDOWS)

    @pl.when(s == 0)
    def _():
        ext_ref[0:POOL_HALO, :] = jnp.zeros((POOL_HALO, d), F32)

    x = x_ref[0]
    ext_ref[POOL_HALO:POOL_HALO + tile, :] = x
    pos = (s * tile + 1 + lax.broadcasted_iota(jnp.int32, (tile, 1), 0)).astype(F32)
    ys = []
    for gi, w in enumerate(POOL_WINDOWS):
        cs = slice(gi * c, (gi + 1) * c)
        acc = ext_ref[:, cs]
        span = 1
        while span < w:
            acc = acc + pltpu.roll(acc, span, axis=0)
            span *= 2
        mean = acc[POOL_HALO:, :] / jnp.minimum(pos, float(w))
        mixed = mean - x[:, cs]
        ys.append(jnp.dot(mixed.astype(BF16), w_ref[gi], preferred_element_type=F32))
    y = jnp.concatenate(ys, axis=1) * scale_ref[...]
    o_ref[0] = _layer_norm(alpha * x + y, g_ref[...], b_ref[...])
    ext_ref[0:POOL_HALO, :] = x[tile - POOL_HALO:, :]


def _pool_ln(x, w, scale, g, b, alpha, tile=1024):
    bsz, seq, d = x.shape
    ng, c, _ = w.shape
    vec = pl.BlockSpec((1, d), lambda i, j: (0, 0))
    return pl.pallas_call(
        functools.partial(_pool_ln_kernel, alpha=alpha, tile=tile),
        grid=(bsz, seq // tile),
        in_specs=[pl.BlockSpec((1, tile, d), lambda i, j: (i, j, 0)),
                  pl.BlockSpec((ng, c, c), lambda i, j: (0, 0, 0)),
                  vec, vec, vec],
        out_specs=pl.BlockSpec((1, tile, d), lambda i, j: (i, j, 0)),
        out_shape=jax.ShapeDtypeStruct(x.shape, F32),
        scratch_shapes=[pltpu.VMEM((POOL_HALO + tile, d), F32)],
        compiler_params=_params(("arbitrary", "arbitrary")),
        name="pool_ln",
    )(x, w.astype(BF16), scale.reshape(1, d), g.reshape(1, d), b.reshape(1, d))


def _first_argmax(v, iota, size):
    m = jnp.max(v, axis=0, keepdims=True)
    idx = jnp.min(jnp.where(v == m, iota, size), axis=0, keepdims=True)
    return m, idx


def _router_kernel(h_ref, wt_ref, b_ref, eidx_ref, gw_ref, rank_ref, cnt_ref, carry_ref, *, tile):
    i = pl.program_id(0)
    n_e = wt_ref.shape[0]
    per_group = n_e // N_EXPERT_GROUPS

    @pl.when(i == 0)
    def _():
        carry_ref[...] = jnp.zeros_like(carry_ref)

    h = h_ref[...]
    h_hi = h.astype(BF16)
    h_lo = (h - h_hi.astype(F32)).astype(BF16)
    wt = wt_ref[...]
    wt_hi = wt.astype(BF16)
    wt_lo = (wt - wt_hi.astype(F32)).astype(BF16)
    nt_dims = (((1,), (1,)), ((), ()))
    logits = (lax.dot_general(wt_hi, h_hi, nt_dims, preferred_element_type=F32)
              + (lax.dot_general(wt_lo, h_hi, nt_dims, preferred_element_type=F32)
                 + lax.dot_general(wt_hi, h_lo, nt_dims, preferred_element_type=F32)))
    s = jax.nn.sigmoid(logits)
    sb = s + b_ref[...]

    iota_g = lax.broadcasted_iota(jnp.int32, (per_group, tile), 0)
    gscores = []
    for g in range(N_EXPERT_GROUPS):
        blk = sb[g * per_group:(g + 1) * per_group, :]
        m1, i1 = _first_argmax(blk, iota_g, per_group)
        m2 = jnp.max(jnp.where(iota_g == i1, NEG_INF, blk), axis=0, keepdims=True)
        gscores.append(m1 + m2)
    gs = jnp.concatenate(gscores, axis=0)
    iota_8 = lax.broadcasted_iota(jnp.int32, (N_EXPERT_GROUPS, tile), 0)
    gsel = jnp.zeros((N_EXPERT_GROUPS, tile), jnp.bool_)
    for _ in range(TOPK_GROUPS):
        _, gi = _first_argmax(gs, iota_8, N_EXPERT_GROUPS)
        hit = iota_8 == gi
        gsel = jnp.logical_or(gsel, hit)
        gs = jnp.where(hit, NEG_INF, gs)
    cand = jnp.concatenate(
        [jnp.where(gsel[g:g + 1, :], sb[g * per_group:(g + 1) * per_group, :], NEG_INF)
         for g in range(N_EXPERT_GROUPS)], axis=0)

    iota_e = lax.broadcasted_iota(jnp.int32, (n_e, tile), 0)
    member = jnp.zeros((n_e, tile), F32)
    idxs, svals = [], []
    for _ in range(TOP_K):
        _, ei = _first_argmax(cand, iota_e, n_e)
        hit = iota_e == ei
        idxs.append(ei)
        svals.append(jnp.sum(jnp.where(hit, s, 0.0), axis=0, keepdims=True))
        member = jnp.where(hit, 1.0, member)
        cand = jnp.where(hit, NEG_INF, cand)
    sv = jnp.concatenate(svals, axis=0)
    gw_ref[...] = sv / jnp.sum(sv, axis=0, keepdims=True) * ROUTED_SCALE
    eidx_ref[...] = jnp.concatenate(idxs, axis=0)

    r = lax.broadcasted_iota(jnp.int32, (tile, tile), 0)
    c = lax.broadcasted_iota(jnp.int32, (tile, tile), 1)
    earlier = (r < c).astype(BF16)
    before = jnp.dot(member.astype(BF16), earlier, preferred_element_type=F32) + carry_ref[...]
    ranks = [jnp.sum(jnp.where(iota_e == ei, before, 0.0), axis=0, keepdims=True) for ei in idxs]
    rank_ref[...] = jnp.concatenate(ranks, axis=0).astype(jnp.int32)
    carry_ref[...] = carry_ref[...] + jnp.sum(member, axis=1, keepdims=True)
    cnt_ref[...] = carry_ref[...].astype(jnp.int32)


def _router(h, w_router, b_router, tile=512):
    n, d = h.shape
    n_e = w_router.shape[1]
    out_kt = pl.BlockSpec((TOP_K, tile), lambda i: (0, i))
    return pl.pallas_call(
        functools.partial(_router_kernel, tile=tile),
        grid=(n // tile,),
        in_specs=[pl.BlockSpec((tile, d), lambda i: (i, 0)),
                  pl.BlockSpec((n_e, d), lambda i: (0, 0)),
                  pl.BlockSpec((n_e, 1), lambda i: (0, 0))],
        out_specs=[out_kt, out_kt, out_kt, pl.BlockSpec((n_e, 1), lambda i: (0, 0))],
        out_shape=[jax.ShapeDtypeStruct((TOP_K, n), jnp.int32),
                   jax.ShapeDtypeStruct((TOP_K, n), F32),
                   jax.ShapeDtypeStruct((TOP_K, n), jnp.int32),
                   jax.ShapeDtypeStruct((n_e, 1), jnp.int32)],
        scratch_shapes=[pltpu.VMEM((n_e, 1), F32)],
        compiler_params=_params(("arbitrary",)),
        name="router",
    )(h, w_router.T, b_router.reshape(n_e, 1))


def _dest_kernel(eidx_ref, rank_ref, pstart_ref, dest_ref):
    n_e = pstart_ref.shape[0]
    iota_e = lax.broadcasted_iota(jnp.int32, (n_e, DEST_TILE), 0)
    starts = pstart_ref[...]
    for t in range(dest_ref.shape[0]):
        cols = slice(t * DEST_TILE, (t + 1) * DEST_TILE)
        rows = [jnp.sum(jnp.where(iota_e == eidx_ref[k:k + 1, cols], starts, 0), axis=0,
                        keepdims=True) for k in range(TOP_K)]
        dest_ref[t] = jnp.concatenate(rows, axis=0) + rank_ref[:, cols]


def _dest_rows(eidx, rank, pstart, tiles=8):
    n = eidx.shape[1]
    n_e = pstart.shape[0]
    kt = pl.BlockSpec((TOP_K, tiles * DEST_TILE), lambda i: (0, i))
    return pl.pallas_call(
        _dest_kernel,
        grid=(n // (tiles * DEST_TILE),),
        in_specs=[kt, kt, pl.BlockSpec((n_e, 1), lambda i: (0, 0))],
        out_specs=pl.BlockSpec((tiles, TOP_K, DEST_TILE), lambda i: (i, 0, 0)),
        out_shape=jax.ShapeDtypeStruct((n // DEST_TILE, TOP_K, DEST_TILE), jnp.int32),
        compiler_params=_params(("arbitrary",)),
        name="dest_rows",
    )(eidx, rank, pstart.reshape(n_e, 1))


def _dispatch_kernel(dest_ref, h_ref, xg_hbm, sem):
    tiles = dest_ref.shape[0]
    for sub in range(tiles):
        def group(g, carry, sub=sub):
            off = pl.multiple_of(g * DMA_UNROLL, DMA_UNROLL)
            for u in range(DMA_UNROLL):
                for k in range(TOP_K):
                    pltpu.make_async_copy(h_ref.at[pl.ds(sub * DEST_TILE + off + u, 1)],
                                          xg_hbm.at[pl.ds(dest_ref[sub, g, u * TOP_K + k], 1)],
                                          sem).start(priority=k % DMA_QUEUES)
            return carry
        lax.fori_loop(0, DEST_TILE // DMA_UNROLL, group, 0)
    for _ in range(tiles * TOP_K):
        pltpu.make_async_copy(xg_hbm.at[pl.ds(0, DEST_TILE)], xg_hbm.at[pl.ds(0, DEST_TILE)],
                              sem).wait()


def _dispatch(dest, h, n_rows, tiles=8):
    n, d = h.shape
    rows = tiles * DEST_TILE
    return pl.pallas_call(
        _dispatch_kernel,
        grid=(n // rows,),
        in_specs=[pl.BlockSpec((tiles,) + dest.shape[1:], lambda i: (i, 0, 0),
                               memory_space=pltpu.SMEM),
                  pl.BlockSpec((rows, d), lambda i: (i, 0))],
        out_specs=pl.BlockSpec(memory_space=pl.ANY),
        out_shape=jax.ShapeDtypeStruct((n_rows, d), F32),
        scratch_shapes=[pltpu.SemaphoreType.DMA(())],
        compiler_params=_params(("arbitrary",)),
        name="dispatch_rows",
    )(dest, h)


def _grouped_kernel(ps_ref, cnt_ref, xg_hbm, wgu_ref, wd_ref, y_hbm, xfirst, xbuf, ybuf, wgu_bf,
                    wd_bf, fsem, isem, osem, done_ref, osize_ref):
    e = pl.program_id(0)
    last = pl.num_programs(0) - 1
    n_rows = cnt_ref[e]
    base = ps_ref[e]
    n_chunks = (n_rows + MOE_BLOCK - 1) // MOE_BLOCK
    f = wd_ref.shape[2]
    row = lax.broadcasted_iota(jnp.int32, (SUB_ROWS, 1), 0)

    def dma_rows(count, c):
        valid = jnp.minimum(count - c * MOE_BLOCK, MOE_BLOCK)
        return pl.multiple_of((valid + SUBLANES - 1) // SUBLANES * SUBLANES, SUBLANES)

    def first_copy(ex):
        size = dma_rows(cnt_ref[ex], 0)
        start = pl.multiple_of(ps_ref[ex], SUBLANES)
        slot = lax.rem(ex, 2)
        return pltpu.make_async_copy(xg_hbm.at[pl.ds(start, size)],
                                     xfirst.at[slot, pl.ds(0, size)], fsem.at[slot])

    def in_copy(c, slot):
        size = dma_rows(n_rows, c)
        start = pl.multiple_of(base + c * MOE_BLOCK, SUBLANES)
        return pltpu.make_async_copy(xg_hbm.at[pl.ds(start, size)],
                                     xbuf.at[slot, pl.ds(0, size)], isem.at[slot])

    def out_copy(start, size, slot):
        return pltpu.make_async_copy(ybuf.at[slot, pl.ds(0, size)],
                                     y_hbm.at[pl.ds(start, size)], osem.at[slot])

    def process(x_ref, x_slot, c):
        valid = jnp.minimum(n_rows - c * MOE_BLOCK, MOE_BLOCK)
        done = done_ref[0]
        slot = lax.rem(done, 2)

        @pl.when(done >= 2)
        def _():
            out_copy(0, pl.multiple_of(osize_ref[slot], SUBLANES), slot).wait()

        def sub_block(j, carry):
            r0 = pl.multiple_of(j * SUB_ROWS, SUB_ROWS)
            x = jnp.where(row + r0 < valid, x_ref[x_slot, pl.ds(r0, SUB_ROWS), :], 0.0).astype(BF16)
            gu = jnp.dot(x, wgu_bf[...], preferred_element_type=F32)
            act = _silu(gu[:, :f]) * gu[:, f:]
            ybuf[slot, pl.ds(r0, SUB_ROWS), :] = jnp.dot(act.astype(BF16), wd_bf[...],
                                                         preferred_element_type=F32)
            return carry

        lax.fori_loop(0, (valid + SUB_ROWS - 1) // SUB_ROWS, sub_block, 0)
        size = dma_rows(n_rows, c)
        osize_ref[slot] = size
        out_copy(pl.multiple_of(base + c * MOE_BLOCK, SUBLANES), size, slot).start()
        done_ref[0] = done + 1

    @pl.when(e == 0)
    def _():
        done_ref[0] = 0
        xfirst[...] = jnp.zeros_like(xfirst)
        xbuf[...] = jnp.zeros_like(xbuf)

        @pl.when(n_rows > 0)
        def _():
            first_copy(0).start()

    nxt = jnp.minimum(e + 1, last)

    @pl.when(jnp.logical_and(e < last, cnt_ref[nxt] > 0))
    def _():
        first_copy(nxt).start()

    @pl.when(n_chunks > 0)
    def _():
        wgu_bf[...] = wgu_ref[0, 0].astype(BF16)
        wd_bf[...] = wd_ref[0, 0].astype(BF16)

        @pl.when(n_chunks > 1)
        def _():
            in_copy(1, 1).start()

        first_copy(e).wait()
        process(xfirst, lax.rem(e, 2), 0)

        def chunk(c, carry):
            slot = lax.rem(c, 2)

            @pl.when(c + 1 < n_chunks)
            def _():
                in_copy(c + 1, 1 - slot).start()

            in_copy(c, slot).wait()
            process(xbuf, slot, c)
            return carry

        lax.fori_loop(1, n_chunks, chunk, 0)

    @pl.when(e == last)
    def _():
        for back in (1, 2):
            @pl.when(done_ref[0] >= back)
            def _():
                slot = lax.rem(done_ref[0] - back, 2)
                out_copy(0, pl.multiple_of(osize_ref[slot], SUBLANES), slot).wait()


def _grouped_experts(xg, pstart, counts, w_gate_up, w_down, layer):
    p, d = xg.shape
    _, n_e, _, f2 = w_gate_up.shape
    f = f2 // 2
    grid_spec = pltpu.PrefetchScalarGridSpec(
        num_scalar_prefetch=2,
        grid=(n_e,),
        in_specs=[
            pl.BlockSpec(memory_space=pl.ANY),
            pl.BlockSpec((1, 1, d, f2), lambda e, ps, cnt: (layer, e, 0, 0)),
            pl.BlockSpec((1, 1, f, d), lambda e, ps, cnt: (layer, e, 0, 0)),
        ],
        out_specs=pl.BlockSpec(memory_space=pl.ANY),
        scratch_shapes=[pltpu.VMEM((2, MOE_BLOCK, d), F32), pltpu.VMEM((2, MOE_BLOCK, d), F32),
                        pltpu.VMEM((2, MOE_BLOCK, d), F32),
                        pltpu.VMEM((d, f2), BF16), pltpu.VMEM((f, d), BF16),
                        pltpu.SemaphoreType.DMA((2,)), pltpu.SemaphoreType.DMA((2,)),
                        pltpu.SemaphoreType.DMA((2,)),
                        pltpu.SMEM((1,), jnp.int32), pltpu.SMEM((2,), jnp.int32)],
    )
    return pl.pallas_call(
        _grouped_kernel,
        grid_spec=grid_spec,
        out_shape=jax.ShapeDtypeStruct((p, d), F32),
        compiler_params=_params(("arbitrary",)),
        name="grouped_experts",
    )(pstart, counts, xg, w_gate_up, w_down)


def _combine_ln_kernel(dpair_ref, dnext_ref, h_ref, gw_ref, y_hbm, wsgu_ref, wsd_ref, g_ref, b_ref,
                       o_ref, ybuf, sem, *, alpha):
    i = pl.program_id(0)
    n_steps = pl.num_programs(0)
    f = wsd_ref.shape[0]

    def start_gather(dref, sub, s):
        def group(g, carry):
            off = pl.multiple_of(g * DMA_UNROLL, DMA_UNROLL)
            for u in range(DMA_UNROLL):
                for k in range(TOP_K):
                    pltpu.make_async_copy(y_hbm.at[pl.ds(dref[sub, g, u * TOP_K + k], 1)],
                                          ybuf.at[s, k, pl.ds(off, DMA_UNROLL)].at[pl.ds(u, 1)],
                                          sem.at[s]).start(priority=k % DMA_QUEUES)
            return carry
        lax.fori_loop(0, DEST_TILE // DMA_UNROLL, group, 0)

    def finish(s):
        rows = slice(s * DEST_TILE, (s + 1) * DEST_TILE)
        h = h_ref[rows, :]
        gu = jnp.dot(h.astype(BF16), wsgu_ref[...], preferred_element_type=F32)
        act = _silu(gu[:, :f]) * gu[:, f:]
        shared = jnp.dot(act.astype(BF16), wsd_ref[...], preferred_element_type=F32)
        for k in range(TOP_K):
            pltpu.make_async_copy(y_hbm.at[pl.ds(0, DEST_TILE)], ybuf.at[s, k], sem.at[s]).wait()
        gw = gw_ref[rows, :]
        routed = gw[:, 0:1] * ybuf[s, 0]
        for k in range(1, TOP_K):
            routed = routed + gw[:, k:k + 1] * ybuf[s, k]
        o_ref[rows, :] = _layer_norm(alpha * h + (routed + shared), g_ref[...], b_ref[...])

    @pl.when(i == 0)
    def _():
        start_gather(dpair_ref, 0, 0)

    start_gather(dpair_ref, 1, 1)
    finish(0)

    @pl.when(i + 1 < n_steps)
    def _():
        start_gather(dnext_ref, 0, 0)

    finish(1)


def _combine_ln(dest, h, gw, y_rows, ws_gate_up, ws_down, g, b, alpha):
    n, d = h.shape
    f2 = ws_gate_up.shape[1]
    nt = n // DEST_TILE
    rows = 2 * DEST_TILE
    vec = pl.BlockSpec((1, d), lambda i: (0, 0))
    return pl.pallas_call(
        functools.partial(_combine_ln_kernel, alpha=alpha),
        grid=(nt // 2,),
        in_specs=[pl.BlockSpec((2,) + dest.shape[1:], lambda i: (i, 0, 0),
                               memory_space=pltpu.SMEM),
                  pl.BlockSpec((1,) + dest.shape[1:],
                               lambda i: (jnp.minimum(2 * i + 2, nt - 1), 0, 0),
                               memory_space=pltpu.SMEM),
                  pl.BlockSpec((rows, d), lambda i: (i, 0)),
                  pl.BlockSpec((rows, TOP_K), lambda i: (i, 0)),
                  pl.BlockSpec(memory_space=pl.ANY),
                  pl.BlockSpec((d, f2), lambda i: (0, 0)),
                  pl.BlockSpec((f2 // 2, d), lambda i: (0, 0)), vec, vec],
        out_specs=pl.BlockSpec((rows, d), lambda i: (i, 0)),
        out_shape=jax.ShapeDtypeStruct((n, d), F32),
        scratch_shapes=[pltpu.VMEM((2, TOP_K, DEST_TILE, d), F32),
                        pltpu.SemaphoreType.DMA((2,))],
        compiler_params=_params(("arbitrary",)),
        name="combine_ln",
    )(dest, dest, h, gw, y_rows, ws_gate_up.astype(BF16), ws_down.astype(BF16),
      g.reshape(1, d), b.reshape(1, d))


def _moe_ln(h, w_router, b_router, w_gate_up, w_down, layer, ws_gate_up, ws_down, g, b, alpha):
    n, d = h.shape
    n_e = w_router.shape[1]
    eidx, gw, rank, counts = _router(h, w_router, b_router)

    n_sorted = (n * TOP_K + n_e * (SUBLANES - 1)) // SUBLANES * SUBLANES
    counts = counts[:, 0]
    padded = (counts + SUBLANES - 1) // SUBLANES * SUBLANES
    pstart = (jnp.cumsum(padded) - padded).astype(jnp.int32)

    dest = _dest_rows(eidx, rank, pstart)
    dest = dest.reshape(-1, TOP_K, DEST_TILE // DMA_UNROLL, DMA_UNROLL)
    dest = dest.transpose(0, 2, 3, 1).reshape(-1, DEST_TILE // DMA_UNROLL, DMA_UNROLL * TOP_K)
    xg = _dispatch(dest, h, n_sorted)
    y_rows = _grouped_experts(xg, pstart, counts, w_gate_up, w_down, layer)
    return _combine_ln(dest, h, gw.T, y_rows, ws_gate_up, ws_down, g, b, alpha)


F_PIECES = 3
ONES_ROWS = 16


def _gate_placement(n_heads):
    import numpy as np
    n_pairs = n_heads // 2
    pk = np.zeros((F_PIECES, LANES, n_pairs * LANES), np.float32)
    ones_k = np.zeros((1, n_pairs * LANES), np.float32)
    pq = np.zeros((F_PIECES, LANES, n_pairs * LANES), np.float32)
    ones_q = np.zeros((1, n_pairs * LANES), np.float32)
    for h in range(n_heads):
        p, e = divmod(h, 2)
        for j in range(F_PIECES):
            pk[j, h, p * LANES + F_PIECES * e + j] = -1.0
            ones_q[0, p * LANES + F_PIECES * e + j] = 1.0
            pq[j, h, p * LANES + F_PIECES * (2 + e) + j] = 1.0
            ones_k[0, p * LANES + F_PIECES * (2 + e) + j] = 1.0
    return (jnp.asarray(pk, BF16), jnp.asarray(ones_k), jnp.asarray(pq, BF16), jnp.asarray(ones_q))


def _kvf_kernel(h_ref, wk_ref, wvt_ref, wf_ref, bf_ref, pk_ref, onesk_ref, k_ref, vt_ref, fp_ref,
                carry_ref, *, tile):
    s = pl.program_id(1)
    n_heads = vt_ref.shape[1]
    n_pairs = n_heads // 2

    @pl.when(s == 0)
    def _():
        carry_ref[...] = jnp.zeros_like(carry_ref)

    h = h_ref[0]
    hb = h.astype(BF16)
    h_lo = (h - hb.astype(F32)).astype(BF16)
    wf = wf_ref[...]
    wf_hi = wf.astype(BF16)
    wf_lo = (wf - wf_hi.astype(F32)).astype(BF16)
    z = (jnp.dot(hb, wf_hi, preferred_element_type=F32)
         + (jnp.dot(hb, wf_lo, preferred_element_type=F32)
            + jnp.dot(h_lo, wf_hi, preferred_element_type=F32))) + bf_ref[...]
    logf = jnp.minimum(z, 0.0) - jnp.log1p(jnp.exp(-jnp.abs(z)))
    row = lax.broadcasted_iota(jnp.int32, (tile, 1), 0)
    span = 1
    while span < tile:
        logf = logf + jnp.where(row >= span, pltpu.roll(logf, span, axis=0), 0.0)
        span *= 2
    cum = logf + carry_ref[...]
    carry_ref[...] = cum[tile - 1:tile, :]

    pieces = []
    rest = cum
    for _ in range(F_PIECES):
        piece = rest.astype(BF16)
        pieces.append(piece)
        rest = rest - piece.astype(F32)
    fp_ref[0] = jnp.concatenate(pieces, axis=1)
    kf = onesk_ref[...]
    for j in range(F_PIECES):
        kf = kf + jnp.dot(pieces[j], pk_ref[j], preferred_element_type=F32)

    k = jnp.dot(hb, wk_ref[...], preferred_element_type=F32)
    parts = []
    for p in range(n_pairs):
        parts += [k[:, p * LANES:(p + 1) * LANES], kf[:, p * LANES:(p + 1) * LANES]]
    k_ref[0] = jnp.concatenate(parts, axis=1).astype(BF16)

    vt = lax.dot_general(wvt_ref[...], hb, (((1,), (1,)), ((), ())),
                         preferred_element_type=F32).astype(BF16)
    for hd in range(n_heads):
        vt_ref[0, hd, 0:HEAD_DIM, :] = vt[hd * HEAD_DIM:(hd + 1) * HEAD_DIM, :]
        vt_ref[0, hd, HEAD_DIM:HEAD_DIM + ONES_ROWS, :] = jnp.ones((ONES_ROWS, tile), BF16)


def _kvf(h3, w_kvf, b_f, pk, ones_k, tile=512):
    bsz, seq, d = h3.shape
    n_h = b_f.shape[0]
    wk = w_kvf[:, :d].astype(BF16)
    wvt = w_kvf[:, d:2 * d].T.astype(BF16)
    wf = jnp.pad(w_kvf[:, 2 * d:], ((0, 0), (0, LANES - n_h)))
    bf = jnp.pad(b_f, (0, LANES - n_h)).reshape(1, LANES)
    const2 = lambda i, j: (0, 0)
    return pl.pallas_call(
        functools.partial(_kvf_kernel, tile=tile),
        grid=(bsz, seq // tile),
        in_specs=[pl.BlockSpec((1, tile, d), lambda i, j: (i, j, 0)),
                  pl.BlockSpec((d, d), const2), pl.BlockSpec((d, d), const2),
                  pl.BlockSpec((d, LANES), const2), pl.BlockSpec((1, LANES), const2),
                  pl.BlockSpec(pk.shape, lambda i, j: (0, 0, 0)),
                  pl.BlockSpec(ones_k.shape, const2)],
        out_specs=[pl.BlockSpec((1, tile, 2 * d), lambda i, j: (i, j, 0)),
                   pl.BlockSpec((1, n_h, HEAD_DIM + ONES_ROWS, tile), lambda i, j: (i, 0, 0, j)),
                   pl.BlockSpec((1, tile, F_PIECES * LANES), lambda i, j: (i, j, 0))],
        out_shape=[jax.ShapeDtypeStruct((bsz, seq, 2 * d), BF16),
                   jax.ShapeDtypeStruct((bsz, n_h, HEAD_DIM + ONES_ROWS, seq), BF16),
                   jax.ShapeDtypeStruct((bsz, seq, F_PIECES * LANES), BF16)],
        scratch_shapes=[pltpu.VMEM((1, LANES), F32)],
        compiler_params=_params(("arbitrary", "arbitrary")),
        name="kvf_proj",
    )(h3, wk, wvt, wf, bf, pk, ones_k)


def _qproj_kernel(h_ref, wq_ref, fp_ref, pq_ref, onesq_ref, q_ref, *, scale):
    n_pairs = q_ref.shape[1] // (2 * LANES)
    q = jnp.dot(h_ref[...].astype(BF16), wq_ref[...], preferred_element_type=F32) * scale
    qf = onesq_ref[...]
    for j in range(F_PIECES):
        qf = qf + jnp.dot(fp_ref[:, j * LANES:(j + 1) * LANES], pq_ref[j],
                          preferred_element_type=F32)
    parts = []
    for p in range(n_pairs):
        parts += [q[:, p * LANES:(p + 1) * LANES], qf[:, p * LANES:(p + 1) * LANES]]
    q_ref[...] = jnp.concatenate(parts, axis=1).astype(BF16)


def _qproj(h, w_q, fpieces, pq, ones_q, scale, tile=1024):
    n, d = h.shape
    n_h = d // HEAD_DIM
    const2 = lambda i: (0, 0)
    return pl.pallas_call(
        functools.partial(_qproj_kernel, scale=scale),
        grid=(n // tile,),
        in_specs=[pl.BlockSpec((tile, d), lambda i: (i, 0)),
                  pl.BlockSpec((d, d), const2),
                  pl.BlockSpec((tile, F_PIECES * LANES), lambda i: (i, 0)),
                  pl.BlockSpec(pq.shape, lambda i: (0, 0, 0)),
                  pl.BlockSpec(ones_q.shape, const2)],
        out_specs=pl.BlockSpec((tile, n_h * LANES), lambda i: (i, 0)),
        out_shape=jax.ShapeDtypeStruct((n, n_h * LANES), BF16),
        compiler_params=_params(("arbitrary",)),
        name="q_proj",
    )(h, w_q.astype(BF16), fpieces, pq, ones_q)


def _attn_kernel(q_ref, k_ref, vt_ref, o_ref, sa_ref, sb_ref, *, tq, tk):
    i = pl.program_id(2)
    per_q = tq // tk
    qa = q_ref[0]
    lane = lax.broadcasted_iota(jnp.int32, (1, 2 * LANES), 1)

    def lanes_in(start, size):
        return jnp.logical_and(lane >= start, lane < start + size)

    qs = []
    for hd in range(2):
        own = jnp.logical_or(
            lanes_in(hd * HEAD_DIM, HEAD_DIM),
            jnp.logical_or(lanes_in(LANES + hd * F_PIECES, F_PIECES),
                           lanes_in(LANES + (2 + hd) * F_PIECES, F_PIECES)))
        qs.append(jnp.where(own, qa, jnp.zeros_like(qa)))
    row = lax.broadcasted_iota(jnp.int32, (tk, tq), 0)
    col = lax.broadcasted_iota(jnp.int32, (tk, tq), 1)

    def logits(j, s_ref):
        kj = k_ref[0, pl.ds(pl.multiple_of(j * tk, tk), tk), :]
        for hd in range(2):
            s_ref[hd] = lax.dot_general(kj, qs[hd], (((1,), (1,)), ((), ())),
                                        preferred_element_type=F32)

    def softmax_pv(j, s_ref, carry, diag_offset):
        ms, ls, accs = carry
        keys = pl.ds(pl.multiple_of(j * tk, tk), tk)
        new_ms, new_ls, new_accs = [], [], []
        for hd in range(2):
            sc = s_ref[hd]
            if diag_offset is not None:
                sc = jnp.where(row + diag_offset <= col, sc, NEG_INF)
            m_new = jnp.maximum(ms[hd], jnp.max(sc, axis=0, keepdims=True))
            pt = jnp.exp(sc - m_new).astype(BF16)
            alpha = jnp.exp(ms[hd] - m_new)
            r = jnp.dot(vt_ref[0, hd, :, keys], pt,
                        preferred_element_type=F32)
            new_ms.append(m_new)
            new_ls.append(alpha * ls[hd] + r[HEAD_DIM:HEAD_DIM + 1, :])
            new_accs.append(alpha * accs[hd] + r[:HEAD_DIM, :])
        return tuple(new_ms), tuple(new_ls), tuple(new_accs)

    def two_blocks(t, carry):
        j = t * 2
        logits(j + 1, sb_ref)
        carry = softmax_pv(j, sa_ref, carry, None)
        logits(j + 2, sa_ref)
        return softmax_pv(j + 1, sb_ref, carry, None)

    init = ((jnp.full((1, tq), NEG_INF, F32),) * 2, (jnp.zeros((1, tq), F32),) * 2,
            (jnp.zeros((HEAD_DIM, tq), F32),) * 2)
    logits(0, sa_ref)
    first_diag = i * per_q
    carry = lax.fori_loop(0, first_diag // 2, two_blocks, init)
    bufs = (sa_ref, sb_ref)
    for u in range(per_q):
        if u + 1 < per_q:
            logits(first_diag + u + 1, bufs[(u + 1) % 2])
        carry = softmax_pv(first_diag + u, bufs[u % 2], carry, u * tk)
    _, ls, accs = carry
    out_t = jnp.concatenate([accs[0] / ls[0], accs[1] / ls[1]], axis=0)
    o_ref[0] = out_t.T.astype(BF16)


def _attention(qaug, kaug, vaug_t, d, tq=512, tk=256):
    bsz, seq, _ = kaug.shape
    n_pairs = vaug_t.shape[1] // 2
    assert (tq // tk) % 2 == 0
    return pl.pallas_call(
        functools.partial(_attn_kernel, tq=tq, tk=tk),
        grid=(bsz, n_pairs, seq // tq),
        in_specs=[pl.BlockSpec((1, tq, 2 * LANES), lambda b, p, i: (b, i, p)),
                  pl.BlockSpec((1, seq, 2 * LANES), lambda b, p, i: (b, 0, p)),
                  pl.BlockSpec((1, 2, HEAD_DIM + ONES_ROWS, seq), lambda b, p, i: (b, p, 0, 0))],
        out_specs=pl.BlockSpec((1, tq, LANES), lambda b, p, i: (b, i, p)),
        out_shape=jax.ShapeDtypeStruct((bsz, seq, d), BF16),
        scratch_shapes=[pltpu.VMEM((2, tk, tq), F32), pltpu.VMEM((2, tk, tq), F32)],
        compiler_params=_params(("arbitrary", "arbitrary", "arbitrary")),
        name="fox_attention",
    )(qaug, kaug, vaug_t)


def _oproj_ln_kernel(o_ref, h_ref, wo_ref, g_ref, b_ref, out_ref, *, alpha):
    mix = jnp.dot(o_ref[...], wo_ref[...], preferred_element_type=F32)
    out_ref[...] = _layer_norm(alpha * h_ref[...] + mix, g_ref[...], b_ref[...])


def _oproj_ln(o, h, w_o, g, b, alpha, tile=1024):
    n, d = h.shape
    vec = pl.BlockSpec((1, d), lambda i: (0, 0))
    tspec = pl.BlockSpec((tile, d), lambda i: (i, 0))
    return pl.pallas_call(
        functools.partial(_oproj_ln_kernel, alpha=alpha),
        grid=(n // tile,),
        in_specs=[tspec, tspec, pl.BlockSpec((d, d), lambda i: (0, 0)), vec, vec],
        out_specs=tspec,
        out_shape=jax.ShapeDtypeStruct((n, d), F32),
        compiler_params=_params(("arbitrary",)),
        name="oproj_ln",
    )(o, h, w_o.astype(BF16), g.reshape(1, d), b.reshape(1, d))


def kernel(x, pool_w, pool_scale, w_q, w_o, w_kvf, b_f, ln_g, ln_b, router_w, router_b,
           w_gate_up, w_down, ws_gate_up, ws_down):
    bsz, seq, d = x.shape
    depth = ln_g.shape[0]
    n_pool = pool_w.shape[0]
    alpha = float((2 * depth) ** 0.25)
    n = bsz * seq
    kaug = vaug_t = fpieces = None
    pk, ones_k, pq, ones_q = _gate_placement(d // HEAD_DIM)
    for l in range(depth):
        if l < n_pool:
            x = _pool_ln(x, pool_w[l], pool_scale[l], ln_g[l, 0], ln_b[l, 0], alpha)
        else:
            if l == n_pool:
                kaug, vaug_t, fpieces = _kvf(x, w_kvf, b_f, pk, ones_k)
            j = l - n_pool
            qaug = _qproj(x.reshape(n, d), w_q[j], fpieces.reshape(n, -1), pq, ones_q,
                          HEAD_DIM ** -0.5)
            o = _attention(qaug.reshape(bsz, seq, -1), kaug, vaug_t, d)
            x = _oproj_ln(o.reshape(n, d), x.reshape(n, d), w_o[j], ln_g[l, 0], ln_b[l, 0],
                          alpha).reshape(bsz, seq, d)
        x = _moe_ln(x.reshape(n, d), router_w[l], router_b[l], w_gate_up, w_down, l,
                    ws_gate_up[l], ws_down[l], ln_g[l, 1], ln_b[l, 1], alpha).reshape(bsz, seq, d)
    return x
```
